```python
import functools
import jax
import jax.numpy as jnp
from jax import lax
import numpy as np

D_MODEL = 1024
BATCH = 2
SEQ = 8192
DEPTH = 2
DEC_BATCH = 32
DEC_SEQ = 8
PAST_LEN = 8192
PAGE_SIZE = 128

N_BRANCH = 3
BRANCH_W = 512
CONV_W = BRANCH_W
CONV_K = 31
HG_HEADS = 4
HG_DK = 128
HG_DV = 128
HG_CHUNK = 64
ATT_HEADS = 8
ATT_HD = 64
Q_BLOCK = 128
PEER_HEADS = 8
N_KEYS = 128
N_EXPERTS = N_KEYS * N_KEYS
PEER_DK = 256
PEER_TOPK = 16
PEER_BLOCK = 128
FORGET_BIAS_INIT = 3.0
EPS = 1e-6
GATE_FLOOR = 1e-20
MASK_VALUE = -1e30
IN_SIZES = (2 * CONV_W, HG_HEADS * HG_DK, HG_HEADS * HG_DK, HG_HEADS * HG_DV, HG_HEADS * HG_DV,
            ATT_HEADS * ATT_HD, ATT_HEADS * ATT_HD, ATT_HEADS * ATT_HD, ATT_HEADS, N_BRANCH * D_MODEL)
IN_COLS = sum(IN_SIZES)

kernel_name = 'hybrid_conv_hgrn2_fox_peer_step'


def _rmsnorm(x, g):
    xf = x.astype(jnp.float32)
    y = xf * lax.rsqrt(jnp.mean(xf * xf, axis=-1, keepdims=True) + EPS)
    return (y * g.astype(jnp.float32)).astype(x.dtype)


def _split_cols(z):
    return jnp.split(z, np.cumsum(IN_SIZES)[:-1].tolist(), axis=-1)


def _conv_branch(a, buf, w_dw, b_dw, ln_g, ln_b):
    u = a[..., :CONV_W] * jax.nn.sigmoid(a[..., CONV_W:])
    ext = jnp.concatenate([buf.astype(u.dtype), u], axis=1)
    y = lax.conv_general_dilated(ext, w_dw[:, None, :].astype(u.dtype), (1,), 'VALID',
                                 dimension_numbers=('NWC', 'WIO', 'NWC'),
                                 feature_group_count=CONV_W)
    y = y.astype(jnp.float32) + b_dw.astype(jnp.float32)
    mu = jnp.mean(y, axis=-1, keepdims=True)
    var = jnp.mean(jnp.square(y - mu), axis=-1, keepdims=True)
    y = (y - mu) * lax.rsqrt(var + EPS) * ln_g.astype(jnp.float32) + ln_b.astype(jnp.float32)
    return jax.nn.silu(y).astype(a.dtype), ext[:, -(CONV_K - 1):]


def _hgrn_lower_bounds(logits):
    p = jax.nn.softmax(logits.astype(jnp.float32), axis=0)
    return jnp.cumsum(p, axis=0) - p[0:1]


def _hgrn_branch(q, fpre, i, g, s0, lb, norm_g):
    B, L, _ = q.shape
    f32 = jnp.float32
    qh = jax.nn.silu(q.astype(f32)).reshape(B, L, HG_HEADS, HG_DK)
    lbf = jnp.clip(lb.astype(f32), 0.0, 1.0)
    f = lbf + (1.0 - lbf) * jax.nn.sigmoid(fpre.astype(f32))
    kh = (1.0 - f).reshape(B, L, HG_HEADS, HG_DK)
    logf = jnp.log(jnp.maximum(f, GATE_FLOOR)).reshape(B, L, HG_HEADS, HG_DK)
    vh = i.astype(f32).reshape(B, L, HG_HEADS, HG_DV)
    c = HG_CHUNK if L % HG_CHUNK == 0 else L
    n = L // c
    causal = jnp.tril(jnp.ones((c, c), bool))[None, :, :, None, None]

    def to_chunks(t):
        return t.reshape(B, n, c, *t.shape[2:]).swapaxes(0, 1)

    def step(S, inp):
        qc, kc, vc, lfc = inp
        b = jnp.cumsum(lfc, axis=1)
        o_inter = jnp.einsum('bthk,bhkv->bthv', qc * jnp.exp(b), S)
        decay = jnp.exp(jnp.where(causal, b[:, :, None] - b[:, None, :], MASK_VALUE))
        A = jnp.einsum('bthk,bshk,btshk->bths', qc, kc, decay)
        o = o_inter + jnp.einsum('bths,bshv->bthv', A, vc)
        b_last = b[:, -1]
        S = jnp.exp(b_last)[..., None] * S + jnp.einsum('bshk,bshv->bhkv', kc * jnp.exp(b_last[:, None] - b), vc)
        return S, o

    S, o = lax.scan(step, s0.astype(f32), (to_chunks(qh), to_chunks(kh), to_chunks(vh), to_chunks(logf)))
    o = o.swapaxes(0, 1).reshape(B, L, HG_HEADS, HG_DV)
    o = o * lax.rsqrt(jnp.mean(o * o, axis=-1, keepdims=True) + EPS) * norm_g.astype(f32)
    o = o.reshape(B, L, HG_HEADS * HG_DV) * jax.nn.sigmoid(g.astype(f32))
    return o.astype(q.dtype), S.astype(s0.dtype)


def _fox_prompt(q, k, v, logf):
    B, S, H, HD = q.shape
    nblk = S // Q_BLOCK
    c = jnp.cumsum(logf, axis=1).transpose(0, 2, 1)
    qb = q.reshape(B, nblk, Q_BLOCK, H, HD).swapaxes(0, 1)
    kpos = jnp.arange(S)
    scale = ATT_HD ** -0.5

    def block(args):
        qi, bi = args
        qpos = bi * Q_BLOCK + jnp.arange(Q_BLOCK)
        cq = lax.dynamic_slice_in_dim(c, bi * Q_BLOCK, Q_BLOCK, axis=2)
        s = jnp.einsum('bqhd,bkhd->bhqk', qi, k).astype(jnp.float32) * scale
        s = s + (cq[..., None] - c[:, :, None, :])
        s = jnp.where(kpos[None, :] <= qpos[:, None], s, MASK_VALUE)
        p = jax.nn.softmax(s, axis=-1).astype(v.dtype)
        return jnp.einsum('bhqk,bkhd->bqhd', p, v)

    o = lax.map(block, (qb, jnp.arange(nblk)))
    return o.swapaxes(0, 1).reshape(B, S, H, HD)


def _fox_sample(q, k, v, logf, ck, cv, clf, page_table):
    DB, T, H, HD = q.shape
    kp = ck[page_table].reshape(DB, -1, H, HD)
    vp = cv[page_table].reshape(DB, -1, H, HD)
    lfp = clf[page_table].reshape(DB, -1, H).astype(jnp.float32)
    P = kp.shape[1]
    c_all = jnp.cumsum(jnp.concatenate([lfp, logf], axis=1), axis=1).transpose(0, 2, 1)
    bias = c_all[:, :, P:, None] - c_all[:, :, None, :]
    scale = ATT_HD ** -0.5
    s = jnp.concatenate([jnp.einsum('bqhd,bkhd->bhqk', q, kp),
                         jnp.einsum('bqhd,bkhd->bhqk', q, k)], axis=-1).astype(jnp.float32) * scale + bias
    mask = jnp.concatenate([jnp.ones((T, P), bool), jnp.tril(jnp.ones((T, T), bool))], axis=1)
    p = jax.nn.softmax(jnp.where(mask, s, MASK_VALUE), axis=-1).astype(v.dtype)
    return jnp.einsum('bhqk,bkhd->bqhd', p[..., :P], vp) + jnp.einsum('bhqk,bkhd->bqhd', p[..., P:], v)


def _peer(x, wq, keys, u, v):
    B, L, D = x.shape
    n = B * L
    xt = jnp.pad(x.reshape(n, D), ((0, (-n) % PEER_BLOCK), (0, 0)))
    kk = PEER_TOPK * PEER_TOPK

    def block(xb):
        q = jnp.einsum('td,dc->tc', xb, wq).reshape(-1, PEER_HEADS, 2, PEER_DK // 2)
        s = jnp.einsum('thpd,hpnd->thpn', q, keys).astype(jnp.float32)
        sv, si = lax.top_k(s, PEER_TOPK)
        cand = (sv[:, :, 0, :, None] + sv[:, :, 1, None, :]).reshape(-1, PEER_HEADS, kk)
        cidx = (si[:, :, 0, :, None] * N_KEYS + si[:, :, 1, None, :]).reshape(-1, PEER_HEADS, kk)
        top_s, pos = lax.top_k(cand, PEER_TOPK)
        eidx = jnp.take_along_axis(cidx, pos, axis=-1)
        w = jax.nn.softmax(top_s, axis=-1)
        hid = jnp.einsum('td,thkd->thk', xb, u[eidx]).astype(jnp.float32)
        act = (jax.nn.gelu(hid, approximate=False) * w).astype(xb.dtype)
        return jnp.einsum('thk,thkd->td', act, v[eidx])

    out = lax.map(block, xt.reshape(-1, PEER_BLOCK, D))
    return out.reshape(-1, D)[:n].reshape(B, L, D)


def _layer(x, conv_buf, hg_state, attn_fn, l, lb, norm1_g, w_in, conv_dw, conv_db, conv_ln_g, conv_ln_b,
           hg_norm_g, att_fb, w_branch, w_out, norm2_g, peer_wq, peer_keys, peer_u, peer_v):
    B, L, _ = x.shape
    xn = _rmsnorm(x, norm1_g[l])
    z = jnp.einsum('bld,dc->blc', xn, w_in[l])
    a_conv, hq, hf, hi, hg, aq, ak, av, af, gates = _split_cols(z)
    y_conv, conv_new = _conv_branch(a_conv, conv_buf, conv_dw[l], conv_db[l], conv_ln_g[l], conv_ln_b[l])
    y_hgrn, hg_new = _hgrn_branch(hq, hf, hi, hg, hg_state, lb, hg_norm_g[l])
    k_rows = ak.reshape(B, L, ATT_HEADS, ATT_HD)
    v_rows = av.reshape(B, L, ATT_HEADS, ATT_HD)
    logf = jax.nn.log_sigmoid(af.astype(jnp.float32) + att_fb[l].astype(jnp.float32))
    y_att = attn_fn(aq.reshape(B, L, ATT_HEADS, ATT_HD), k_rows, v_rows, logf)
    branches = jnp.stack([y_conv, y_hgrn, y_att.reshape(B, L, BRANCH_W).astype(x.dtype)], axis=0)
    proj = jnp.einsum('nblw,nwd->nbld', branches, w_branch[l])
    gate = jax.nn.sigmoid(gates.reshape(B, L, N_BRANCH, D_MODEL))
    merged = jnp.einsum('blnd,nbld->bld', gate, proj)
    h = x + jnp.einsum('bld,de->ble', merged, w_out[l])
    h = h + _peer(_rmsnorm(h, norm2_g[l]), peer_wq[l], peer_keys[l], peer_u[l], peer_v[l])
    return h, conv_new, hg_new, k_rows, v_rows, logf


def setup_inputs(seed: int = 0) -> dict:
    key = jax.random.key(seed)
    ks = jax.random.split(key, 32)
    f32 = jnp.float32
    n_pages = PAST_LEN // PAGE_SIZE
    n_phys = (DEC_BATCH * n_pages * 5) // 4

    def nrm(k, shape, s):
        return s * jax.random.normal(k, shape, f32)

    inp = {}
    inp['x_prompt'] = nrm(ks[0], (BATCH, SEQ, D_MODEL), 1.0)
    inp['x_sample'] = nrm(ks[1], (DEC_BATCH, DEC_SEQ, D_MODEL), 1.0)
    inp['cache_k'] = nrm(ks[2], (DEPTH, n_phys, PAGE_SIZE, ATT_HEADS, ATT_HD), 1.0)
    inp['cache_v'] = nrm(ks[3], (DEPTH, n_phys, PAGE_SIZE, ATT_HEADS, ATT_HD), 1.0)
    inp['cache_logf'] = jax.nn.log_sigmoid(FORGET_BIAS_INIT + nrm(ks[4], (DEPTH, n_phys, PAGE_SIZE, ATT_HEADS), 1.0))
    inp['state_conv'] = nrm(ks[5], (DEPTH, DEC_BATCH, CONV_K - 1, CONV_W), 0.5)
    inp['state_hgrn'] = nrm(ks[6], (DEPTH, DEC_BATCH, HG_HEADS, HG_DK, HG_DV), 0.5)
    inp['page_table'] = jax.random.permutation(ks[7], n_phys)[:DEC_BATCH * n_pages].reshape(DEC_BATCH, n_pages).astype(jnp.int32)
    inp['norm1_g'] = 1.0 + nrm(ks[8], (DEPTH, D_MODEL), 0.01)
    inp['w_in'] = nrm(ks[9], (DEPTH, D_MODEL, IN_COLS), D_MODEL ** -0.5)
    inp['conv_dw'] = nrm(ks[10], (DEPTH, CONV_K, CONV_W), CONV_K ** -0.5)
    inp['conv_db'] = nrm(ks[11], (DEPTH, CONV_W), 0.01)
    inp['conv_ln_g'] = 1.0 + nrm(ks[12], (DEPTH, CONV_W), 0.01)
    inp['conv_ln_b'] = nrm(ks[13], (DEPTH, CONV_W), 0.01)
    inp['hg_lb_logits'] = nrm(ks[14], (DEPTH, HG_HEADS * HG_DK), 0.1)
    inp['hg_norm_g'] = 1.0 + nrm(ks[15], (DEPTH, HG_DV), 0.01)
    inp['att_fb'] = FORGET_BIAS_INIT + nrm(ks[16], (DEPTH, ATT_HEADS), 0.1)
    inp['w_branch'] = nrm(ks[17], (DEPTH, N_BRANCH, BRANCH_W, D_MODEL), BRANCH_W ** -0.5)
    inp['w_out'] = nrm(ks[18], (DEPTH, D_MODEL, D_MODEL), D_MODEL ** -0.5)
    inp['norm2_g'] = 1.0 + nrm(ks[19], (DEPTH, D_MODEL), 0.01)
    inp['peer_wq'] = nrm(ks[20], (DEPTH, D_MODEL, PEER_HEADS * PEER_DK), D_MODEL ** -0.5)
    inp['peer_keys'] = nrm(ks[21], (DEPTH, PEER_HEADS, 2, N_KEYS, PEER_DK // 2), (PEER_DK // 2) ** -0.5)
    inp['peer_u'] = nrm(ks[22], (DEPTH, N_EXPERTS, D_MODEL), D_MODEL ** -0.5)
    inp['peer_v'] = nrm(ks[23], (DEPTH, N_EXPERTS, D_MODEL), 0.5 * PEER_HEADS ** -0.5)
    inp['final_g'] = 1.0 + nrm(ks[24], (D_MODEL,), 0.01)
    return inp


def reference(x_prompt, x_sample, cache_k, cache_v, cache_logf, state_conv, state_hgrn, page_table,
              norm1_g, w_in, conv_dw, conv_db, conv_ln_g, conv_ln_b, hg_lb_logits, hg_norm_g, att_fb,
              w_branch, w_out, norm2_g, peer_wq, peer_keys, peer_u, peer_v, final_g):
    lbs = _hgrn_lower_bounds(hg_lb_logits)
    layer_w = (norm1_g, w_in, conv_dw, conv_db, conv_ln_g, conv_ln_b, hg_norm_g, att_fb,
               w_branch, w_out, norm2_g, peer_wq, peer_keys, peer_u, peer_v)
    hp, hs = x_prompt, x_sample
    bp = x_prompt.shape[0]
    rp_all, rs_all = [], []
    for l in range(DEPTH):
        buf0 = jnp.zeros((bp, CONV_K - 1, CONV_W), x_prompt.dtype)
        s0 = jnp.zeros((bp, HG_HEADS, HG_DK, HG_DV), x_prompt.dtype)
        hp, *rp = _layer(hp, buf0, s0, _fox_prompt, l, lbs[l], *layer_w)
        attn_s = functools.partial(_fox_sample, ck=cache_k[l], cv=cache_v[l], clf=cache_logf[l], page_table=page_table)
        hs, *rs = _layer(hs, state_conv[l], state_hgrn[l], attn_s, l, lbs[l], *layer_w)
        rp_all.append(rp)
        rs_all.append(rs)
    y_prompt = _rmsnorm(hp, final_g)
    y_sample = _rmsnorm(hs, final_g)
    new_conv_prompt = jnp.stack([r[0] for r in rp_all])
    new_hgrn_prompt = jnp.stack([r[1] for r in rp_all])
    new_k_prompt = jnp.stack([r[2] for r in rp_all])
    new_v_prompt = jnp.stack([r[3] for r in rp_all])
    new_logf_prompt = jnp.stack([r[4] for r in rp_all])
    new_conv_sample = jnp.stack([r[0] for r in rs_all])
    new_hgrn_sample = jnp.stack([r[1] for r in rs_all])
    new_k_sample = jnp.stack([r[2] for r in rs_all])
    new_v_sample = jnp.stack([r[3] for r in rs_all])
    new_logf_sample = jnp.stack([r[4] for r in rs_all])
    return (y_prompt, y_sample, new_k_prompt, new_v_prompt, new_logf_prompt, new_conv_prompt, new_hgrn_prompt,
            new_k_sample, new_v_sample, new_logf_sample, new_conv_sample, new_hgrn_sample)
```

```python
import functools

import jax
import jax.numpy as jnp
from jax import lax
from jax.experimental import pallas as pl
from jax.experimental.pallas import tpu as pltpu

F32 = jnp.float32
BF16 = jnp.bfloat16
HI = lax.Precision.HIGHEST
SDS = jax.ShapeDtypeStruct

D_MODEL = 1024
BRANCH_W = 512
CONV_K = 31
HG_HEADS = 4
HG_DK = 128
ATT_HEADS = 8
ATT_HD = 64
PEER_HEADS = 8
N_KEYS = 128
PEER_TOPK = 16
PAGE_SIZE = 128
EPS = 1e-6
GATE_FLOOR = 1e-20
MASK_VALUE = -1e30
LANES = 128
_HD_SHIFT = ATT_HD.bit_length() - 1
_HEAD_SHIFT = ATT_HEADS.bit_length() - 1

Z_COLS = 15 * BRANCH_W
ZB_CONV = 3
ZB_HQ, ZB_HF, ZB_HI, ZB_HG, ZB_AQ, ZB_AK, ZB_AV = 8, 9, 10, 11, 12, 13, 14

_PAIRS = tuple((a, b) for a in range(PEER_TOPK) for b in range(PEER_TOPK) if (a + 1) * (b + 1) <= PEER_TOPK)

_NT = (((1,), (1,)), ((), ()))
_TN = (((0,), (0,)), ((), ()))


def _params(*sem, vmem_mb=48):
    return pltpu.CompilerParams(dimension_semantics=sem, vmem_limit_bytes=vmem_mb * 1024 * 1024)


def _sigmoid(x):
    return 1.0 / (1.0 + jnp.exp(-x))


def _log_sigmoid(x):
    return jnp.minimum(x, 0.0) - jnp.log1p(jnp.exp(-jnp.abs(x)))


def _rms(x, g):
    return x * lax.rsqrt(jnp.mean(x * x, axis=-1, keepdims=True) + EPS) * g


def _in_proj_kernel(x_ref, g_ref, w_ref, wf_ref, fb_ref, z_ref, lf_ref, xn_ref):
    @pl.when(pl.program_id(1) == 0)
    def _():
        xn = _rms(x_ref[...], g_ref[...]).astype(BF16)
        xn_ref[...] = xn
        af = jnp.dot(xn, wf_ref[...], preferred_element_type=F32) + fb_ref[...]
        lf_ref[...] = _log_sigmoid(af)

    z_ref[...] = jnp.dot(xn_ref[...], w_ref[...], preferred_element_type=F32)


def _in_proj(x, g, w_main, w_f, fb, tm):
    n, d = x.shape
    tn = BRANCH_W
    return pl.pallas_call(
        _in_proj_kernel,
        grid=(n // tm, Z_COLS // tn),
        in_specs=[
            pl.BlockSpec((tm, d), lambda i, j: (i, 0)),
            pl.BlockSpec((1, d), lambda i, j: (0, 0)),
            pl.BlockSpec((d, tn), lambda i, j: (0, j)),
            pl.BlockSpec((d, LANES), lambda i, j: (0, 0)),
            pl.BlockSpec((1, LANES), lambda i, j: (0, 0)),
        ],
        out_specs=[
            pl.BlockSpec((tm, tn), lambda i, j: (i, j)),
            pl.BlockSpec((tm, LANES), lambda i, j: (i, 0)),
        ],
        out_shape=[SDS((n, Z_COLS), F32), SDS((n, LANES), F32)],
        scratch_shapes=[pltpu.VMEM((tm, d), BF16)],
        compiler_params=_params("arbitrary", "arbitrary"),
        name="in_proj",
    )(x, g, w_main, w_f, fb)


_CONV_PAD = 32


def _conv_kernel(a_ref, buf_ref, w_ref, b_ref, g_ref, bt_ref, y_ref, nc_ref, ext_ref, *, tl):
    hist = CONV_K - 1
    lo = _CONV_PAD - hist

    @pl.when(pl.program_id(1) == 0)
    def _():
        ext_ref[lo:_CONV_PAD, :] = buf_ref[0]

    a = a_ref[...]
    ext_ref[_CONV_PAD:_CONV_PAD + tl, :] = a[:, :BRANCH_W] * _sigmoid(a[:, BRANCH_W:])
    acc = jnp.zeros((tl, BRANCH_W), F32)
    for j in range(CONV_K):
        acc = acc + w_ref[j:j + 1, :] * ext_ref[lo + j:lo + j + tl, :]
    y = acc + b_ref[...]
    mu = jnp.mean(y, axis=-1, keepdims=True)
    yc = y - mu
    var = jnp.mean(yc * yc, axis=-1, keepdims=True)
    y = yc * lax.rsqrt(var + EPS) * g_ref[...] + bt_ref[...]
    y_ref[...] = y * _sigmoid(y)
    tail = ext_ref[tl + lo:tl + _CONV_PAD, :]
    nc_ref[0] = tail
    ext_ref[lo:_CONV_PAD, :] = tail


def _conv_branch(z, buf, w, b, g, bt, nb, seq, tl):
    nt = seq // tl
    hist = CONV_K - 1
    vec = pl.BlockSpec((1, BRANCH_W), lambda i, t: (0, 0))
    return pl.pallas_call(
        functools.partial(_conv_kernel, tl=tl),
        grid=(nb, nt),
        in_specs=[
            pl.BlockSpec((tl, 2 * BRANCH_W), lambda i, t: (i * nt + t, ZB_CONV)),
            pl.BlockSpec((1, hist, BRANCH_W), lambda i, t: (i, 0, 0)),
            pl.BlockSpec((CONV_K, BRANCH_W), lambda i, t: (0, 0)),
            vec, vec, vec,
        ],
        out_specs=[
            pl.BlockSpec((tl, BRANCH_W), lambda i, t: (i * nt + t, 0)),
            pl.BlockSpec((1, hist, BRANCH_W), lambda i, t: (i, 0, 0)),
        ],
        out_shape=[SDS((nb * seq, BRANCH_W), F32), SDS((nb, hist, BRANCH_W), F32)],
        scratch_shapes=[pltpu.VMEM((_CONV_PAD + tl, BRANCH_W), F32)],
        compiler_params=_params("arbitrary", "arbitrary"),
        name="conv_branch",
    )(z, buf, w, b, g, bt)


def _hgrn_kernel(q_ref, f_ref, i_ref, g_ref, s0_ref, lb_ref, ng_ref, y_ref, sn_ref,
                 st_ref, qs_ref, ks_ref, bs_ref, *, tl, c):
    t = pl.program_id(1)

    @pl.when(t == 0)
    def _():
        for h in range(HG_HEADS):
            st_ref[h] = s0_ref[0, h].T

    lb = jnp.clip(lb_ref[...], 0.0, 1.0)
    f = lb + (1.0 - lb) * _sigmoid(f_ref[...])
    lf = jnp.log(jnp.maximum(f, GATE_FLOOR))
    q = q_ref[...]
    qs_ref[...] = q * _sigmoid(q)
    ks_ref[...] = 1.0 - f
    r = lax.broadcasted_iota(jnp.int32, (tl, tl), 0)
    cc = lax.broadcasted_iota(jnp.int32, (tl, tl), 1)
    tri = jnp.where(cc <= r, jnp.where(cc >= (r & -c), 1.0, 0.0), 0.0).astype(F32)
    bs_ref[...] = jnp.dot(tri, lf, precision=HI, preferred_element_type=F32)
    rowid = lax.broadcasted_iota(jnp.int32, (c, 1), 0)

    def chunk(ci, carry):
        off = pl.multiple_of(ci * c, c)
        for h in range(HG_HEADS):
            sl = slice(h * HG_DK, (h + 1) * HG_DK)
            qc = qs_ref[pl.ds(off, c), sl]
            kc = ks_ref[pl.ds(off, c), sl]
            bc = bs_ref[pl.ds(off, c), sl]
            vc = i_ref[pl.ds(off, c), sl]
            o = jnp.zeros((c, HG_DK), F32)
            for s in range(c):
                k_row = kc[s:s + 1, :]
                b_row = bc[s:s + 1, :]
                v_row = vc[s:s + 1, :]
                p = qc * k_row * jnp.exp(jnp.minimum(bc - b_row, 0.0))
                a = jnp.sum(p, axis=1, keepdims=True)
                o = o + jnp.where(rowid >= s, a, 0.0) * v_row
            st = st_ref[h]
            qt = (qc * jnp.exp(bc)).astype(BF16)
            o = o + lax.dot_general(qt, st.astype(BF16), _NT, preferred_element_type=F32)
            b_last = bc[c - 1:c, :]
            kt = (kc * jnp.exp(b_last - bc)).astype(BF16)
            upd = lax.dot_general(vc.astype(BF16), kt, _TN, preferred_element_type=F32)
            st_ref[h] = st * jnp.exp(b_last) + upd
            o = _rms(o, ng_ref[...])
            y_ref[pl.ds(off, c), sl] = o * _sigmoid(g_ref[pl.ds(off, c), sl])
        return carry

    lax.fori_loop(0, tl // c, chunk, 0)

    @pl.when(t == pl.num_programs(1) - 1)
    def _():
        for h in range(HG_HEADS):
            sn_ref[0, h] = st_ref[h].T


def _hgrn_branch(z, s0, lb, ng, nb, seq, tl, c):
    nt = seq // tl
    w = HG_HEADS * HG_DK

    def col(j):
        return pl.BlockSpec((tl, w), lambda i, t: (i * nt + t, j))

    st_spec = pl.BlockSpec((1, HG_HEADS, HG_DK, HG_DK), lambda i, t: (i, 0, 0, 0))
    return pl.pallas_call(
        functools.partial(_hgrn_kernel, tl=tl, c=c),
        grid=(nb, nt),
        in_specs=[col(ZB_HQ), col(ZB_HF), col(ZB_HI), col(ZB_HG), st_spec,
                  pl.BlockSpec((1, w), lambda i, t: (0, 0)),
                  pl.BlockSpec((1, HG_DK), lambda i, t: (0, 0))],
        out_specs=[pl.BlockSpec((tl, w), lambda i, t: (i * nt + t, 0)), st_spec],
        out_shape=[SDS((nb * seq, w), F32), SDS((nb, HG_HEADS, HG_DK, HG_DK), F32)],
        scratch_shapes=[pltpu.VMEM((HG_HEADS, HG_DK, HG_DK), F32),
                        pltpu.VMEM((tl, w), F32), pltpu.VMEM((tl, w), F32), pltpu.VMEM((tl, w), F32)],
        compiler_params=_params("arbitrary", "arbitrary"),
        name="hgrn_branch",
    )(z, z, z, z, s0, lb, ng)


def _cumsum_kernel(lf_ref, ct_ref, carry_ref, *, tl):
    @pl.when(pl.program_id(1) == 0)
    def _():
        carry_ref[...] = jnp.zeros_like(carry_ref)

    r = lax.broadcasted_iota(jnp.int32, (tl, tl), 0)
    cc = lax.broadcasted_iota(jnp.int32, (tl, tl), 1)
    tri = jnp.where(cc <= r, 1.0, 0.0).astype(F32)
    cs = jnp.dot(tri, lf_ref[...], precision=HI, preferred_element_type=F32) + carry_ref[...]
    carry_ref[...] = cs[tl - 1:tl, :]
    ct_ref[0] = cs.T[:ATT_HEADS, :]


def _logf_cumsum(lf, nb, seq, tl):
    nt = seq // tl
    return pl.pallas_call(
        functools.partial(_cumsum_kernel, tl=tl),
        grid=(nb, nt),
        in_specs=[pl.BlockSpec((tl, LANES), lambda i, t: (i * nt + t, 0))],
        out_specs=pl.BlockSpec((1, ATT_HEADS, tl), lambda i, t: (i, 0, t)),
        out_shape=SDS((nb, ATT_HEADS, seq), F32),
        scratch_shapes=[pltpu.VMEM((1, LANES), F32)],
        compiler_params=_params("arbitrary", "arbitrary"),
        name="logf_cumsum",
    )(lf)


def _fox_kernel(q_ref, k_ref, v_ref, ck_ref, cq_ref, o_ref, qm_ref, m_ref, l_ref, acc_ref, *, t):
    pair = pl.program_id(1)
    qi = pl.program_id(2)
    ki = pl.program_id(3)
    lane = lax.broadcasted_iota(jnp.int32, (1, LANES), 1)

    @pl.when(ki == 0)
    def _():
        q = q_ref[...] * (ATT_HD ** -0.5)
        qm_ref[0] = jnp.where(lane < ATT_HD, q, 0.0).astype(BF16)
        qm_ref[1] = jnp.where(lane >= ATT_HD, q, 0.0).astype(BF16)
        m_ref[...] = jnp.full_like(m_ref, MASK_VALUE)
        l_ref[...] = jnp.zeros_like(l_ref)
        acc_ref[...] = jnp.zeros_like(acc_ref)

    def step(diagonal):
        k = k_ref[...].astype(BF16)
        v = v_ref[...].astype(BF16)
        head_iota = lax.broadcasted_iota(jnp.int32, (ATT_HEADS, 1), 0)
        for j in range(2):
            own = head_iota == 2 * pair + j
            ck = jnp.sum(jnp.where(own, ck_ref[0], 0.0), axis=0, keepdims=True)
            c0 = jnp.sum(jnp.where(own, cq_ref[0, :, 0:1], 0.0), axis=0, keepdims=True)
            s = lax.dot_general(qm_ref[j], k, _NT, preferred_element_type=F32)
            s = s + (c0 - ck)
            if diagonal:
                rid = lax.broadcasted_iota(jnp.int32, (t, t), 0)
                cid = lax.broadcasted_iota(jnp.int32, (t, t), 1)
                s = jnp.where(cid <= rid, s, MASK_VALUE)
            m_old = m_ref[j]
            m_new = jnp.maximum(m_old, jnp.max(s, axis=1, keepdims=True))
            alpha = jnp.exp(m_old - m_new)
            p = jnp.exp(s - m_new)
            l_ref[j] = alpha * l_ref[j] + jnp.sum(p, axis=1, keepdims=True)
            acc_ref[j] = alpha * acc_ref[j] + jnp.dot(p.astype(BF16), v, preferred_element_type=F32)
            m_ref[j] = m_new

    @pl.when(ki < qi)
    def _():
        step(False)

    @pl.when(ki == qi)
    def _():
        step(True)
        o0 = acc_ref[0] / l_ref[0]
        o1 = acc_ref[1] / l_ref[1]
        o_ref[...] = jnp.where(lane < ATT_HD, o0, o1)


def _fox_prompt(z, ct, nb, seq, t):
    nq = seq // t
    npair = ATT_HEADS // 2
    zq, zk, zv = (zb * (BRANCH_W // LANES) for zb in (ZB_AQ, ZB_AK, ZB_AV))
    return pl.pallas_call(
        functools.partial(_fox_kernel, t=t),
        grid=(nb, npair, nq, nq),
        in_specs=[
            pl.BlockSpec((t, LANES), lambda b, p, qi, ki: (b * nq + qi, zq + p)),
            pl.BlockSpec((t, LANES), lambda b, p, qi, ki: (b * nq + jnp.minimum(ki, qi), zk + p)),
            pl.BlockSpec((t, LANES), lambda b, p, qi, ki: (b * nq + jnp.minimum(ki, qi), zv + p)),
            pl.BlockSpec((1, ATT_HEADS, t), lambda b, p, qi, ki: (b, 0, jnp.minimum(ki, qi))),
            pl.BlockSpec((1, ATT_HEADS, t), lambda b, p, qi, ki: (b, 0, qi)),
        ],
        out_specs=pl.BlockSpec((t, LANES), lambda b, p, qi, ki: (b * nq + qi, p)),
        out_shape=SDS((nb * seq, BRANCH_W), F32),
        scratch_shapes=[pltpu.VMEM((2, t, LANES), BF16), pltpu.VMEM((2, t, 1), F32),
                        pltpu.VMEM((2, t, 1), F32), pltpu.VMEM((2, t, LANES), F32)],
        compiler_params=_params("arbitrary", "arbitrary", "arbitrary", "arbitrary"),
        name="fox_prompt",
    )(z, z, z, ct, ct)


def _fox_sample_kernel(pt_ref, q_ref, kn_ref, vn_ref, lfn_ref, kp_ref, vp_ref, lfp_ref, o_ref,
                       qb_ref, m_ref, l_ref, acc_ref, car_ref, cn_ref, *, tt):
    del pt_ref
    pg = pl.program_id(1)
    rows = tt * ATT_HEADS
    w = ATT_HEADS * ATT_HD
    rid = lax.broadcasted_iota(jnp.int32, (rows, 1), 0)
    hmask = (lax.broadcasted_iota(jnp.int32, (rows, w), 1) >> _HD_SHIFT) == (rid & (ATT_HEADS - 1))
    tok = rid >> _HEAD_SHIFT
    lane = lax.broadcasted_iota(jnp.int32, (rows, LANES), 1)
    r2 = lax.broadcasted_iota(jnp.int32, (LANES, LANES), 0)
    c2 = lax.broadcasted_iota(jnp.int32, (LANES, LANES), 1)

    def tile_rows(x):
        return jnp.broadcast_to(x[None], (tt,) + x.shape).reshape(rows, x.shape[-1])

    def online(s, v):
        m_old = m_ref[...]
        m_new = jnp.maximum(m_old, jnp.max(s, axis=1, keepdims=True))
        alpha = jnp.exp(m_old - m_new)
        p = jnp.exp(s - m_new)
        l_ref[...] = alpha * l_ref[...] + jnp.sum(p, axis=1, keepdims=True)
        acc_ref[...] = alpha * acc_ref[...] + jnp.dot(p.astype(BF16), v, preferred_element_type=F32)
        m_ref[...] = m_new

    @pl.when(pg == 0)
    def _():
        q = q_ref[...] * (ATT_HD ** -0.5)
        qrep = jnp.broadcast_to(q[:, None, :], (tt, ATT_HEADS, w)).reshape(rows, w)
        qb_ref[...] = jnp.where(hmask, qrep, 0.0).astype(BF16)
        m_ref[...] = jnp.full_like(m_ref, MASK_VALUE)
        l_ref[...] = jnp.zeros_like(l_ref)
        acc_ref[...] = jnp.zeros_like(acc_ref)
        car_ref[...] = jnp.zeros_like(car_ref)
        incl = jnp.where(r2 <= c2, 1.0, 0.0).astype(F32)
        cn_ref[...] = jnp.dot(tile_rows(lfn_ref[0]), incl, precision=HI, preferred_element_type=F32)

    cn = cn_ref[...]
    cn_own = jnp.sum(jnp.where(lane == tok, cn, 0.0), axis=1, keepdims=True)
    lfe = tile_rows(lfp_ref[0, 0])
    after = jnp.where(r2 > c2, 1.0, 0.0).astype(F32)
    suffix = jnp.dot(lfe, after, precision=HI, preferred_element_type=F32)
    s = lax.dot_general(qb_ref[...], kp_ref[0, 0].astype(BF16), _NT, preferred_element_type=F32)
    online(s + (cn_own + car_ref[...] + suffix), vp_ref[0, 0].astype(BF16))
    car_ref[...] = car_ref[...] + jnp.sum(lfe, axis=1, keepdims=True)

    @pl.when(pg == pl.num_programs(1) - 1)
    def _():
        pad = jnp.zeros((LANES - tt, w), F32)
        kn = jnp.concatenate([kn_ref[...], pad], axis=0).astype(BF16)
        vn = jnp.concatenate([vn_ref[...], pad], axis=0).astype(BF16)
        sn = lax.dot_general(qb_ref[...], kn, _NT, preferred_element_type=F32)
        sn = jnp.where(lane <= tok, sn + (cn_own - cn), MASK_VALUE)
        online(sn, vn)
        o = jnp.where(hmask, acc_ref[...] / l_ref[...], 0.0)
        o_ref[...] = jnp.sum(o.reshape(tt, ATT_HEADS, w), axis=1)


def _fox_sample(z, lfn_t, ck, cv, clf_t, page_table, layer, nb, tt):
    npg = page_table.shape[1]
    w = ATT_HEADS * ATT_HD
    rows = tt * ATT_HEADS

    def page(b, p, pt):
        return (layer, pt[b * npg + (npg - 1 - p)], 0, 0)

    grid_spec = pltpu.PrefetchScalarGridSpec(
        num_scalar_prefetch=1,
        grid=(nb, npg),
        in_specs=[
            pl.BlockSpec((tt, w), lambda b, p, pt: (b, ZB_AQ)),
            pl.BlockSpec((tt, w), lambda b, p, pt: (b, ZB_AK)),
            pl.BlockSpec((tt, w), lambda b, p, pt: (b, ZB_AV)),
            pl.BlockSpec((1, ATT_HEADS, LANES), lambda b, p, pt: (b, 0, 0)),
            pl.BlockSpec((1, 1, PAGE_SIZE, w), page),
            pl.BlockSpec((1, 1, PAGE_SIZE, w), page),
            pl.BlockSpec((1, 1, ATT_HEADS, PAGE_SIZE), page),
        ],
        out_specs=pl.BlockSpec((tt, w), lambda b, p, pt: (b, 0)),
        scratch_shapes=[pltpu.VMEM((rows, w), BF16), pltpu.VMEM((rows, 1), F32), pltpu.VMEM((rows, 1), F32),
                        pltpu.VMEM((rows, w), F32), pltpu.VMEM((rows, 1), F32), pltpu.VMEM((rows, LANES), F32)],
    )
    return pl.pallas_call(
        functools.partial(_fox_sample_kernel, tt=tt),
        grid_spec=grid_spec,
        out_shape=SDS((nb * tt, w), F32),
        compiler_params=_params("arbitrary", "arbitrary"),
        name="fox_sample",
    )(page_table.reshape(-1), z, z, z, lfn_t, ck, cv, clf_t)


def _merge_kernel(x_ref, yc_ref, yh_ref, ya_ref, gt_ref, wb_ref, wo_ref, g2_ref, h_ref, xn_ref):
    merged = jnp.zeros(x_ref.shape, F32)
    for b, y_ref in enumerate((yc_ref, yh_ref, ya_ref)):
        proj = jnp.dot(y_ref[...].astype(BF16), wb_ref[b], preferred_element_type=F32)
        merged = merged + _sigmoid(gt_ref[:, b * D_MODEL:(b + 1) * D_MODEL]) * proj
    h = x_ref[...] + jnp.dot(merged.astype(BF16), wo_ref[...], preferred_element_type=F32)
    h_ref[...] = h
    xn_ref[...] = _rms(h, g2_ref[...]).astype(BF16)


def _merge(x, yc, yh, ya, z, wb, wo, g2, tm):
    n, d = x.shape
    br = pl.BlockSpec((tm, BRANCH_W), lambda i: (i, 0))
    return pl.pallas_call(
        _merge_kernel,
        grid=(n // tm,),
        in_specs=[
            pl.BlockSpec((tm, d), lambda i: (i, 0)), br, br, br,
            pl.BlockSpec((tm, 3 * d), lambda i: (i, 0)),
            pl.BlockSpec((3, BRANCH_W, d), lambda i: (0, 0, 0)),
            pl.BlockSpec((d, d), lambda i: (0, 0)),
            pl.BlockSpec((1, d), lambda i: (0, 0)),
        ],
        out_specs=[pl.BlockSpec((tm, d), lambda i: (i, 0)), pl.BlockSpec((tm, d), lambda i: (i, 0))],
        out_shape=[SDS((n, d), F32), SDS((n, d), BF16)],
        compiler_params=_params("arbitrary"),
        name="merge",
    )(x, yc, yh, ya, z, wb, wo, g2)


def _take_max(work, row_iota, n_rows):
    m = jnp.max(work, axis=0, keepdims=True)
    first = jnp.min(jnp.where(work == m, row_iota, float(n_rows)), axis=0, keepdims=True)
    return m, jnp.where(row_iota == first, -jnp.inf, work)


_N_CAND = -(-len(_PAIRS) // 8) * 8


def _peer_route_kernel(xn_ref, wqt_ref, keys_ref, s1_ref, e1_ref, s2_ref, e2_ref, tau_ref, cand_ref, *, t):
    xn = xn_ref[...]
    key_iota = lax.broadcasted_iota(jnp.int32, (N_KEYS, t), 0).astype(F32)
    n_cand = _N_CAND
    cand_iota = lax.broadcasted_iota(jnp.int32, (n_cand, t), 0).astype(F32)
    cand_ref[len(_PAIRS):, :] = jnp.full((n_cand - len(_PAIRS), t), -jnp.inf, F32)

    def head(h, carry):
        dk = 2 * N_KEYS
        qt = lax.dot_general(wqt_ref[pl.ds(pl.multiple_of(h * dk, dk), dk), :], xn, _NT,
                             preferred_element_type=F32)
        scores, tops = [], []
        for p in range(2):
            s = jnp.dot(keys_ref[h, p], qt[p * N_KEYS:(p + 1) * N_KEYS].astype(BF16), preferred_element_type=F32)
            scores.append(s)
            vals, work = [], s
            for _ in range(PEER_TOPK):
                m, work = _take_max(work, key_iota, N_KEYS)
                vals.append(m)
            tops.append(vals)
        for i, (a, b) in enumerate(_PAIRS):
            cand_ref[i:i + 1, :] = tops[0][a] + tops[1][b]
        work = cand_ref[...]
        best = []
        for _ in range(PEER_TOPK):
            m, work = _take_max(work, cand_iota, n_cand)
            best.append(m)
        norm = jnp.zeros((1, t), F32)
        for m in best:
            norm = norm + jnp.exp(m - best[0])
        s1_ref[h] = scores[0]
        s2_ref[h] = scores[1]
        e1_ref[h] = jnp.exp(scores[0] - tops[0][0]) / norm
        e2_ref[h] = jnp.exp(scores[1] - tops[1][0])
        tau_ref[h] = best[-1]
        return carry

    lax.fori_loop(0, PEER_HEADS, head, 0)


def _peer_route(xn, wqt, keys, t):
    n, d = xn.shape
    big = pl.BlockSpec((PEER_HEADS, N_KEYS, t), lambda i: (0, 0, i))
    big_shape = SDS((PEER_HEADS, N_KEYS, n), F32)
    return pl.pallas_call(
        functools.partial(_peer_route_kernel, t=t),
        grid=(n // t,),
        in_specs=[
            pl.BlockSpec((t, d), lambda i: (i, 0)),
            pl.BlockSpec(wqt.shape, lambda i: (0, 0)),
            pl.BlockSpec(keys.shape, lambda i: (0, 0, 0, 0)),
        ],
        out_specs=[big, big, big, big, pl.BlockSpec((PEER_HEADS, 1, t), lambda i: (0, 0, i))],
        out_shape=[big_shape, big_shape, big_shape, big_shape, SDS((PEER_HEADS, 1, n), F32)],
        scratch_shapes=[pltpu.VMEM((_N_CAND, t), F32)],
        compiler_params=_params("arbitrary"),
        name="peer_route",
    )(xn, wqt, keys)


def _peer_dense_kernel(xn_ref, h_ref, u_ref, vt_ref, s1_ref, e1_ref, s2_ref, e2_ref, tau_ref, o_ref, acc_ref,
                       *, t, ec):
    c = pl.program_id(1)

    @pl.when(c == 0)
    def _():
        acc_ref[...] = jnp.zeros_like(acc_ref)

    hid = lax.dot_general(u_ref[...], xn_ref[...], _NT, preferred_element_type=F32)
    n_i1 = ec // N_KEYS
    parts = []
    for ii in range(n_i1):
        i1 = c * n_i1 + ii
        wsum = jnp.zeros((N_KEYS, t), F32)
        for h in range(PEER_HEADS):
            s1 = s1_ref[h, pl.ds(i1, 1), :]
            e1 = e1_ref[h, pl.ds(i1, 1), :]
            keep = (s1 + s2_ref[h]) >= tau_ref[h]
            wsum = wsum + jnp.where(keep, e1 * e2_ref[h], 0.0)
        parts.append(wsum)
    wd = parts[0] if n_i1 == 1 else jnp.concatenate(parts, axis=0)
    gelu = 0.5 * hid * (1.0 + lax.erf(hid * (2.0 ** -0.5)))
    act = (gelu * wd).astype(BF16)
    acc_ref[...] += jnp.dot(vt_ref[...], act, preferred_element_type=F32)

    @pl.when(c == pl.num_programs(1) - 1)
    def _():
        o_ref[...] = h_ref[...] + acc_ref[...].T


def _peer_dense(xn, h, u, vt, route, t, ec):
    n, d = xn.shape
    ne = u.shape[0]
    s1, e1, s2, e2, tau = route
    big = pl.BlockSpec((PEER_HEADS, N_KEYS, t), lambda i, c: (0, 0, i))
    return pl.pallas_call(
        functools.partial(_peer_dense_kernel, t=t, ec=ec),
        grid=(n // t, ne // ec),
        in_specs=[
            pl.BlockSpec((t, d), lambda i, c: (i, 0)),
            pl.BlockSpec((t, d), lambda i, c: (i, 0)),
            pl.BlockSpec((ec, d), lambda i, c: (c, 0)),
            pl.BlockSpec((d, ec), lambda i, c: (0, c)),
            big, big, big, big,
            pl.BlockSpec((PEER_HEADS, 1, t), lambda i, c: (0, 0, i)),
        ],
        out_specs=pl.BlockSpec((t, d), lambda i, c: (i, 0)),
        out_shape=SDS((n, d), F32),
        scratch_shapes=[pltpu.VMEM((d, t), F32)],
        compiler_params=_params("arbitrary", "arbitrary"),
        name="peer_dense",
    )(xn, h, u, vt, s1, e1, s2, e2, tau)


def _final_norm_kernel(x_ref, g_ref, o_ref):
    o_ref[...] = _rms(x_ref[...], g_ref[...])


def _final_norm(x, g, tm):
    n, d = x.shape
    return pl.pallas_call(
        _final_norm_kernel,
        grid=(n // tm,),
        in_specs=[pl.BlockSpec((tm, d), lambda i: (i, 0)), pl.BlockSpec((1, d), lambda i: (0, 0))],
        out_specs=pl.BlockSpec((tm, d), lambda i: (i, 0)),
        out_shape=SDS((n, d), F32),
        compiler_params=_params("arbitrary"),
        name="final_norm",
    )(x, g)


def _tile(n, pref):
    return pref if n % pref == 0 else n


def _layer_weights(l, lbs, norm1_g, w_in, conv_dw, conv_db, conv_ln_g, conv_ln_b, hg_norm_g, att_fb,
                   w_branch, w_out, norm2_g, peer_wq, peer_keys, peer_u, peer_v):
    n_main = 2 * BRANCH_W + 4 * HG_HEADS * HG_DK + 3 * ATT_HEADS * ATT_HD
    w = w_in[l]
    row = lambda a: a.reshape(1, -1).astype(F32)
    return dict(
        g1=row(norm1_g[l]),
        w_main=jnp.concatenate([w[:, n_main + ATT_HEADS:], w[:, :n_main]], axis=1).astype(BF16),
        w_f=jnp.pad(w[:, n_main:n_main + ATT_HEADS], ((0, 0), (0, LANES - ATT_HEADS))).astype(BF16),
        fb=jnp.pad(att_fb[l].astype(F32), (0, LANES - ATT_HEADS)).reshape(1, LANES),
        conv_w=conv_dw[l].astype(F32), conv_b=row(conv_db[l]), ln_g=row(conv_ln_g[l]), ln_b=row(conv_ln_b[l]),
        lb=row(lbs[l]), hg_g=row(hg_norm_g[l]),
        wb=w_branch[l].astype(BF16), wo=w_out[l].astype(BF16), g2=row(norm2_g[l]),
        wqt=peer_wq[l].T.astype(BF16), keys=peer_keys[l].astype(BF16),
        u=peer_u[l].astype(BF16), vt=peer_v[l].T.astype(BF16),
    )


def _token_mixers(x, wt, nb, seq, conv_buf, hg_state, attend):
    n = nb * seq
    z, lf = _in_proj(x, wt["g1"], wt["w_main"], wt["w_f"], wt["fb"], _tile(n, 1024))
    y_conv, conv_new = _conv_branch(z, conv_buf, wt["conv_w"], wt["conv_b"], wt["ln_g"], wt["ln_b"],
                                    nb, seq, _tile(seq, 512))
    y_hgrn, hg_new = _hgrn_branch(z, hg_state, wt["lb"], wt["hg_g"], nb, seq, _tile(seq, 128), _tile(seq, 16))
    y_att = attend(z, lf)
    h, xn = _merge(x, y_conv, y_hgrn, y_att, z, wt["wb"], wt["wo"], wt["g2"], _tile(n, 512))
    tp = _tile(n, 256)
    route = _peer_route(xn, wt["wqt"], wt["keys"], tp)
    out = _peer_dense(xn, h, wt["u"], wt["vt"], route, tp, 512)
    w = ATT_HEADS * ATT_HD
    k_rows = z[:, ZB_AK * BRANCH_W:ZB_AK * BRANCH_W + w].reshape(nb, seq, ATT_HEADS, ATT_HD)
    v_rows = z[:, ZB_AV * BRANCH_W:ZB_AV * BRANCH_W + w].reshape(nb, seq, ATT_HEADS, ATT_HD)
    logf = lf[:, :ATT_HEADS].reshape(nb, seq, ATT_HEADS)
    return out, conv_new, hg_new, k_rows, v_rows, logf


def kernel(x_prompt, x_sample, cache_k, cache_v, cache_logf, state_conv, state_hgrn, page_table, norm1_g, w_in, conv_dw, conv_db, conv_ln_g, conv_ln_b, hg_lb_logits, hg_norm_g, att_fb, w_branch, w_out, norm2_g, peer_wq, peer_keys, peer_u, peer_v, final_g):
    depth = w_in.shape[0]
    bp, sp, d = x_prompt.shape
    bs, ss, _ = x_sample.shape
    w = ATT_HEADS * ATT_HD
    probs = jax.nn.softmax(hg_lb_logits.astype(F32), axis=0)
    lbs = jnp.cumsum(probs, axis=0) - probs[0:1]
    ck = cache_k.reshape(cache_k.shape[:3] + (w,))
    cv = cache_v.reshape(cache_v.shape[:3] + (w,))
    clf_t = jnp.swapaxes(cache_logf.astype(F32), 2, 3)

    hp = x_prompt.reshape(bp * sp, d)
    hs = x_sample.reshape(bs * ss, d)
    outs_p, outs_s = [], []
    for l in range(depth):
        wt = _layer_weights(l, lbs, norm1_g, w_in, conv_dw, conv_db, conv_ln_g, conv_ln_b, hg_norm_g, att_fb,
                            w_branch, w_out, norm2_g, peer_wq, peer_keys, peer_u, peer_v)

        def attend_prompt(z, lf):
            ct = _logf_cumsum(lf, bp, sp, _tile(sp, 512))
            return _fox_prompt(z, ct, bp, sp, _tile(sp, 512))

        def attend_sample(z, lf, l=l):
            lfn_t = jnp.swapaxes(lf[:, :ATT_HEADS].reshape(bs, ss, ATT_HEADS), 1, 2)
            lfn_t = jnp.pad(lfn_t, ((0, 0), (0, 0), (0, LANES - ss)))
            return _fox_sample(z, lfn_t, ck, cv, clf_t, page_table, l, bs, ss)

        hp, *rp = _token_mixers(hp, wt, bp, sp, jnp.zeros((bp, CONV_K - 1, BRANCH_W), F32),
                                jnp.zeros((bp, HG_HEADS, HG_DK, HG_DK), F32), attend_prompt)
        hs, *rs = _token_mixers(hs, wt, bs, ss, state_conv[l], state_hgrn[l], attend_sample)
        outs_p.append(rp)
        outs_s.append(rs)

    fg = final_g.reshape(1, d).astype(F32)
    y_prompt = _final_norm(hp, fg, _tile(bp * sp, 1024)).reshape(bp, sp, d)
    y_sample = _final_norm(hs, fg, _tile(bs * ss, 1024)).reshape(bs, ss, d)
    stack = lambda outs, i: jnp.stack([r[i] for r in outs])
    return (y_prompt, y_sample,
            stack(outs_p, 2), stack(outs_p, 3), stack(outs_p, 4), stack(outs_p, 0), stack(outs_p, 1),
            stack(outs_s, 2), stack(outs_s, 3), stack(outs_s, 4), stack(outs_s, 0), stack(outs_s, 1))
```

```python
import functools

import jax
import jax.numpy as jnp
from jax import lax
from jax.experimental import pallas as pl
from jax.experimental.pallas import tpu as pltpu

F32 = jnp.float32
BF16 = jnp.bfloat16
HI = lax.Precision.HIGHEST
SDS = jax.ShapeDtypeStruct

D_MODEL = 1024
BRANCH_W = 512
CONV_K = 31
HG_HEADS = 4
HG_DK = 128
ATT_HEADS = 8
ATT_HD = 64
PEER_HEADS = 8
N_KEYS = 128
PEER_TOPK = 16
PAGE_SIZE = 128
EPS = 1e-6
GATE_FLOOR = 1e-20
MASK_VALUE = -1e30
LANES = 128
_HD_SHIFT = ATT_HD.bit_length() - 1
_HEAD_SHIFT = ATT_HEADS.bit_length() - 1

Z_COLS = 15 * BRANCH_W
ZB_CONV = 3
ZB_HQ, ZB_HF, ZB_HI, ZB_HG, ZB_AQ, ZB_AK, ZB_AV = 8, 9, 10, 11, 12, 13, 14

_PAIRS = tuple((a, b) for a in range(PEER_TOPK) for b in range(PEER_TOPK) if (a + 1) * (b + 1) <= PEER_TOPK)

_NT = (((1,), (1,)), ((), ()))
_TN = (((0,), (0,)), ((), ()))


def _params(*sem, vmem_mb=48):
    return pltpu.CompilerParams(dimension_semantics=sem, vmem_limit_bytes=vmem_mb * 1024 * 1024)


def _sigmoid(x):
    return 1.0 / (1.0 + jnp.exp(-x))


def _log_sigmoid(x):
    return jnp.minimum(x, 0.0) - jnp.log1p(jnp.exp(-jnp.abs(x)))


def _rms(x, g):
    return x * lax.rsqrt(jnp.mean(x * x, axis=-1, keepdims=True) + EPS) * g


def _in_proj_kernel(x_ref, g_ref, w_ref, wf_ref, fb_ref, z_ref, lf_ref, xn_ref):
    @pl.when(pl.program_id(1) == 0)
    def _():
        xn = _rms(x_ref[...], g_ref[...]).astype(BF16)
        xn_ref[...] = xn
        af = jnp.dot(xn, wf_ref[...], preferred_element_type=F32) + fb_ref[...]
        lf_ref[...] = _log_sigmoid(af)

    z_ref[...] = jnp.dot(xn_ref[...], w_ref[...], preferred_element_type=F32)


def _in_proj(x, g, w_main, w_f, fb, tm):
    n, d = x.shape
    tn = BRANCH_W
    return pl.pallas_call(
        _in_proj_kernel,
        grid=(n // tm, Z_COLS // tn),
        in_specs=[
            pl.BlockSpec((tm, d), lambda i, j: (i, 0)),
            pl.BlockSpec((1, d), lambda i, j: (0, 0)),
            pl.BlockSpec((d, tn), lambda i, j: (0, j)),
            pl.BlockSpec((d, LANES), lambda i, j: (0, 0)),
            pl.BlockSpec((1, LANES), lambda i, j: (0, 0)),
        ],
        out_specs=[
            pl.BlockSpec((tm, tn), lambda i, j: (i, j)),
            pl.BlockSpec((tm, LANES), lambda i, j: (i, 0)),
        ],
        out_shape=[SDS((n, Z_COLS), F32), SDS((n, LANES), F32)],
        scratch_shapes=[pltpu.VMEM((tm, d), BF16)],
        compiler_params=_params("arbitrary", "arbitrary"),
        name="in_proj",
    )(x, g, w_main, w_f, fb)


_CONV_PAD = 32


def _conv_kernel(a_ref, buf_ref, w_ref, b_ref, g_ref, bt_ref, y_ref, nc_ref, ext_ref, *, tl):
    hist = CONV_K - 1
    lo = _CONV_PAD - hist

    @pl.when(pl.program_id(1) == 0)
    def _():
        ext_ref[lo:_CONV_PAD, :] = buf_ref[0]

    a = a_ref[...]
    ext_ref[_CONV_PAD:_CONV_PAD + tl, :] = a[:, :BRANCH_W] * _sigmoid(a[:, BRANCH_W:])
    acc = jnp.zeros((tl, BRANCH_W), F32)
    for j in range(CONV_K):
        acc = acc + w_ref[j:j + 1, :] * ext_ref[lo + j:lo + j + tl, :]
    y = acc + b_ref[...]
    mu = jnp.mean(y, axis=-1, keepdims=True)
    yc = y - mu
    var = jnp.mean(yc * yc, axis=-1, keepdims=True)
    y = yc * lax.rsqrt(var + EPS) * g_ref[...] + bt_ref[...]
    y_ref[...] = y * _sigmoid(y)
    tail = ext_ref[tl + lo:tl + _CONV_PAD, :]
    nc_ref[0] = tail
    ext_ref[lo:_CONV_PAD, :] = tail


def _conv_branch(z, buf, w, b, g, bt, nb, seq, tl):
    nt = seq // tl
    hist = CONV_K - 1
    vec = pl.BlockSpec((1, BRANCH_W), lambda i, t: (0, 0))
    return pl.pallas_call(
        functools.partial(_conv_kernel, tl=tl),
        grid=(nb, nt),
        in_specs=[
            pl.BlockSpec((tl, 2 * BRANCH_W), lambda i, t: (i * nt + t, ZB_CONV)),
            pl.BlockSpec((1, hist, BRANCH_W), lambda i, t: (i, 0, 0)),
            pl.BlockSpec((CONV_K, BRANCH_W), lambda i, t: (0, 0)),
            vec, vec, vec,
        ],
        out_specs=[
            pl.BlockSpec((tl, BRANCH_W), lambda i, t: (i * nt + t, 0)),
            pl.BlockSpec((1, hist, BRANCH_W), lambda i, t: (i, 0, 0)),
        ],
        out_shape=[SDS((nb * seq, BRANCH_W), F32), SDS((nb, hist, BRANCH_W), F32)],
        scratch_shapes=[pltpu.VMEM((_CONV_PAD + tl, BRANCH_W), F32)],
        compiler_params=_params("arbitrary", "arbitrary"),
        name="conv_branch",
    )(z, buf, w, b, g, bt)


def _hgrn_kernel(q_ref, f_ref, i_ref, g_ref, s0_ref, lb_ref, ng_ref, y_ref, sn_ref,
                 st_ref, qs_ref, ks_ref, bs_ref, *, tl, c):
    t = pl.program_id(1)

    @pl.when(t == 0)
    def _():
        for h in range(HG_HEADS):
            st_ref[h] = s0_ref[0, h].T

    lb = jnp.clip(lb_ref[...], 0.0, 1.0)
    f = lb + (1.0 - lb) * _sigmoid(f_ref[...])
    lf = jnp.log(jnp.maximum(f, GATE_FLOOR))
    q = q_ref[...]
    qs_ref[...] = q * _sigmoid(q)
    ks_ref[...] = 1.0 - f
    r = lax.broadcasted_iota(jnp.int32, (tl, tl), 0)
    cc = lax.broadcasted_iota(jnp.int32, (tl, tl), 1)
    tri = jnp.where(cc <= r, jnp.where(cc >= (r & -c), 1.0, 0.0), 0.0).astype(F32)
    bs_ref[...] = jnp.dot(tri, lf, precision=HI, preferred_element_type=F32)
    rowid = lax.broadcasted_iota(jnp.int32, (c, 1), 0)

    def chunk(ci, carry):
        off = pl.multiple_of(ci * c, c)
        for h in range(HG_HEADS):
            sl = slice(h * HG_DK, (h + 1) * HG_DK)
            qc = qs_ref[pl.ds(off, c), sl]
            kc = ks_ref[pl.ds(off, c), sl]
            bc = bs_ref[pl.ds(off, c), sl]
            vc = i_ref[pl.ds(off, c), sl]
            o = jnp.zeros((c, HG_DK), F32)
            for s in range(c):
                k_row = kc[s:s + 1, :]
                b_row = bc[s:s + 1, :]
                v_row = vc[s:s + 1, :]
                p = qc * k_row * jnp.exp(jnp.minimum(bc - b_row, 0.0))
                a = jnp.sum(p, axis=1, keepdims=True)
                o = o + jnp.where(rowid >= s, a, 0.0) * v_row
            st = st_ref[h]
            qt = (qc * jnp.exp(bc)).astype(BF16)
            o = o + lax.dot_general(qt, st.astype(BF16), _NT, preferred_element_type=F32)
            b_last = bc[c - 1:c, :]
            kt = (kc * jnp.exp(b_last - bc)).astype(BF16)
            upd = lax.dot_general(vc.astype(BF16), kt, _TN, preferred_element_type=F32)
            st_ref[h] = st * jnp.exp(b_last) + upd
            o = _rms(o, ng_ref[...])
            y_ref[pl.ds(off, c), sl] = o * _sigmoid(g_ref[pl.ds(off, c), sl])
        return carry

    lax.fori_loop(0, tl // c, chunk, 0)

    @pl.when(t == pl.num_programs(1) - 1)
    def _():
        for h in range(HG_HEADS):
            sn_ref[0, h] = st_ref[h].T


def _hgrn_branch(z, s0, lb, ng, nb, seq, tl, c):
    nt = seq // tl
    w = HG_HEADS * HG_DK

    def col(j):
        return pl.BlockSpec((tl, w), lambda i, t: (i * nt + t, j))

    st_spec = pl.BlockSpec((1, HG_HEADS, HG_DK, HG_DK), lambda i, t: (i, 0, 0, 0))
    return pl.pallas_call(
        functools.partial(_hgrn_kernel, tl=tl, c=c),
        grid=(nb, nt),
        in_specs=[col(ZB_HQ), col(ZB_HF), col(ZB_HI), col(ZB_HG), st_spec,
                  pl.BlockSpec((1, w), lambda i, t: (0, 0)),
                  pl.BlockSpec((1, HG_DK), lambda i, t: (0, 0))],
        out_specs=[pl.BlockSpec((tl, w), lambda i, t: (i * nt + t, 0)), st_spec],
        out_shape=[SDS((nb * seq, w), F32), SDS((nb, HG_HEADS, HG_DK, HG_DK), F32)],
        scratch_shapes=[pltpu.VMEM((HG_HEADS, HG_DK, HG_DK), F32),
                        pltpu.VMEM((tl, w), F32), pltpu.VMEM((tl, w), F32), pltpu.VMEM((tl, w), F32)],
        compiler_params=_params("arbitrary", "arbitrary"),
        name="hgrn_branch",
    )(z, z, z, z, s0, lb, ng)


def _cumsum_kernel(lf_ref, ct_ref, carry_ref, *, tl):
    @pl.when(pl.program_id(1) == 0)
    def _():
        carry_ref[...] = jnp.zeros_like(carry_ref)

    r = lax.broadcasted_iota(jnp.int32, (tl, tl), 0)
    cc = lax.broadcasted_iota(jnp.int32, (tl, tl), 1)
    tri = jnp.where(cc <= r, 1.0, 0.0).astype(F32)
    cs = jnp.dot(tri, lf_ref[...], precision=HI, preferred_element_type=F32) + carry_ref[...]
    carry_ref[...] = cs[tl - 1:tl, :]
    ct_ref[0] = cs.T[:ATT_HEADS, :]


def _logf_cumsum(lf, nb, seq, tl):
    nt = seq // tl
    return pl.pallas_call(
        functools.partial(_cumsum_kernel, tl=tl),
        grid=(nb, nt),
        in_specs=[pl.BlockSpec((tl, LANES), lambda i, t: (i * nt + t, 0))],
        out_specs=pl.BlockSpec((1, ATT_HEADS, tl), lambda i, t: (i, 0, t)),
        out_shape=SDS((nb, ATT_HEADS, seq), F32),
        scratch_shapes=[pltpu.VMEM((1, LANES), F32)],
        compiler_params=_params("arbitrary", "arbitrary"),
        name="logf_cumsum",
    )(lf)


_LOG2E = 1.4426950408889634
_FOX_SUB = 128


def _fox_kernel(q_ref, k_ref, v_ref, ck_ref, cq_ref, o_ref, qm_ref, m_ref, acc_ref, *, t, sub):
    pair = pl.program_id(1)
    qi = pl.program_id(2)
    ki = pl.program_id(3)
    lane = lax.broadcasted_iota(jnp.int32, (1, LANES), 1)

    @pl.when(ki == 0)
    def _():
        q = q_ref[...] * (ATT_HD ** -0.5 * _LOG2E)
        qm_ref[0] = jnp.where(lane < ATT_HD, q, 0.0).astype(BF16)
        qm_ref[1] = jnp.where(lane >= ATT_HD, q, 0.0).astype(BF16)
        m_ref[...] = jnp.full_like(m_ref, MASK_VALUE)
        acc_ref[...] = jnp.zeros_like(acc_ref)

    def step(diagonal):
        k = k_ref[...].astype(BF16)
        v = jnp.concatenate([v_ref[...].astype(BF16), jnp.ones((t, LANES), BF16)], axis=1)
        head_iota = lax.broadcasted_iota(jnp.int32, (ATT_HEADS, 1), 0)
        for j in range(2):
            own = head_iota == 2 * pair + j
            ck = jnp.sum(jnp.where(own, ck_ref[0], 0.0), axis=0, keepdims=True)
            c0 = jnp.sum(jnp.where(own, cq_ref[0, :, 0:1], 0.0), axis=0, keepdims=True)
            bias = (c0 - ck) * _LOG2E
            for r0 in range(0, t, sub):
                nk = r0 + sub if diagonal else t
                rows = slice(r0, r0 + sub)
                s = lax.dot_general(qm_ref[j, rows, :], k[:nk], _NT, preferred_element_type=F32) + bias[:, :nk]
                if diagonal:
                    rid = lax.broadcasted_iota(jnp.int32, (sub, nk), 0) + r0
                    cid = lax.broadcasted_iota(jnp.int32, (sub, nk), 1)
                    s = jnp.where(cid <= rid, s, MASK_VALUE)
                m_old = m_ref[j, rows, :]
                m_new = jnp.maximum(m_old, jnp.max(s, axis=1, keepdims=True))
                alpha = jnp.exp2(m_old - m_new)
                p = jnp.exp2(s - m_new[:, 0:1]).astype(BF16)
                pv = jnp.dot(p, v[:nk], preferred_element_type=F32)
                acc_ref[j, rows, :] = jnp.concatenate([alpha, alpha], axis=1) * acc_ref[j, rows, :] + pv
                m_ref[j, rows, :] = m_new

    @pl.when(ki < qi)
    def _():
        step(False)

    @pl.when(ki == qi)
    def _():
        step(True)
        o0 = acc_ref[0, :, :LANES] / acc_ref[0, :, LANES:]
        o1 = acc_ref[1, :, :LANES] / acc_ref[1, :, LANES:]
        o_ref[...] = jnp.where(lane < ATT_HD, o0, o1)


def _fox_prompt(z, ct, nb, seq, t):
    nq = seq // t
    npair = ATT_HEADS // 2
    zq, zk, zv = (zb * (BRANCH_W // LANES) for zb in (ZB_AQ, ZB_AK, ZB_AV))
    return pl.pallas_call(
        functools.partial(_fox_kernel, t=t, sub=min(t, _FOX_SUB)),
        grid=(nb, npair, nq, nq),
        in_specs=[
            pl.BlockSpec((t, LANES), lambda b, p, qi, ki: (b * nq + qi, zq + p)),
            pl.BlockSpec((t, LANES), lambda b, p, qi, ki: (b * nq + jnp.minimum(ki, qi), zk + p)),
            pl.BlockSpec((t, LANES), lambda b, p, qi, ki: (b * nq + jnp.minimum(ki, qi), zv + p)),
            pl.BlockSpec((1, ATT_HEADS, t), lambda b, p, qi, ki: (b, 0, jnp.minimum(ki, qi))),
            pl.BlockSpec((1, ATT_HEADS, t), lambda b, p, qi, ki: (b, 0, qi)),
        ],
        out_specs=pl.BlockSpec((t, LANES), lambda b, p, qi, ki: (b * nq + qi, p)),
        out_shape=SDS((nb * seq, BRANCH_W), F32),
        scratch_shapes=[pltpu.VMEM((2, t, LANES), BF16), pltpu.VMEM((2, t, LANES), F32),
                        pltpu.VMEM((2, t, 2 * LANES), F32)],
        compiler_params=_params("arbitrary", "arbitrary", "arbitrary", "arbitrary"),
        name="fox_prompt",
    )(z, z, z, ct, ct)


def _fox_sample_kernel(pt_ref, q_ref, kn_ref, vn_ref, lfn_ref, kp_ref, vp_ref, lfp_ref, o_ref,
                       qb_ref, m_ref, l_ref, acc_ref, car_ref, cn_ref, *, tt):
    del pt_ref
    pg = pl.program_id(1)
    rows = tt * ATT_HEADS
    w = ATT_HEADS * ATT_HD
    rid = lax.broadcasted_iota(jnp.int32, (rows, 1), 0)
    hmask = (lax.broadcasted_iota(jnp.int32, (rows, w), 1) >> _HD_SHIFT) == (rid & (ATT_HEADS - 1))
    tok = rid >> _HEAD_SHIFT
    lane = lax.broadcasted_iota(jnp.int32, (rows, LANES), 1)
    r2 = lax.broadcasted_iota(jnp.int32, (LANES, LANES), 0)
    c2 = lax.broadcasted_iota(jnp.int32, (LANES, LANES), 1)

    def tile_rows(x):
        return jnp.broadcast_to(x[None], (tt,) + x.shape).reshape(rows, x.shape[-1])

    def online(s, v):
        m_old = m_ref[...]
        m_new = jnp.maximum(m_old, jnp.max(s, axis=1, keepdims=True))
        alpha = jnp.exp(m_old - m_new)
        p = jnp.exp(s - m_new)
        l_ref[...] = alpha * l_ref[...] + jnp.sum(p, axis=1, keepdims=True)
        acc_ref[...] = alpha * acc_ref[...] + jnp.dot(p.astype(BF16), v, preferred_element_type=F32)
        m_ref[...] = m_new

    @pl.when(pg == 0)
    def _():
        q = q_ref[...] * (ATT_HD ** -0.5)
        qrep = jnp.broadcast_to(q[:, None, :], (tt, ATT_HEADS, w)).reshape(rows, w)
        qb_ref[...] = jnp.where(hmask, qrep, 0.0).astype(BF16)
        m_ref[...] = jnp.full_like(m_ref, MASK_VALUE)
        l_ref[...] = jnp.zeros_like(l_ref)
        acc_ref[...] = jnp.zeros_like(acc_ref)
        car_ref[...] = jnp.zeros_like(car_ref)
        incl = jnp.where(r2 <= c2, 1.0, 0.0).astype(F32)
        cn_ref[...] = jnp.dot(tile_rows(lfn_ref[0]), incl, precision=HI, preferred_element_type=F32)

    cn = cn_ref[...]
    cn_own = jnp.sum(jnp.where(lane == tok, cn, 0.0), axis=1, keepdims=True)
    lfe = tile_rows(lfp_ref[0, 0])
    after = jnp.where(r2 > c2, 1.0, 0.0).astype(F32)
    suffix = jnp.dot(lfe, after, precision=HI, preferred_element_type=F32)
    s = lax.dot_general(qb_ref[...], kp_ref[0, 0].astype(BF16), _NT, preferred_element_type=F32)
    online(s + (cn_own + car_ref[...] + suffix), vp_ref[0, 0].astype(BF16))
    car_ref[...] = car_ref[...] + jnp.sum(lfe, axis=1, keepdims=True)

    @pl.when(pg == pl.num_programs(1) - 1)
    def _():
        pad = jnp.zeros((LANES - tt, w), F32)
        kn = jnp.concatenate([kn_ref[...], pad], axis=0).astype(BF16)
        vn = jnp.concatenate([vn_ref[...], pad], axis=0).astype(BF16)
        sn = lax.dot_general(qb_ref[...], kn, _NT, preferred_element_type=F32)
        sn = jnp.where(lane <= tok, sn + (cn_own - cn), MASK_VALUE)
        online(sn, vn)
        o = jnp.where(hmask, acc_ref[...] / l_ref[...], 0.0)
        o_ref[...] = jnp.sum(o.reshape(tt, ATT_HEADS, w), axis=1)


def _fox_sample(z, lfn_t, ck, cv, clf_t, page_table, layer, nb, tt):
    npg = page_table.shape[1]
    w = ATT_HEADS * ATT_HD
    rows = tt * ATT_HEADS

    def page(b, p, pt):
        return (layer, pt[b * npg + (npg - 1 - p)], 0, 0)

    grid_spec = pltpu.PrefetchScalarGridSpec(
        num_scalar_prefetch=1,
        grid=(nb, npg),
        in_specs=[
            pl.BlockSpec((tt, w), lambda b, p, pt: (b, ZB_AQ)),
            pl.BlockSpec((tt, w), lambda b, p, pt: (b, ZB_AK)),
            pl.BlockSpec((tt, w), lambda b, p, pt: (b, ZB_AV)),
            pl.BlockSpec((1, ATT_HEADS, LANES), lambda b, p, pt: (b, 0, 0)),
            pl.BlockSpec((1, 1, PAGE_SIZE, w), page),
            pl.BlockSpec((1, 1, PAGE_SIZE, w), page),
            pl.BlockSpec((1, 1, ATT_HEADS, PAGE_SIZE), page),
        ],
        out_specs=pl.BlockSpec((tt, w), lambda b, p, pt: (b, 0)),
        scratch_shapes=[pltpu.VMEM((rows, w), BF16), pltpu.VMEM((rows, 1), F32), pltpu.VMEM((rows, 1), F32),
                        pltpu.VMEM((rows, w), F32), pltpu.VMEM((rows, 1), F32), pltpu.VMEM((rows, LANES), F32)],
    )
    return pl.pallas_call(
        functools.partial(_fox_sample_kernel, tt=tt),
        grid_spec=grid_spec,
        out_shape=SDS((nb * tt, w), F32),
        compiler_params=_params("arbitrary", "arbitrary"),
        name="fox_sample",
    )(page_table.reshape(-1), z, z, z, lfn_t, ck, cv, clf_t)


def _merge_kernel(x_ref, yc_ref, yh_ref, ya_ref, gt_ref, wb_ref, wo_ref, g2_ref, h_ref, xnt_ref):
    merged = jnp.zeros(x_ref.shape, F32)
    for b, y_ref in enumerate((yc_ref, yh_ref, ya_ref)):
        proj = jnp.dot(y_ref[...].astype(BF16), wb_ref[b], preferred_element_type=F32)
        merged = merged + _sigmoid(gt_ref[:, b * D_MODEL:(b + 1) * D_MODEL]) * proj
    h = x_ref[...] + jnp.dot(merged.astype(BF16), wo_ref[...], preferred_element_type=F32)
    h_ref[...] = h
    xnt_ref[...] = _rms(h, g2_ref[...]).T.astype(BF16)


def _merge(x, yc, yh, ya, z, wb, wo, g2, tm):
    n, d = x.shape
    br = pl.BlockSpec((tm, BRANCH_W), lambda i: (i, 0))
    return pl.pallas_call(
        _merge_kernel,
        grid=(n // tm,),
        in_specs=[
            pl.BlockSpec((tm, d), lambda i: (i, 0)), br, br, br,
            pl.BlockSpec((tm, 3 * d), lambda i: (i, 0)),
            pl.BlockSpec((3, BRANCH_W, d), lambda i: (0, 0, 0)),
            pl.BlockSpec((d, d), lambda i: (0, 0)),
            pl.BlockSpec((1, d), lambda i: (0, 0)),
        ],
        out_specs=[pl.BlockSpec((tm, d), lambda i: (i, 0)), pl.BlockSpec((d, tm), lambda i: (0, i))],
        out_shape=[SDS((n, d), F32), SDS((d, n), BF16)],
        compiler_params=_params("arbitrary"),
        name="merge",
    )(x, yc, yh, ya, z, wb, wo, g2)


def _take_max(work, row_iota, n_rows):
    m = jnp.max(work, axis=0, keepdims=True)
    first = jnp.min(jnp.where(work == m, row_iota, float(n_rows)), axis=0, keepdims=True)
    hit = row_iota == first
    return m, jnp.where(hit, -jnp.inf, work), hit


_N_CAND = -(-len(_PAIRS) // 8) * 8


def _peer_route_kernel(xnt_ref, wqt_ref, keys_ref, cut1_ref, e1_ref, rank2_ref, e2_ref, cand_ref, *, t):
    xnt = xnt_ref[...]
    key_iota = lax.broadcasted_iota(jnp.int32, (N_KEYS, t), 0).astype(F32)
    n_cand = _N_CAND
    cand_iota = lax.broadcasted_iota(jnp.int32, (n_cand, t), 0).astype(F32)
    cand_ref[len(_PAIRS):, :] = jnp.full((n_cand - len(_PAIRS), t), -jnp.inf, F32)

    def head(h, carry):
        dk = 2 * N_KEYS
        qt = jnp.dot(wqt_ref[pl.ds(pl.multiple_of(h * dk, dk), dk), :], xnt, preferred_element_type=F32)
        scores, tops = [], []
        for p in range(2):
            s = jnp.dot(keys_ref[h, p], qt[p * N_KEYS:(p + 1) * N_KEYS].astype(BF16), preferred_element_type=F32)
            scores.append(s)
            vals, work = [], s
            rank = jnp.full((N_KEYS, t), float(PEER_TOPK), F32)
            for i in range(PEER_TOPK):
                m, work, hit = _take_max(work, key_iota, N_KEYS)
                vals.append(m)
                if p == 1:
                    rank = jnp.where(hit, float(i), rank)
            tops.append(vals)
        for i, (a, b) in enumerate(_PAIRS):
            cand_ref[i:i + 1, :] = tops[0][a] + tops[1][b]
        work = cand_ref[...]
        best = []
        for _ in range(PEER_TOPK):
            m, work, _ = _take_max(work, cand_iota, n_cand)
            best.append(m)
        norm = jnp.zeros((1, t), F32)
        for m in best:
            norm = norm + jnp.exp(m - best[0])
        cut = jnp.zeros((N_KEYS, t), F32)
        for b in range(PEER_TOPK):
            cut = cut + jnp.where(scores[0] + tops[1][b] >= best[-1], 1.0, 0.0)
        cut1_ref[h] = cut
        e1_ref[h] = jnp.exp(scores[0] - tops[0][0]) / norm
        rank2_ref[h] = rank.astype(BF16)
        e2_ref[h] = jnp.exp(scores[1] - tops[1][0]).astype(BF16)
        return carry

    lax.fori_loop(0, PEER_HEADS, head, 0)


def _peer_route(xnt, wqt, keys, t):
    d, n = xnt.shape
    big = pl.BlockSpec((PEER_HEADS, N_KEYS, t), lambda i: (0, 0, i))
    wide, narrow = SDS((PEER_HEADS, N_KEYS, n), F32), SDS((PEER_HEADS, N_KEYS, n), BF16)
    return pl.pallas_call(
        functools.partial(_peer_route_kernel, t=t),
        grid=(n // t,),
        in_specs=[
            pl.BlockSpec((d, t), lambda i: (0, i)),
            pl.BlockSpec(wqt.shape, lambda i: (0, 0)),
            pl.BlockSpec(keys.shape, lambda i: (0, 0, 0, 0)),
        ],
        out_specs=[big, big, big, big],
        out_shape=[wide, wide, narrow, narrow],
        scratch_shapes=[pltpu.VMEM((_N_CAND, t), F32)],
        compiler_params=_params("arbitrary"),
        name="peer_route",
    )(xnt, wqt, keys)


_BF16_ROWS = 16


def _peer_dense_kernel(xnt_ref, h_ref, u_ref, vt_ref, cut1_ref, e1_ref, rank2_ref, e2_ref, o_ref,
                       acc_ref, act_ref, cutx_ref, e1x_ref, rank2x_ref, e2x_ref, *, t, ec):
    c = pl.program_id(1)
    n_chunks = pl.num_programs(1) - 1

    @pl.when(c == 0)
    def _():
        acc_ref[...] = jnp.zeros_like(acc_ref)
        act_ref[1] = jnp.zeros((ec, t), BF16)
        rank2x_ref[...] = rank2_ref[...]
        e2x_ref[...] = e2_ref[...]

        def expand(i, carry):
            for h in range(PEER_HEADS):
                cutx_ref[h, i] = jnp.broadcast_to(cut1_ref[h, pl.ds(i, 1), :], (_BF16_ROWS, t)).astype(BF16)
                e1x_ref[h, i] = jnp.broadcast_to(e1_ref[h, pl.ds(i, 1), :], (_BF16_ROWS, t)).astype(BF16)
            return carry

        lax.fori_loop(0, N_KEYS, expand, 0)

    acc_ref[...] += jnp.dot(vt_ref[...], act_ref[(c + 1) & 1], preferred_element_type=F32)

    chunk = jnp.minimum(c, n_chunks - 1)
    n_i1 = ec // N_KEYS
    slot = c & 1
    for ii in range(n_i1):
        i1 = chunk * n_i1 + ii
        rows = slice(ii * N_KEYS, (ii + 1) * N_KEYS)
        hid = jnp.dot(u_ref[rows, :], xnt_ref[...], preferred_element_type=F32)
        wsum = jnp.zeros((N_KEYS, t), BF16)
        for h in range(PEER_HEADS):
            cut = jnp.concatenate([cutx_ref[h, i1]] * (N_KEYS // _BF16_ROWS), axis=0)
            e1 = jnp.concatenate([e1x_ref[h, i1]] * (N_KEYS // _BF16_ROWS), axis=0)
            wsum = wsum + jnp.where(rank2x_ref[h] < cut, e1 * e2x_ref[h], jnp.zeros((), BF16))
        gelu = 0.5 * hid * (1.0 + lax.erf(hid * (2.0 ** -0.5)))
        act_ref[slot, rows, :] = gelu.astype(BF16) * wsum

    @pl.when(c == n_chunks)
    def _():
        o_ref[...] = h_ref[...] + acc_ref[...].T


def _peer_dense(xnt, h, u, vt, route, t, ec):
    d, n = xnt.shape
    nc = u.shape[0] // ec
    big = pl.BlockSpec((PEER_HEADS, N_KEYS, t), lambda i, c: (0, 0, i))
    return pl.pallas_call(
        functools.partial(_peer_dense_kernel, t=t, ec=ec),
        grid=(n // t, nc + 1),
        in_specs=[
            pl.BlockSpec((d, t), lambda i, c: (0, i)),
            pl.BlockSpec((t, d), lambda i, c: (i, 0)),
            pl.BlockSpec((ec, d), lambda i, c: (jnp.minimum(c, nc - 1), 0)),
            pl.BlockSpec((d, ec), lambda i, c: (0, jnp.maximum(c - 1, 0))),
            big, big, big, big,
        ],
        out_specs=pl.BlockSpec((t, d), lambda i, c: (i, 0)),
        out_shape=SDS((n, d), F32),
        scratch_shapes=[pltpu.VMEM((d, t), F32), pltpu.VMEM((2, ec, t), BF16),
                        pltpu.VMEM((PEER_HEADS, N_KEYS, _BF16_ROWS, t), BF16),
                        pltpu.VMEM((PEER_HEADS, N_KEYS, _BF16_ROWS, t), BF16),
                        pltpu.VMEM((PEER_HEADS, N_KEYS, t), BF16), pltpu.VMEM((PEER_HEADS, N_KEYS, t), BF16)],
        compiler_params=_params("arbitrary", "arbitrary"),
        name="peer_dense",
    )(xnt, h, u, vt, *route)


def _final_norm_kernel(x_ref, g_ref, o_ref):
    o_ref[...] = _rms(x_ref[...], g_ref[...])


def _final_norm(x, g, tm):
    n, d = x.shape
    return pl.pallas_call(
        _final_norm_kernel,
        grid=(n // tm,),
        in_specs=[pl.BlockSpec((tm, d), lambda i: (i, 0)), pl.BlockSpec((1, d), lambda i: (0, 0))],
        out_specs=pl.BlockSpec((tm, d), lambda i: (i, 0)),
        out_shape=SDS((n, d), F32),
        compiler_params=_params("arbitrary"),
        name="final_norm",
    )(x, g)


def _tile(n, pref):
    return pref if n % pref == 0 else n


def _layer_weights(l, lbs, norm1_g, w_in, conv_dw, conv_db, conv_ln_g, conv_ln_b, hg_norm_g, att_fb,
                   w_branch, w_out, norm2_g, peer_wq, peer_keys, peer_u, peer_v):
    n_main = 2 * BRANCH_W + 4 * HG_HEADS * HG_DK + 3 * ATT_HEADS * ATT_HD
    w = w_in[l]
    row = lambda a: a.reshape(1, -1).astype(F32)
    return dict(
        g1=row(norm1_g[l]),
        w_main=jnp.concatenate([w[:, n_main + ATT_HEADS:], w[:, :n_main]], axis=1).astype(BF16),
        w_f=jnp.pad(w[:, n_main:n_main + ATT_HEADS], ((0, 0), (0, LANES - ATT_HEADS))).astype(BF16),
        fb=jnp.pad(att_fb[l].astype(F32), (0, LANES - ATT_HEADS)).reshape(1, LANES),
        conv_w=conv_dw[l].astype(F32), conv_b=row(conv_db[l]), ln_g=row(conv_ln_g[l]), ln_b=row(conv_ln_b[l]),
        lb=row(lbs[l]), hg_g=row(hg_norm_g[l]),
        wb=w_branch[l].astype(BF16), wo=w_out[l].astype(BF16), g2=row(norm2_g[l]),
        wqt=peer_wq[l].T.astype(BF16), keys=peer_keys[l].astype(BF16),
        u=peer_u[l].astype(BF16), vt=peer_v[l].T.astype(BF16),
    )


def _token_mixers(x, wt, nb, seq, conv_buf, hg_state, attend):
    n = nb * seq
    z, lf = _in_proj(x, wt["g1"], wt["w_main"], wt["w_f"], wt["fb"], _tile(n, 1024))
    y_conv, conv_new = _conv_branch(z, conv_buf, wt["conv_w"], wt["conv_b"], wt["ln_g"], wt["ln_b"],
                                    nb, seq, _tile(seq, 512))
    y_hgrn, hg_new = _hgrn_branch(z, hg_state, wt["lb"], wt["hg_g"], nb, seq, _tile(seq, 128), _tile(seq, 16))
    y_att = attend(z, lf)
    h, xn = _merge(x, y_conv, y_hgrn, y_att, z, wt["wb"], wt["wo"], wt["g2"], _tile(n, 512))
    tp = _tile(n, 256)
    route = _peer_route(xn, wt["wqt"], wt["keys"], tp)
    out = _peer_dense(xn, h, wt["u"], wt["vt"], route, tp, 512)
    w = ATT_HEADS * ATT_HD
    k_rows = z[:, ZB_AK * BRANCH_W:ZB_AK * BRANCH_W + w].reshape(nb, seq, ATT_HEADS, ATT_HD)
    v_rows = z[:, ZB_AV * BRANCH_W:ZB_AV * BRANCH_W + w].reshape(nb, seq, ATT_HEADS, ATT_HD)
    logf = lf[:, :ATT_HEADS].reshape(nb, seq, ATT_HEADS)
    return out, conv_new, hg_new, k_rows, v_rows, logf


def kernel(x_prompt, x_sample, cache_k, cache_v, cache_logf, state_conv, state_hgrn, page_table, norm1_g, w_in, conv_dw, conv_db, conv_ln_g, conv_ln_b, hg_lb_logits, hg_norm_g, att_fb, w_branch, w_out, norm2_g, peer_wq, peer_keys, peer_u, peer_v, final_g):
    depth = w_in.shape[0]
    bp, sp, d = x_prompt.shape
    bs, ss, _ = x_sample.shape
    w = ATT_HEADS * ATT_HD
    probs = jax.nn.softmax(hg_lb_logits.astype(F32), axis=0)
    lbs = jnp.cumsum(probs, axis=0) - probs[0:1]
    ck = cache_k.reshape(cache_k.shape[:3] + (w,))
    cv = cache_v.reshape(cache_v.shape[:3] + (w,))
    clf_t = jnp.swapaxes(cache_logf.astype(F32), 2, 3)

    hp = x_prompt.reshape(bp * sp, d)
    hs = x_sample.reshape(bs * ss, d)
    outs_p, outs_s = [], []
    for l in range(depth):
        wt = _layer_weights(l, lbs, norm1_g, w_in, conv_dw, conv_db, conv_ln_g, conv_ln_b, hg_norm_g, att_fb,
                            w_branch, w_out, norm2_g, peer_wq, peer_keys, peer_u, peer_v)

        def attend_prompt(z, lf):
            ct = _logf_cumsum(lf, bp, sp, _tile(sp, 512))
            return _fox_prompt(z, ct, bp, sp, _tile(sp, 512))

        def attend_sample(z, lf, l=l):
            lfn_t = jnp.swapaxes(lf[:, :ATT_HEADS].reshape(bs, ss, ATT_HEADS), 1, 2)
            lfn_t = jnp.pad(lfn_t, ((0, 0), (0, 0), (0, LANES - ss)))
            return _fox_sample(z, lfn_t, ck, cv, clf_t, page_table, l, bs, ss)

        hp, *rp = _token_mixers(hp, wt, bp, sp, jnp.zeros((bp, CONV_K - 1, BRANCH_W), F32),
                                jnp.zeros((bp, HG_HEADS, HG_DK, HG_DK), F32), attend_prompt)
        hs, *rs = _token_mixers(hs, wt, bs, ss, state_conv[l], state_hgrn[l], attend_sample)
        outs_p.append(rp)
        outs_s.append(rs)

    fg = final_g.reshape(1, d).astype(F32)
    y_prompt = _final_norm(hp, fg, _tile(bp * sp, 1024)).reshape(bp, sp, d)
    y_sample = _final_norm(hs, fg, _tile(bs * ss, 1024)).reshape(bs, ss, d)
    stack = lambda outs, i: jnp.stack([r[i] for r in outs])
    return (y_prompt, y_sample,
            stack(outs_p, 2), stack(outs_p, 3), stack(outs_p, 4), stack(outs_p, 0), stack(outs_p, 1),
            stack(outs_s, 2), stack(outs_s, 3), stack(outs_s, 4), stack(outs_s, 0), stack(outs_s, 1))
```

```python
import functools

import jax
import jax.numpy as jnp
from jax import lax
from jax.experimental import pallas as pl
from jax.experimental.pallas import tpu as pltpu

F32 = jnp.float32
BF16 = jnp.bfloat16
HI = lax.Precision.HIGHEST
SDS = jax.ShapeDtypeStruct

D_MODEL = 1024
BRANCH_W = 512
CONV_K = 31
HG_HEADS = 4
HG_DK = 128
ATT_HEADS = 8
ATT_HD = 64
PEER_HEADS = 8
N_KEYS = 128
PEER_TOPK = 16
PAGE_SIZE = 128
EPS = 1e-6
GATE_FLOOR = 1e-20
MASK_VALUE = -1e30
LANES = 128
_HD_SHIFT = ATT_HD.bit_length() - 1
_HEAD_SHIFT = ATT_HEADS.bit_length() - 1

Z_COLS = 15 * BRANCH_W
ZB_CONV = 3
ZB_HQ, ZB_HF, ZB_HI, ZB_HG, ZB_AQ, ZB_AK, ZB_AV = 8, 9, 10, 11, 12, 13, 14

_PAIRS = tuple((a, b) for a in range(PEER_TOPK) for b in range(PEER_TOPK) if (a + 1) * (b + 1) <= PEER_TOPK)

_NT = (((1,), (1,)), ((), ()))
_TN = (((0,), (0,)), ((), ()))


def _params(*sem, vmem_mb=48):
    return pltpu.CompilerParams(dimension_semantics=sem, vmem_limit_bytes=vmem_mb * 1024 * 1024)


def _sigmoid(x):
    return 1.0 / (1.0 + jnp.exp(-x))


def _log_sigmoid(x):
    return jnp.minimum(x, 0.0) - jnp.log1p(jnp.exp(-jnp.abs(x)))


def _rms(x, g):
    return x * lax.rsqrt(jnp.mean(x * x, axis=-1, keepdims=True) + EPS) * g


def _in_proj_kernel(x_ref, g_ref, w_ref, wf_ref, fb_ref, z_ref, lf_ref, xn_ref):
    @pl.when(pl.program_id(1) == 0)
    def _():
        xn = _rms(x_ref[...], g_ref[...]).astype(BF16)
        xn_ref[...] = xn
        af = jnp.dot(xn, wf_ref[...], preferred_element_type=F32) + fb_ref[...]
        lf_ref[...] = _log_sigmoid(af)

    z_ref[...] = jnp.dot(xn_ref[...], w_ref[...], preferred_element_type=F32)


def _in_proj(x, g, w_main, w_f, fb, tm):
    n, d = x.shape
    tn = BRANCH_W
    return pl.pallas_call(
        _in_proj_kernel,
        grid=(n // tm, Z_COLS // tn),
        in_specs=[
            pl.BlockSpec((tm, d), lambda i, j: (i, 0)),
            pl.BlockSpec((1, d), lambda i, j: (0, 0)),
            pl.BlockSpec((d, tn), lambda i, j: (0, j)),
            pl.BlockSpec((d, LANES), lambda i, j: (0, 0)),
            pl.BlockSpec((1, LANES), lambda i, j: (0, 0)),
        ],
        out_specs=[
            pl.BlockSpec((tm, tn), lambda i, j: (i, j)),
            pl.BlockSpec((tm, LANES), lambda i, j: (i, 0)),
        ],
        out_shape=[SDS((n, Z_COLS), F32), SDS((n, LANES), F32)],
        scratch_shapes=[pltpu.VMEM((tm, d), BF16)],
        compiler_params=_params("arbitrary", "arbitrary"),
        name="in_proj",
    )(x, g, w_main, w_f, fb)


_CONV_PAD = 32


def _conv_kernel(a_ref, buf_ref, w_ref, b_ref, g_ref, bt_ref, y_ref, nc_ref, ext_ref, *, tl):
    hist = CONV_K - 1
    lo = _CONV_PAD - hist

    @pl.when(pl.program_id(1) == 0)
    def _():
        ext_ref[lo:_CONV_PAD, :] = buf_ref[0]

    a = a_ref[...]
    ext_ref[_CONV_PAD:_CONV_PAD + tl, :] = a[:, :BRANCH_W] * _sigmoid(a[:, BRANCH_W:])
    acc = jnp.zeros((tl, BRANCH_W), F32)
    for j in range(CONV_K):
        acc = acc + w_ref[j:j + 1, :] * ext_ref[lo + j:lo + j + tl, :]
    y = acc + b_ref[...]
    mu = jnp.mean(y, axis=-1, keepdims=True)
    yc = y - mu
    var = jnp.mean(yc * yc, axis=-1, keepdims=True)
    y = yc * lax.rsqrt(var + EPS) * g_ref[...] + bt_ref[...]
    y_ref[...] = y * _sigmoid(y)
    tail = ext_ref[tl + lo:tl + _CONV_PAD, :]
    nc_ref[0] = tail
    ext_ref[lo:_CONV_PAD, :] = tail


def _conv_branch(z, buf, w, b, g, bt, nb, seq, tl):
    nt = seq // tl
    hist = CONV_K - 1
    vec = pl.BlockSpec((1, BRANCH_W), lambda i, t: (0, 0))
    return pl.pallas_call(
        functools.partial(_conv_kernel, tl=tl),
        grid=(nb, nt),
        in_specs=[
            pl.BlockSpec((tl, 2 * BRANCH_W), lambda i, t: (i * nt + t, ZB_CONV)),
            pl.BlockSpec((1, hist, BRANCH_W), lambda i, t: (i, 0, 0)),
            pl.BlockSpec((CONV_K, BRANCH_W), lambda i, t: (0, 0)),
            vec, vec, vec,
        ],
        out_specs=[
            pl.BlockSpec((tl, BRANCH_W), lambda i, t: (i * nt + t, 0)),
            pl.BlockSpec((1, hist, BRANCH_W), lambda i, t: (i, 0, 0)),
        ],
        out_shape=[SDS((nb * seq, BRANCH_W), F32), SDS((nb, hist, BRANCH_W), F32)],
        scratch_shapes=[pltpu.VMEM((_CONV_PAD + tl, BRANCH_W), F32)],
        compiler_params=_params("arbitrary", "arbitrary"),
        name="conv_branch",
    )(z, buf, w, b, g, bt)


def _hgrn_kernel(q_ref, f_ref, i_ref, g_ref, s0_ref, lb_ref, ng_ref, y_ref, sn_ref,
                 st_ref, qs_ref, ks_ref, bs_ref, *, tl, c):
    t = pl.program_id(1)

    @pl.when(t == 0)
    def _():
        for h in range(HG_HEADS):
            st_ref[h] = s0_ref[0, h].T

    lb = jnp.clip(lb_ref[...], 0.0, 1.0)
    f = lb + (1.0 - lb) * _sigmoid(f_ref[...])
    lf = jnp.log(jnp.maximum(f, GATE_FLOOR))
    q = q_ref[...]
    qs_ref[...] = q * _sigmoid(q)
    ks_ref[...] = 1.0 - f
    r = lax.broadcasted_iota(jnp.int32, (tl, tl), 0)
    cc = lax.broadcasted_iota(jnp.int32, (tl, tl), 1)
    tri = jnp.where(cc <= r, jnp.where(cc >= (r & -c), 1.0, 0.0), 0.0).astype(F32)
    bs_ref[...] = jnp.dot(tri, lf, precision=HI, preferred_element_type=F32)
    rowid = lax.broadcasted_iota(jnp.int32, (c, 1), 0)

    def chunk(ci, carry):
        off = pl.multiple_of(ci * c, c)
        for h in range(HG_HEADS):
            sl = slice(h * HG_DK, (h + 1) * HG_DK)
            qc = qs_ref[pl.ds(off, c), sl]
            kc = ks_ref[pl.ds(off, c), sl]
            bc = bs_ref[pl.ds(off, c), sl]
            vc = i_ref[pl.ds(off, c), sl]
            o = jnp.zeros((c, HG_DK), F32)
            for s in range(c):
                k_row = kc[s:s + 1, :]
                b_row = bc[s:s + 1, :]
                v_row = vc[s:s + 1, :]
                p = qc * k_row * jnp.exp(jnp.minimum(bc - b_row, 0.0))
                a = jnp.sum(p, axis=1, keepdims=True)
                o = o + jnp.where(rowid >= s, a, 0.0) * v_row
            st = st_ref[h]
            qt = (qc * jnp.exp(bc)).astype(BF16)
            o = o + lax.dot_general(qt, st.astype(BF16), _NT, preferred_element_type=F32)
            b_last = bc[c - 1:c, :]
            kt = (kc * jnp.exp(b_last - bc)).astype(BF16)
            upd = lax.dot_general(vc.astype(BF16), kt, _TN, preferred_element_type=F32)
            st_ref[h] = st * jnp.exp(b_last) + upd
            o = _rms(o, ng_ref[...])
            y_ref[pl.ds(off, c), sl] = o * _sigmoid(g_ref[pl.ds(off, c), sl])
        return carry

    lax.fori_loop(0, tl // c, chunk, 0)

    @pl.when(t == pl.num_programs(1) - 1)
    def _():
        for h in range(HG_HEADS):
            sn_ref[0, h] = st_ref[h].T


def _hgrn_branch(z, s0, lb, ng, nb, seq, tl, c):
    nt = seq // tl
    w = HG_HEADS * HG_DK

    def col(j):
        return pl.BlockSpec((tl, w), lambda i, t: (i * nt + t, j))

    st_spec = pl.BlockSpec((1, HG_HEADS, HG_DK, HG_DK), lambda i, t: (i, 0, 0, 0))
    return pl.pallas_call(
        functools.partial(_hgrn_kernel, tl=tl, c=c),
        grid=(nb, nt),
        in_specs=[col(ZB_HQ), col(ZB_HF), col(ZB_HI), col(ZB_HG), st_spec,
                  pl.BlockSpec((1, w), lambda i, t: (0, 0)),
                  pl.BlockSpec((1, HG_DK), lambda i, t: (0, 0))],
        out_specs=[pl.BlockSpec((tl, w), lambda i, t: (i * nt + t, 0)), st_spec],
        out_shape=[SDS((nb * seq, w), F32), SDS((nb, HG_HEADS, HG_DK, HG_DK), F32)],
        scratch_shapes=[pltpu.VMEM((HG_HEADS, HG_DK, HG_DK), F32),
                        pltpu.VMEM((tl, w), F32), pltpu.VMEM((tl, w), F32), pltpu.VMEM((tl, w), F32)],
        compiler_params=_params("arbitrary", "arbitrary"),
        name="hgrn_branch",
    )(z, z, z, z, s0, lb, ng)


def _cumsum_kernel(lf_ref, ct_ref, carry_ref, *, tl):
    @pl.when(pl.program_id(1) == 0)
    def _():
        carry_ref[...] = jnp.zeros_like(carry_ref)

    r = lax.broadcasted_iota(jnp.int32, (tl, tl), 0)
    cc = lax.broadcasted_iota(jnp.int32, (tl, tl), 1)
    tri = jnp.where(cc <= r, 1.0, 0.0).astype(F32)
    cs = jnp.dot(tri, lf_ref[...], precision=HI, preferred_element_type=F32) + carry_ref[...]
    carry_ref[...] = cs[tl - 1:tl, :]
    ct_ref[0] = cs.T[:ATT_HEADS, :]


def _logf_cumsum(lf, nb, seq, tl):
    nt = seq // tl
    return pl.pallas_call(
        functools.partial(_cumsum_kernel, tl=tl),
        grid=(nb, nt),
        in_specs=[pl.BlockSpec((tl, LANES), lambda i, t: (i * nt + t, 0))],
        out_specs=pl.BlockSpec((1, ATT_HEADS, tl), lambda i, t: (i, 0, t)),
        out_shape=SDS((nb, ATT_HEADS, seq), F32),
        scratch_shapes=[pltpu.VMEM((1, LANES), F32)],
        compiler_params=_params("arbitrary", "arbitrary"),
        name="logf_cumsum",
    )(lf)


_LOG2E = 1.4426950408889634
_FOX_BLOCK = 1024
_FOX_DIAG_SPLIT = 2


def _fox_kernel(q_ref, k_ref, v_ref, ck_ref, cq_ref, o_ref, qm_ref, m_ref, acc_ref, *, t):
    pair = pl.program_id(1)
    qi = pl.program_id(2)
    ki = pl.program_id(3)
    lane = lax.broadcasted_iota(jnp.int32, (1, LANES), 1)

    @pl.when(ki == 0)
    def _():
        q = q_ref[...] * (ATT_HD ** -0.5 * _LOG2E)
        qm_ref[0] = jnp.where(lane < ATT_HD, q, 0.0).astype(BF16)
        qm_ref[1] = jnp.where(lane >= ATT_HD, q, 0.0).astype(BF16)
        m_ref[...] = jnp.full_like(m_ref, MASK_VALUE)
        acc_ref[...] = jnp.zeros_like(acc_ref)

    def step(diagonal):
        k = k_ref[...].astype(BF16)
        v = jnp.concatenate([v_ref[...].astype(BF16), jnp.ones((t, LANES), BF16)], axis=1)
        head_iota = lax.broadcasted_iota(jnp.int32, (ATT_HEADS, 1), 0)
        for j in range(2):
            own = head_iota == 2 * pair + j
            ck = jnp.sum(jnp.where(own, ck_ref[0], 0.0), axis=0, keepdims=True)
            c0 = jnp.sum(jnp.where(own, cq_ref[0, :, 0:1], 0.0), axis=0, keepdims=True)
            bias = (c0 - ck) * _LOG2E
            qm, m_all, acc_all = qm_ref[j], m_ref[j], acc_ref[j]
            m_out, acc_out = [], []
            sub = t // _FOX_DIAG_SPLIT if diagonal else t
            for r0 in range(0, t, sub):
                nk = r0 + sub if diagonal else t
                rows = slice(r0, r0 + sub)
                s = lax.dot_general(qm[rows], k[:nk], _NT, preferred_element_type=F32) + bias[:, :nk]
                if diagonal:
                    rid = lax.broadcasted_iota(jnp.int32, (sub, nk), 0) + r0
                    cid = lax.broadcasted_iota(jnp.int32, (sub, nk), 1)
                    s = jnp.where(cid <= rid, s, MASK_VALUE)
                m_old = m_all[rows]
                m_new = jnp.maximum(m_old, jnp.max(s, axis=1, keepdims=True))
                alpha = jnp.exp2(m_old - m_new)
                p = jnp.exp2(s - m_new[:, 0:1]).astype(BF16)
                pv = jnp.dot(p, v[:nk], preferred_element_type=F32)
                acc_out.append(jnp.concatenate([alpha, alpha], axis=1) * acc_all[rows] + pv)
                m_out.append(m_new)
            acc_ref[j] = jnp.concatenate(acc_out, axis=0)
            m_ref[j] = jnp.concatenate(m_out, axis=0)

    @pl.when(ki < qi)
    def _():
        step(False)

    @pl.when(ki == qi)
    def _():
        step(True)
        o0 = acc_ref[0, :, :LANES] / acc_ref[0, :, LANES:]
        o1 = acc_ref[1, :, :LANES] / acc_ref[1, :, LANES:]
        o_ref[...] = jnp.where(lane < ATT_HD, o0, o1)


def _fox_prompt(z, ct, nb, seq, t):
    nq = seq // t
    npair = ATT_HEADS // 2
    zq, zk, zv = (zb * (BRANCH_W // LANES) for zb in (ZB_AQ, ZB_AK, ZB_AV))
    return pl.pallas_call(
        functools.partial(_fox_kernel, t=t),
        grid=(nb, npair, nq, nq),
        in_specs=[
            pl.BlockSpec((t, LANES), lambda b, p, qi, ki: (b * nq + qi, zq + p)),
            pl.BlockSpec((t, LANES), lambda b, p, qi, ki: (b * nq + jnp.minimum(ki, qi), zk + p)),
            pl.BlockSpec((t, LANES), lambda b, p, qi, ki: (b * nq + jnp.minimum(ki, qi), zv + p)),
            pl.BlockSpec((1, ATT_HEADS, t), lambda b, p, qi, ki: (b, 0, jnp.minimum(ki, qi))),
            pl.BlockSpec((1, ATT_HEADS, t), lambda b, p, qi, ki: (b, 0, qi)),
        ],
        out_specs=pl.BlockSpec((t, LANES), lambda b, p, qi, ki: (b * nq + qi, p)),
        out_shape=SDS((nb * seq, BRANCH_W), F32),
        scratch_shapes=[pltpu.VMEM((2, t, LANES), BF16), pltpu.VMEM((2, t, LANES), F32),
                        pltpu.VMEM((2, t, 2 * LANES), F32)],
        compiler_params=_params("arbitrary", "arbitrary", "arbitrary", "arbitrary"),
        name="fox_prompt",
    )(z, z, z, ct, ct)


def _page_suffix_kernel(lf_ref, suf_ref, tot_ref):
    r = lax.broadcasted_iota(jnp.int32, (PAGE_SIZE, PAGE_SIZE), 0)
    c = lax.broadcasted_iota(jnp.int32, (PAGE_SIZE, PAGE_SIZE), 1)
    lf = lf_ref[...]
    suf_ref[...] = jnp.dot(lf, jnp.where(r > c, 1.0, 0.0).astype(F32), precision=HI, preferred_element_type=F32)
    tot_ref[...] = jnp.dot(lf, jnp.ones((PAGE_SIZE, PAGE_SIZE), F32), precision=HI, preferred_element_type=F32)


def _page_suffix(lf_rows, tr):
    n = lf_rows.shape[0]
    spec = pl.BlockSpec((tr, PAGE_SIZE), lambda i: (i, 0))
    return pl.pallas_call(
        _page_suffix_kernel,
        grid=(n // tr,),
        in_specs=[spec],
        out_specs=[spec, spec],
        out_shape=[SDS((n, PAGE_SIZE), F32), SDS((n, PAGE_SIZE), F32)],
        compiler_params=_params("arbitrary"),
        name="page_suffix",
    )(lf_rows)


_PAGES_PER_STEP = 4


def _fox_sample_kernel(pt_ref, q_ref, kn_ref, vn_ref, lfn_ref, *rest, g, rows):
    del pt_ref
    kp_refs, vp_refs, pb_refs = rest[:g], rest[g:2 * g], rest[2 * g:3 * g]
    o_ref, m_ref, l_ref, acc_ref, car_ref, cnr_ref, cnc_ref = rest[3 * g:]
    step = pl.program_id(1)
    hm = ATT_HEADS - 1
    r_col = lax.broadcasted_iota(jnp.int32, (rows, 1), 0)
    c_new = lax.broadcasted_iota(jnp.int32, (rows, rows), 1)
    r_new = lax.broadcasted_iota(jnp.int32, (rows, rows), 0)
    ncol = PAGE_SIZE * ATT_HEADS
    same_head = (lax.broadcasted_iota(jnp.int32, (rows, ncol), 1) & hm) == (r_col & hm)

    def online(scores, values):
        m_old = m_ref[...]
        m_new = m_old
        for s in scores:
            m_new = jnp.maximum(m_new, jnp.max(s, axis=1, keepdims=True))
        alpha = jnp.exp(m_old - m_new)
        l_new = alpha * l_ref[...]
        acc = alpha * acc_ref[...]
        for s, v in zip(scores, values):
            p = jnp.exp(s - m_new)
            l_new = l_new + jnp.sum(p, axis=1, keepdims=True)
            acc = acc + jnp.dot(p.astype(BF16), v, preferred_element_type=F32)
        l_ref[...] = l_new
        acc_ref[...] = acc
        m_ref[...] = m_new

    @pl.when(step == 0)
    def _():
        m_ref[...] = jnp.full_like(m_ref, MASK_VALUE)
        l_ref[...] = jnp.zeros_like(l_ref)
        acc_ref[...] = jnp.zeros_like(acc_ref)
        car_ref[...] = jnp.zeros_like(car_ref)
        incl = jnp.where(((r_new & hm) == (c_new & hm)) & (r_new <= c_new), 1.0, 0.0).astype(F32)
        lfn = jnp.broadcast_to(lfn_ref[0], (8, rows))
        cn_row = jnp.dot(lfn, incl, precision=HI, preferred_element_type=F32)[0:1, :]
        cnr_ref[...] = cn_row
        cnc_ref[...] = jnp.sum(jnp.where(r_new == c_new, cn_row, 0.0), axis=1, keepdims=True)

    q = (q_ref[...] * (ATT_HD ** -0.5)).astype(BF16)
    cn_own = cnc_ref[...]
    car = car_ref[...]
    scores = []
    for i in range(g):
        s = lax.dot_general(q, kp_refs[i][0].astype(BF16), _NT, preferred_element_type=F32)
        scores.append(jnp.where(same_head, s + cn_own + (car + pb_refs[i][0, 0:1, :]), MASK_VALUE))
        car = car + pb_refs[i][0, 1:2, :]
    car_ref[...] = car
    online(scores, [vp_refs[i][0].astype(BF16) for i in range(g)])

    @pl.when(step == pl.num_programs(1) - 1)
    def _():
        sn = lax.dot_general(q, kn_ref[...].astype(BF16), _NT, preferred_element_type=F32)
        valid = ((r_new & hm) == (c_new & hm)) & ((c_new >> _HEAD_SHIFT) <= (r_new >> _HEAD_SHIFT))
        online([jnp.where(valid, sn + (cn_own - cnr_ref[...]), MASK_VALUE)], [vn_ref[...].astype(BF16)])
        o_ref[...] = acc_ref[...] / l_ref[...]


def _fox_sample(q2, k2, v2, lfn_row, ck, cv, page_bias, page_table, page_base, nb, rows):
    npg = page_table.shape[1]
    g = _PAGES_PER_STEP
    assert npg % g == 0
    ncol = PAGE_SIZE * ATT_HEADS

    def page(i):
        def index(b, p, pt):
            return (page_base + pt[b * npg + (npg - 1 - (p * g + i))], 0, 0)
        return index

    new = pl.BlockSpec((rows, ATT_HD), lambda b, p, pt: (b, 0))
    in_specs = [new, new, new, pl.BlockSpec((1, 1, rows), lambda b, p, pt: (b, 0, 0))]
    in_specs += [pl.BlockSpec((1, ncol, ATT_HD), page(i)) for i in range(g)]
    in_specs += [pl.BlockSpec((1, ncol, ATT_HD), page(i)) for i in range(g)]
    in_specs += [pl.BlockSpec((1, 2, ncol), page(i)) for i in range(g)]
    grid_spec = pltpu.PrefetchScalarGridSpec(
        num_scalar_prefetch=1,
        grid=(nb, npg // g),
        in_specs=in_specs,
        out_specs=new,
        scratch_shapes=[pltpu.VMEM((rows, 1), F32), pltpu.VMEM((rows, 1), F32), pltpu.VMEM((rows, ATT_HD), F32),
                        pltpu.VMEM((1, ncol), F32), pltpu.VMEM((1, rows), F32), pltpu.VMEM((rows, 1), F32)],
    )
    return pl.pallas_call(
        functools.partial(_fox_sample_kernel, g=g, rows=rows),
        grid_spec=grid_spec,
        out_shape=SDS((nb * rows, ATT_HD), F32),
        compiler_params=_params("arbitrary", "arbitrary"),
        name="fox_sample",
    )(page_table.reshape(-1), q2, k2, v2, lfn_row, *([ck] * g), *([cv] * g), *([page_bias] * g))


def _merge_kernel(x_ref, yc_ref, yh_ref, ya_ref, gt_ref, wb_ref, wo_ref, g2_ref, h_ref, xnt_ref):
    merged = jnp.zeros(x_ref.shape, F32)
    for b, y_ref in enumerate((yc_ref, yh_ref, ya_ref)):
        proj = jnp.dot(y_ref[...].astype(BF16), wb_ref[b], preferred_element_type=F32)
        merged = merged + _sigmoid(gt_ref[:, b * D_MODEL:(b + 1) * D_MODEL]) * proj
    h = x_ref[...] + jnp.dot(merged.astype(BF16), wo_ref[...], preferred_element_type=F32)
    h_ref[...] = h
    xnt_ref[...] = _rms(h, g2_ref[...]).T.astype(BF16)


def _merge(x, yc, yh, ya, z, wb, wo, g2, tm):
    n, d = x.shape
    br = pl.BlockSpec((tm, BRANCH_W), lambda i: (i, 0))
    return pl.pallas_call(
        _merge_kernel,
        grid=(n // tm,),
        in_specs=[
            pl.BlockSpec((tm, d), lambda i: (i, 0)), br, br, br,
            pl.BlockSpec((tm, 3 * d), lambda i: (i, 0)),
            pl.BlockSpec((3, BRANCH_W, d), lambda i: (0, 0, 0)),
            pl.BlockSpec((d, d), lambda i: (0, 0)),
            pl.BlockSpec((1, d), lambda i: (0, 0)),
        ],
        out_specs=[pl.BlockSpec((tm, d), lambda i: (i, 0)), pl.BlockSpec((d, tm), lambda i: (0, i))],
        out_shape=[SDS((n, d), F32), SDS((d, n), BF16)],
        compiler_params=_params("arbitrary"),
        name="merge",
    )(x, yc, yh, ya, z, wb, wo, g2)


def _take_max(work, row_iota, n_rows):
    m = jnp.max(work, axis=0, keepdims=True)
    first = jnp.min(jnp.where(work == m, row_iota, float(n_rows)), axis=0, keepdims=True)
    hit = row_iota == first
    return m, jnp.where(hit, -jnp.inf, work), hit


_N_CAND = -(-len(_PAIRS) // 8) * 8


def _peer_route_kernel(xnt_ref, wqt_ref, keys_ref, cut1_ref, e1_ref, rank2_ref, e2_ref, cand_ref, *, t):
    xnt = xnt_ref[...]
    key_iota = lax.broadcasted_iota(jnp.int32, (N_KEYS, t), 0).astype(F32)
    n_cand = _N_CAND
    cand_iota = lax.broadcasted_iota(jnp.int32, (n_cand, t), 0).astype(F32)
    cand_ref[len(_PAIRS):, :] = jnp.full((n_cand - len(_PAIRS), t), -jnp.inf, F32)

    def head(h, carry):
        dk = 2 * N_KEYS
        qt = jnp.dot(wqt_ref[pl.ds(pl.multiple_of(h * dk, dk), dk), :], xnt, preferred_element_type=F32)
        scores, tops = [], []
        for p in range(2):
            s = jnp.dot(keys_ref[h, p], qt[p * N_KEYS:(p + 1) * N_KEYS].astype(BF16), preferred_element_type=F32)
            scores.append(s)
            vals, work = [], s
            rank = jnp.full((N_KEYS, t), float(PEER_TOPK), F32)
            for i in range(PEER_TOPK):
                m, work, hit = _take_max(work, key_iota, N_KEYS)
                vals.append(m)
                if p == 1:
                    rank = jnp.where(hit, float(i), rank)
            tops.append(vals)
        for i, (a, b) in enumerate(_PAIRS):
            cand_ref[i:i + 1, :] = tops[0][a] + tops[1][b]
        work = cand_ref[...]
        best = []
        for _ in range(PEER_TOPK):
            m, work, _ = _take_max(work, cand_iota, n_cand)
            best.append(m)
        norm = jnp.zeros((1, t), F32)
        for m in best:
            norm = norm + jnp.exp(m - best[0])
        cut = jnp.zeros((N_KEYS, t), F32)
        for b in range(PEER_TOPK):
            cut = cut + jnp.where(scores[0] + tops[1][b] >= best[-1], 1.0, 0.0)
        cut1_ref[h] = cut
        e1_ref[h] = jnp.exp(scores[0] - tops[0][0]) / norm
        rank2_ref[h] = rank.astype(BF16)
        e2_ref[h] = jnp.exp(scores[1] - tops[1][0]).astype(BF16)
        return carry

    lax.fori_loop(0, PEER_HEADS, head, 0)


def _peer_route(xnt, wqt, keys, t):
    d, n = xnt.shape
    big = pl.BlockSpec((PEER_HEADS, N_KEYS, t), lambda i: (0, 0, i))
    wide, narrow = SDS((PEER_HEADS, N_KEYS, n), F32), SDS((PEER_HEADS, N_KEYS, n), BF16)
    return pl.pallas_call(
        functools.partial(_peer_route_kernel, t=t),
        grid=(n // t,),
        in_specs=[
            pl.BlockSpec((d, t), lambda i: (0, i)),
            pl.BlockSpec(wqt.shape, lambda i: (0, 0)),
            pl.BlockSpec(keys.shape, lambda i: (0, 0, 0, 0)),
        ],
        out_specs=[big, big, big, big],
        out_shape=[wide, wide, narrow, narrow],
        scratch_shapes=[pltpu.VMEM((_N_CAND, t), F32)],
        compiler_params=_params("arbitrary"),
        name="peer_route",
    )(xnt, wqt, keys)


_BF16_ROWS = 16
_PEER_CHUNK = 2048


def _peer_dense_kernel(xnt_ref, h_ref, u_ref, vt_ref, cut1_ref, e1_ref, rank2_ref, e2_ref, o_ref,
                       acc_ref, act_ref, cutx_ref, e1x_ref, rank2x_ref, e2x_ref, *, t, ec):
    c = pl.program_id(1)
    n_chunks = pl.num_programs(1) - 1

    @pl.when(c == 0)
    def _():
        acc_ref[...] = jnp.zeros_like(acc_ref)
        act_ref[1] = jnp.zeros((ec, t), BF16)
        rank2x_ref[...] = rank2_ref[...]
        e2x_ref[...] = e2_ref[...]

        def expand(i, carry):
            for h in range(PEER_HEADS):
                cutx_ref[h, i] = jnp.broadcast_to(cut1_ref[h, pl.ds(i, 1), :], (_BF16_ROWS, t)).astype(BF16)
                e1x_ref[h, i] = jnp.broadcast_to(e1_ref[h, pl.ds(i, 1), :], (_BF16_ROWS, t)).astype(BF16)
            return carry

        lax.fori_loop(0, N_KEYS, expand, 0)

    acc_ref[...] += jnp.dot(vt_ref[...], act_ref[(c + 1) & 1], preferred_element_type=F32)

    chunk = jnp.minimum(c, n_chunks - 1)
    n_i1 = ec // N_KEYS
    slot = c & 1
    for ii in range(n_i1):
        i1 = chunk * n_i1 + ii
        rows = slice(ii * N_KEYS, (ii + 1) * N_KEYS)
        hid = jnp.dot(u_ref[rows, :], xnt_ref[...], preferred_element_type=F32)
        wsum = jnp.zeros((N_KEYS, t), BF16)
        for h in range(PEER_HEADS):
            cut = jnp.concatenate([cutx_ref[h, i1]] * (N_KEYS // _BF16_ROWS), axis=0)
            e1 = jnp.concatenate([e1x_ref[h, i1]] * (N_KEYS // _BF16_ROWS), axis=0)
            wsum = wsum + jnp.where(rank2x_ref[h] < cut, e1 * e2x_ref[h], jnp.zeros((), BF16))
        gelu = 0.5 * hid * (1.0 + lax.erf(hid * (2.0 ** -0.5)))
        act_ref[slot, rows, :] = gelu.astype(BF16) * wsum

    @pl.when(c == n_chunks)
    def _():
        o_ref[...] = h_ref[...] + acc_ref[...].T


def _peer_dense(xnt, h, u, vt, route, t, ec):
    d, n = xnt.shape
    nc = u.shape[0] // ec
    big = pl.BlockSpec((PEER_HEADS, N_KEYS, t), lambda i, c: (0, 0, i))
    return pl.pallas_call(
        functools.partial(_peer_dense_kernel, t=t, ec=ec),
        grid=(n // t, nc + 1),
        in_specs=[
            pl.BlockSpec((d, t), lambda i, c: (0, i)),
            pl.BlockSpec((t, d), lambda i, c: (i, 0)),
            pl.BlockSpec((ec, d), lambda i, c: (jnp.minimum(c, nc - 1), 0)),
            pl.BlockSpec((d, ec), lambda i, c: (0, jnp.maximum(c - 1, 0))),
            big, big, big, big,
        ],
        out_specs=pl.BlockSpec((t, d), lambda i, c: (i, 0)),
        out_shape=SDS((n, d), F32),
        scratch_shapes=[pltpu.VMEM((d, t), F32), pltpu.VMEM((2, ec, t), BF16),
                        pltpu.VMEM((PEER_HEADS, N_KEYS, _BF16_ROWS, t), BF16),
                        pltpu.VMEM((PEER_HEADS, N_KEYS, _BF16_ROWS, t), BF16),
                        pltpu.VMEM((PEER_HEADS, N_KEYS, t), BF16), pltpu.VMEM((PEER_HEADS, N_KEYS, t), BF16)],
        compiler_params=_params("arbitrary", "arbitrary", vmem_mb=56),
        name="peer_dense",
    )(xnt, h, u, vt, *route)


def _final_norm_kernel(x_ref, g_ref, o_ref):
    o_ref[...] = _rms(x_ref[...], g_ref[...])


def _final_norm(x, g, tm):
    n, d = x.shape
    return pl.pallas_call(
        _final_norm_kernel,
        grid=(n // tm,),
        in_specs=[pl.BlockSpec((tm, d), lambda i: (i, 0)), pl.BlockSpec((1, d), lambda i: (0, 0))],
        out_specs=pl.BlockSpec((tm, d), lambda i: (i, 0)),
        out_shape=SDS((n, d), F32),
        compiler_params=_params("arbitrary"),
        name="final_norm",
    )(x, g)


def _tile(n, pref):
    return pref if n % pref == 0 else n


def _layer_weights(l, lbs, norm1_g, w_in, conv_dw, conv_db, conv_ln_g, conv_ln_b, hg_norm_g, att_fb,
                   w_branch, w_out, norm2_g, peer_wq, peer_keys, peer_u, peer_v):
    n_main = 2 * BRANCH_W + 4 * HG_HEADS * HG_DK + 3 * ATT_HEADS * ATT_HD
    w = w_in[l]
    row = lambda a: a.reshape(1, -1).astype(F32)
    return dict(
        g1=row(norm1_g[l]),
        w_main=jnp.concatenate([w[:, n_main + ATT_HEADS:], w[:, :n_main]], axis=1).astype(BF16),
        w_f=jnp.pad(w[:, n_main:n_main + ATT_HEADS], ((0, 0), (0, LANES - ATT_HEADS))).astype(BF16),
        fb=jnp.pad(att_fb[l].astype(F32), (0, LANES - ATT_HEADS)).reshape(1, LANES),
        conv_w=conv_dw[l].astype(F32), conv_b=row(conv_db[l]), ln_g=row(conv_ln_g[l]), ln_b=row(conv_ln_b[l]),
        lb=row(lbs[l]), hg_g=row(hg_norm_g[l]),
        wb=w_branch[l].astype(BF16), wo=w_out[l].astype(BF16), g2=row(norm2_g[l]),
        wqt=peer_wq[l].T.astype(BF16), keys=peer_keys[l].astype(BF16),
        u=peer_u[l].astype(BF16), vt=peer_v[l].T.astype(BF16),
    )


def _token_mixers(x, wt, nb, seq, conv_buf, hg_state, attend):
    n = nb * seq
    z, lf = _in_proj(x, wt["g1"], wt["w_main"], wt["w_f"], wt["fb"], _tile(n, 1024))
    y_conv, conv_new = _conv_branch(z, conv_buf, wt["conv_w"], wt["conv_b"], wt["ln_g"], wt["ln_b"],
                                    nb, seq, _tile(seq, 512))
    y_hgrn, hg_new = _hgrn_branch(z, hg_state, wt["lb"], wt["hg_g"], nb, seq, _tile(seq, 128), _tile(seq, 16))
    y_att = attend(z, lf)
    h, xn = _merge(x, y_conv, y_hgrn, y_att, z, wt["wb"], wt["wo"], wt["g2"], _tile(n, 512))
    tp = _tile(n, 256)
    route = _peer_route(xn, wt["wqt"], wt["keys"], tp)
    out = _peer_dense(xn, h, wt["u"], wt["vt"], route, tp, _PEER_CHUNK)
    w = ATT_HEADS * ATT_HD
    k_rows = z[:, ZB_AK * BRANCH_W:ZB_AK * BRANCH_W + w].reshape(nb, seq, ATT_HEADS, ATT_HD)
    v_rows = z[:, ZB_AV * BRANCH_W:ZB_AV * BRANCH_W + w].reshape(nb, seq, ATT_HEADS, ATT_HD)
    logf = lf[:, :ATT_HEADS].reshape(nb, seq, ATT_HEADS)
    return out, conv_new, hg_new, k_rows, v_rows, logf


def kernel(x_prompt, x_sample, cache_k, cache_v, cache_logf, state_conv, state_hgrn, page_table, norm1_g, w_in, conv_dw, conv_db, conv_ln_g, conv_ln_b, hg_lb_logits, hg_norm_g, att_fb, w_branch, w_out, norm2_g, peer_wq, peer_keys, peer_u, peer_v, final_g):
    depth = w_in.shape[0]
    bp, sp, d = x_prompt.shape
    bs, ss, _ = x_sample.shape
    w = ATT_HEADS * ATT_HD
    probs = jax.nn.softmax(hg_lb_logits.astype(F32), axis=0)
    lbs = jnp.cumsum(probs, axis=0) - probs[0:1]
    n_phys = cache_k.shape[1]
    ncol = PAGE_SIZE * ATT_HEADS
    ck = cache_k.reshape(depth * n_phys, ncol, ATT_HD)
    cv = cache_v.reshape(depth * n_phys, ncol, ATT_HD)
    lf_rows = jnp.swapaxes(cache_logf.astype(F32), 2, 3).reshape(depth * n_phys * ATT_HEADS, PAGE_SIZE)
    page_bias = jnp.stack(
        [jnp.swapaxes(a.reshape(depth * n_phys, ATT_HEADS, PAGE_SIZE), 1, 2).reshape(depth * n_phys, ncol)
         for a in _page_suffix(lf_rows, _tile(lf_rows.shape[0], 2048))], axis=1)

    hp = x_prompt.reshape(bp * sp, d)
    hs = x_sample.reshape(bs * ss, d)
    outs_p, outs_s = [], []
    for l in range(depth):
        wt = _layer_weights(l, lbs, norm1_g, w_in, conv_dw, conv_db, conv_ln_g, conv_ln_b, hg_norm_g, att_fb,
                            w_branch, w_out, norm2_g, peer_wq, peer_keys, peer_u, peer_v)

        def attend_prompt(z, lf):
            ct = _logf_cumsum(lf, bp, sp, _tile(sp, 512))
            return _fox_prompt(z, ct, bp, sp, _tile(sp, _FOX_BLOCK))

        def attend_sample(z, lf, l=l):
            rows = ss * ATT_HEADS
            q2, k2, v2 = (z[:, zb * BRANCH_W:zb * BRANCH_W + w].reshape(bs * rows, ATT_HD)
                          for zb in (ZB_AQ, ZB_AK, ZB_AV))
            lfn_row = lf[:, :ATT_HEADS].reshape(bs, 1, rows)
            o = _fox_sample(q2, k2, v2, lfn_row, ck, cv, page_bias, page_table, l * n_phys, bs, rows)
            return o.reshape(bs * ss, w)

        hp, *rp = _token_mixers(hp, wt, bp, sp, jnp.zeros((bp, CONV_K - 1, BRANCH_W), F32),
                                jnp.zeros((bp, HG_HEADS, HG_DK, HG_DK), F32), attend_prompt)
        hs, *rs = _token_mixers(hs, wt, bs, ss, state_conv[l], state_hgrn[l], attend_sample)
        outs_p.append(rp)
        outs_s.append(rs)

    fg = final_g.reshape(1, d).astype(F32)
    y_prompt = _final_norm(hp, fg, _tile(bp * sp, 1024)).reshape(bp, sp, d)
    y_sample = _final_norm(hs, fg, _tile(bs * ss, 1024)).reshape(bs, ss, d)
    stack = lambda outs, i: jnp.stack([r[i] for r in outs])
    return (y_prompt, y_sample,
            stack(outs_p, 2), stack(outs_p, 3), stack(outs_p, 4), stack(outs_p, 0), stack(outs_p, 1),
            stack(outs_s, 2), stack(outs_s, 3), stack(outs_s, 4), stack(outs_s, 0), stack(outs_s, 1))
```

```python
import functools

import jax
import jax.numpy as jnp
from jax import lax
from jax.experimental import pallas as pl
from jax.experimental.pallas import tpu as pltpu

F32 = jnp.float32
BF16 = jnp.bfloat16
HI = lax.Precision.HIGHEST
SDS = jax.ShapeDtypeStruct

D_MODEL = 1024
BRANCH_W = 512
CONV_K = 31
HG_HEADS = 4
HG_DK = 128
ATT_HEADS = 8
ATT_HD = 64
PEER_HEADS = 8
N_KEYS = 128
PEER_TOPK = 16
PAGE_SIZE = 128
EPS = 1e-6
GATE_FLOOR = 1e-20
MASK_VALUE = -1e30
LANES = 128
_HD_SHIFT = ATT_HD.bit_length() - 1
_HEAD_SHIFT = ATT_HEADS.bit_length() - 1

Z_COLS = 15 * BRANCH_W
ZB_CONV = 3
ZB_HQ, ZB_HF, ZB_HI, ZB_HG, ZB_AQ, ZB_AK, ZB_AV = 8, 9, 10, 11, 12, 13, 14

_PAIRS = tuple((a, b) for a in range(PEER_TOPK) for b in range(PEER_TOPK) if (a + 1) * (b + 1) <= PEER_TOPK)

_NT = (((1,), (1,)), ((), ()))
_TN = (((0,), (0,)), ((), ()))


def _params(*sem, vmem_mb=48):
    return pltpu.CompilerParams(dimension_semantics=sem, vmem_limit_bytes=vmem_mb * 1024 * 1024)


def _sigmoid(x):
    return 1.0 / (1.0 + jnp.exp(-x))


def _log_sigmoid(x):
    return jnp.minimum(x, 0.0) - jnp.log1p(jnp.exp(-jnp.abs(x)))


def _rms(x, g):
    return x * lax.rsqrt(jnp.mean(x * x, axis=-1, keepdims=True) + EPS) * g


def _in_proj_kernel(x_ref, g_ref, w_ref, wf_ref, fb_ref, z_ref, lf_ref, xn_ref):
    @pl.when(pl.program_id(1) == 0)
    def _():
        xn = _rms(x_ref[...], g_ref[...]).astype(BF16)
        xn_ref[...] = xn
        af = jnp.dot(xn, wf_ref[...], preferred_element_type=F32) + fb_ref[...]
        lf_ref[...] = _log_sigmoid(af)

    z_ref[...] = jnp.dot(xn_ref[...], w_ref[...], preferred_element_type=F32)


def _in_proj(x, g, w_main, w_f, fb, tm):
    n, d = x.shape
    tn = 3 * BRANCH_W
    return pl.pallas_call(
        _in_proj_kernel,
        grid=(n // tm, Z_COLS // tn),
        in_specs=[
            pl.BlockSpec((tm, d), lambda i, j: (i, 0)),
            pl.BlockSpec((1, d), lambda i, j: (0, 0)),
            pl.BlockSpec((d, tn), lambda i, j: (0, j)),
            pl.BlockSpec((d, LANES), lambda i, j: (0, 0)),
            pl.BlockSpec((1, LANES), lambda i, j: (0, 0)),
        ],
        out_specs=[
            pl.BlockSpec((tm, tn), lambda i, j: (i, j)),
            pl.BlockSpec((tm, LANES), lambda i, j: (i, 0)),
        ],
        out_shape=[SDS((n, Z_COLS), F32), SDS((n, LANES), F32)],
        scratch_shapes=[pltpu.VMEM((tm, d), BF16)],
        compiler_params=_params("arbitrary", "arbitrary"),
        name="in_proj",
    )(x, g, w_main, w_f, fb)


_CONV_PAD = 32


def _conv_kernel(a_ref, buf_ref, w_ref, b_ref, g_ref, bt_ref, y_ref, nc_ref, ext_ref, *, tl):
    hist = CONV_K - 1
    lo = _CONV_PAD - hist

    @pl.when(pl.program_id(1) == 0)
    def _():
        ext_ref[lo:_CONV_PAD, :] = buf_ref[0]

    a = a_ref[...]
    ext_ref[_CONV_PAD:_CONV_PAD + tl, :] = a[:, :BRANCH_W] * _sigmoid(a[:, BRANCH_W:])
    acc = jnp.zeros((tl, BRANCH_W), F32)
    for j in range(CONV_K):
        acc = acc + w_ref[j:j + 1, :] * ext_ref[lo + j:lo + j + tl, :]
    y = acc + b_ref[...]
    mu = jnp.mean(y, axis=-1, keepdims=True)
    yc = y - mu
    var = jnp.mean(yc * yc, axis=-1, keepdims=True)
    y = yc * lax.rsqrt(var + EPS) * g_ref[...] + bt_ref[...]
    y_ref[...] = y * _sigmoid(y)
    tail = ext_ref[tl + lo:tl + _CONV_PAD, :]
    nc_ref[0] = tail
    ext_ref[lo:_CONV_PAD, :] = tail


def _conv_branch(z, buf, w, b, g, bt, nb, seq, tl):
    nt = seq // tl
    hist = CONV_K - 1
    vec = pl.BlockSpec((1, BRANCH_W), lambda i, t: (0, 0))
    return pl.pallas_call(
        functools.partial(_conv_kernel, tl=tl),
        grid=(nb, nt),
        in_specs=[
            pl.BlockSpec((tl, 2 * BRANCH_W), lambda i, t: (i * nt + t, ZB_CONV)),
            pl.BlockSpec((1, hist, BRANCH_W), lambda i, t: (i, 0, 0)),
            pl.BlockSpec((CONV_K, BRANCH_W), lambda i, t: (0, 0)),
            vec, vec, vec,
        ],
        out_specs=[
            pl.BlockSpec((tl, BRANCH_W), lambda i, t: (i * nt + t, 0)),
            pl.BlockSpec((1, hist, BRANCH_W), lambda i, t: (i, 0, 0)),
        ],
        out_shape=[SDS((nb * seq, BRANCH_W), F32), SDS((nb, hist, BRANCH_W), F32)],
        scratch_shapes=[pltpu.VMEM((_CONV_PAD + tl, BRANCH_W), F32)],
        compiler_params=_params("arbitrary", "arbitrary"),
        name="conv_branch",
    )(z, buf, w, b, g, bt)


def _hgrn_kernel(q_ref, f_ref, i_ref, g_ref, s0_ref, lb_ref, ng_ref, y_ref, sn_ref,
                 st_ref, qs_ref, ks_ref, bs_ref, *, tl, c):
    t = pl.program_id(1)

    @pl.when(t == 0)
    def _():
        for h in range(HG_HEADS):
            st_ref[h] = s0_ref[0, h].T

    lb = jnp.clip(lb_ref[...], 0.0, 1.0)
    f = lb + (1.0 - lb) * _sigmoid(f_ref[...])
    lf = jnp.log(jnp.maximum(f, GATE_FLOOR))
    q = q_ref[...]
    qs_ref[...] = q * _sigmoid(q)
    ks_ref[...] = 1.0 - f
    r = lax.broadcasted_iota(jnp.int32, (tl, tl), 0)
    cc = lax.broadcasted_iota(jnp.int32, (tl, tl), 1)
    tri = jnp.where(cc <= r, jnp.where(cc >= (r & -c), 1.0, 0.0), 0.0).astype(F32)
    bs_ref[...] = jnp.dot(tri, lf, precision=HI, preferred_element_type=F32)
    rowid = lax.broadcasted_iota(jnp.int32, (c, 1), 0)

    def chunk(ci, carry):
        off = pl.multiple_of(ci * c, c)
        for h in range(HG_HEADS):
            sl = slice(h * HG_DK, (h + 1) * HG_DK)
            qc = qs_ref[pl.ds(off, c), sl]
            kc = ks_ref[pl.ds(off, c), sl]
            bc = bs_ref[pl.ds(off, c), sl]
            vc = i_ref[pl.ds(off, c), sl]
            o = jnp.zeros((c, HG_DK), F32)
            for s in range(c):
                k_row = kc[s:s + 1, :]
                b_row = bc[s:s + 1, :]
                v_row = vc[s:s + 1, :]
                p = qc * k_row * jnp.exp(jnp.minimum(bc - b_row, 0.0))
                a = jnp.sum(p, axis=1, keepdims=True)
                o = o + jnp.where(rowid >= s, a, 0.0) * v_row
            st = st_ref[h]
            qt = (qc * jnp.exp(bc)).astype(BF16)
            o = o + lax.dot_general(qt, st.astype(BF16), _NT, preferred_element_type=F32)
            b_last = bc[c - 1:c, :]
            kt = (kc * jnp.exp(b_last - bc)).astype(BF16)
            upd = lax.dot_general(vc.astype(BF16), kt, _TN, preferred_element_type=F32)
            st_ref[h] = st * jnp.exp(b_last) + upd
            o = _rms(o, ng_ref[...])
            y_ref[pl.ds(off, c), sl] = o * _sigmoid(g_ref[pl.ds(off, c), sl])
        return carry

    lax.fori_loop(0, tl // c, chunk, 0)

    @pl.when(t == pl.num_programs(1) - 1)
    def _():
        for h in range(HG_HEADS):
            sn_ref[0, h] = st_ref[h].T


def _hgrn_branch(z, s0, lb, ng, nb, seq, tl, c):
    nt = seq // tl
    w = HG_HEADS * HG_DK

    def col(j):
        return pl.BlockSpec((tl, w), lambda i, t: (i * nt + t, j))

    st_spec = pl.BlockSpec((1, HG_HEADS, HG_DK, HG_DK), lambda i, t: (i, 0, 0, 0))
    return pl.pallas_call(
        functools.partial(_hgrn_kernel, tl=tl, c=c),
        grid=(nb, nt),
        in_specs=[col(ZB_HQ), col(ZB_HF), col(ZB_HI), col(ZB_HG), st_spec,
                  pl.BlockSpec((1, w), lambda i, t: (0, 0)),
                  pl.BlockSpec((1, HG_DK), lambda i, t: (0, 0))],
        out_specs=[pl.BlockSpec((tl, w), lambda i, t: (i * nt + t, 0)), st_spec],
        out_shape=[SDS((nb * seq, w), F32), SDS((nb, HG_HEADS, HG_DK, HG_DK), F32)],
        scratch_shapes=[pltpu.VMEM((HG_HEADS, HG_DK, HG_DK), F32),
                        pltpu.VMEM((tl, w), F32), pltpu.VMEM((tl, w), F32), pltpu.VMEM((tl, w), F32)],
        compiler_params=_params("arbitrary", "arbitrary"),
        name="hgrn_branch",
    )(z, z, z, z, s0, lb, ng)


def _cumsum_kernel(lf_ref, ct_ref, carry_ref, *, tl):
    @pl.when(pl.program_id(1) == 0)
    def _():
        carry_ref[...] = jnp.zeros_like(carry_ref)

    r = lax.broadcasted_iota(jnp.int32, (tl, tl), 0)
    cc = lax.broadcasted_iota(jnp.int32, (tl, tl), 1)
    tri = jnp.where(cc <= r, 1.0, 0.0).astype(F32)
    cs = jnp.dot(tri, lf_ref[...], precision=HI, preferred_element_type=F32) + carry_ref[...]
    carry_ref[...] = cs[tl - 1:tl, :]
    ct_ref[0] = cs.T[:ATT_HEADS, :]


def _logf_cumsum(lf, nb, seq, tl):
    nt = seq // tl
    return pl.pallas_call(
        functools.partial(_cumsum_kernel, tl=tl),
        grid=(nb, nt),
        in_specs=[pl.BlockSpec((tl, LANES), lambda i, t: (i * nt + t, 0))],
        out_specs=pl.BlockSpec((1, ATT_HEADS, tl), lambda i, t: (i, 0, t)),
        out_shape=SDS((nb, ATT_HEADS, seq), F32),
        scratch_shapes=[pltpu.VMEM((1, LANES), F32)],
        compiler_params=_params("arbitrary", "arbitrary"),
        name="logf_cumsum",
    )(lf)


_LOG2E = 1.4426950408889634
_FOX_BLOCK = 1024
_FOX_DIAG_SPLIT = 2


def _fox_kernel(q_ref, k_ref, v_ref, ck_ref, cq_ref, o_ref, qm_ref, m_ref, acc_ref, *, t):
    pair = pl.program_id(1)
    qi = pl.program_id(2)
    ki = pl.program_id(3)
    lane = lax.broadcasted_iota(jnp.int32, (1, LANES), 1)

    @pl.when(ki == 0)
    def _():
        q = q_ref[...] * (ATT_HD ** -0.5 * _LOG2E)
        qm_ref[0] = jnp.where(lane < ATT_HD, q, 0.0).astype(BF16)
        qm_ref[1] = jnp.where(lane >= ATT_HD, q, 0.0).astype(BF16)
        m_ref[...] = jnp.full_like(m_ref, MASK_VALUE)
        acc_ref[...] = jnp.zeros_like(acc_ref)

    def step(diagonal):
        k = k_ref[...].astype(BF16)
        v = jnp.concatenate([v_ref[...].astype(BF16), jnp.ones((t, LANES), BF16)], axis=1)
        head_iota = lax.broadcasted_iota(jnp.int32, (ATT_HEADS, 1), 0)
        for j in range(2):
            own = head_iota == 2 * pair + j
            ck = jnp.sum(jnp.where(own, ck_ref[0], 0.0), axis=0, keepdims=True)
            c0 = jnp.sum(jnp.where(own, cq_ref[0, :, 0:1], 0.0), axis=0, keepdims=True)
            bias = (c0 - ck) * _LOG2E
            qm, m_all, acc_all = qm_ref[j], m_ref[j], acc_ref[j]
            m_out, acc_out = [], []
            sub = t // _FOX_DIAG_SPLIT if diagonal else t
            for r0 in range(0, t, sub):
                nk = r0 + sub if diagonal else t
                rows = slice(r0, r0 + sub)
                s = lax.dot_general(qm[rows], k[:nk], _NT, preferred_element_type=F32) + bias[:, :nk]
                if diagonal:
                    rid = lax.broadcasted_iota(jnp.int32, (sub, nk), 0) + r0
                    cid = lax.broadcasted_iota(jnp.int32, (sub, nk), 1)
                    s = jnp.where(cid <= rid, s, MASK_VALUE)
                m_old = m_all[rows]
                m_new = jnp.maximum(m_old, jnp.max(s, axis=1, keepdims=True))
                alpha = jnp.exp2(m_old - m_new)
                p = jnp.exp2(s - m_new[:, 0:1]).astype(BF16)
                pv = jnp.dot(p, v[:nk], preferred_element_type=F32)
                acc_out.append(jnp.concatenate([alpha, alpha], axis=1) * acc_all[rows] + pv)
                m_out.append(m_new)
            acc_ref[j] = jnp.concatenate(acc_out, axis=0)
            m_ref[j] = jnp.concatenate(m_out, axis=0)

    @pl.when(ki < qi)
    def _():
        step(False)

    @pl.when(ki == qi)
    def _():
        step(True)
        o0 = acc_ref[0, :, :LANES] / acc_ref[0, :, LANES:]
        o1 = acc_ref[1, :, :LANES] / acc_ref[1, :, LANES:]
        o_ref[...] = jnp.where(lane < ATT_HD, o0, o1)


def _fox_prompt(z, ct, nb, seq, t):
    nq = seq // t
    npair = ATT_HEADS // 2
    zq, zk, zv = (zb * (BRANCH_W // LANES) for zb in (ZB_AQ, ZB_AK, ZB_AV))
    return pl.pallas_call(
        functools.partial(_fox_kernel, t=t),
        grid=(nb, npair, nq, nq),
        in_specs=[
            pl.BlockSpec((t, LANES), lambda b, p, qi, ki: (b * nq + qi, zq + p)),
            pl.BlockSpec((t, LANES), lambda b, p, qi, ki: (b * nq + jnp.minimum(ki, qi), zk + p)),
            pl.BlockSpec((t, LANES), lambda b, p, qi, ki: (b * nq + jnp.minimum(ki, qi), zv + p)),
            pl.BlockSpec((1, ATT_HEADS, t), lambda b, p, qi, ki: (b, 0, jnp.minimum(ki, qi))),
            pl.BlockSpec((1, ATT_HEADS, t), lambda b, p, qi, ki: (b, 0, qi)),
        ],
        out_specs=pl.BlockSpec((t, LANES), lambda b, p, qi, ki: (b * nq + qi, p)),
        out_shape=SDS((nb * seq, BRANCH_W), F32),
        scratch_shapes=[pltpu.VMEM((2, t, LANES), BF16), pltpu.VMEM((2, t, LANES), F32),
                        pltpu.VMEM((2, t, 2 * LANES), F32)],
        compiler_params=_params("arbitrary", "arbitrary", "arbitrary", "arbitrary"),
        name="fox_prompt",
    )(z, z, z, ct, ct)


def _page_suffix_kernel(lf_ref, suf_ref, tot_ref):
    r = lax.broadcasted_iota(jnp.int32, (PAGE_SIZE, PAGE_SIZE), 0)
    c = lax.broadcasted_iota(jnp.int32, (PAGE_SIZE, PAGE_SIZE), 1)
    lf = lf_ref[...]
    suf_ref[...] = jnp.dot(lf, jnp.where(r > c, 1.0, 0.0).astype(F32), precision=HI, preferred_element_type=F32)
    tot_ref[...] = jnp.dot(lf, jnp.ones((PAGE_SIZE, PAGE_SIZE), F32), precision=HI, preferred_element_type=F32)


def _page_suffix(lf_rows, tr):
    n = lf_rows.shape[0]
    spec = pl.BlockSpec((tr, PAGE_SIZE), lambda i: (i, 0))
    return pl.pallas_call(
        _page_suffix_kernel,
        grid=(n // tr,),
        in_specs=[spec],
        out_specs=[spec, spec],
        out_shape=[SDS((n, PAGE_SIZE), F32), SDS((n, PAGE_SIZE), F32)],
        compiler_params=_params("arbitrary"),
        name="page_suffix",
    )(lf_rows)


_PAGES_PER_STEP = 8


def _fox_sample_kernel(pt_ref, q_ref, kn_ref, vn_ref, lfn_ref, *rest, g, tt):
    del pt_ref
    kp_refs, vp_refs, sf_refs, tot_refs = rest[:g], rest[g:2 * g], rest[2 * g:3 * g], rest[3 * g:4 * g]
    o_ref, qb_ref, m_ref, l_ref, acc_ref, car_ref, cn_ref = rest[4 * g:]
    step = pl.program_id(1)
    rows = tt * ATT_HEADS
    w = ATT_HEADS * ATT_HD
    rid = lax.broadcasted_iota(jnp.int32, (rows, 1), 0)
    hmask = (lax.broadcasted_iota(jnp.int32, (rows, w), 1) >> _HD_SHIFT) == (rid & (ATT_HEADS - 1))
    tok = rid >> _HEAD_SHIFT
    lane = lax.broadcasted_iota(jnp.int32, (rows, PAGE_SIZE), 1)

    def tile_rows(x):
        return jnp.broadcast_to(x[None], (tt,) + x.shape).reshape(rows, x.shape[-1])

    def online(scores, values, dims):
        m_old = m_ref[...]
        m_new = m_old
        for s in scores:
            m_new = jnp.maximum(m_new, jnp.max(s, axis=1, keepdims=True))
        alpha = jnp.exp(m_old - m_new)
        l_new = alpha * l_ref[...]
        acc = alpha * acc_ref[...]
        for s, v in zip(scores, values):
            p = jnp.exp(s - m_new)
            l_new = l_new + jnp.sum(p, axis=1, keepdims=True)
            acc = acc + lax.dot_general(p.astype(BF16), v, dims, preferred_element_type=F32)
        l_ref[...] = l_new
        acc_ref[...] = acc
        m_ref[...] = m_new

    @pl.when(step == 0)
    def _():
        q = q_ref[...] * (ATT_HD ** -0.5)
        qrep = jnp.broadcast_to(q[:, None, :], (tt, ATT_HEADS, w)).reshape(rows, w)
        qb_ref[...] = jnp.where(hmask, qrep, 0.0).astype(BF16)
        m_ref[...] = jnp.full_like(m_ref, MASK_VALUE)
        l_ref[...] = jnp.zeros_like(l_ref)
        acc_ref[...] = jnp.zeros_like(acc_ref)
        car_ref[...] = jnp.zeros_like(car_ref)
        r2 = lax.broadcasted_iota(jnp.int32, (PAGE_SIZE, PAGE_SIZE), 0)
        c2 = lax.broadcasted_iota(jnp.int32, (PAGE_SIZE, PAGE_SIZE), 1)
        incl = jnp.where(r2 <= c2, 1.0, 0.0).astype(F32)
        cn_ref[...] = jnp.dot(tile_rows(lfn_ref[0]), incl, precision=HI, preferred_element_type=F32)

    qb = qb_ref[...]
    cn = cn_ref[...]
    cn_own = jnp.sum(jnp.where(lane == tok, cn, 0.0), axis=1, keepdims=True)
    car = car_ref[...]
    scores = []
    for i in range(g):
        kt = kp_refs[i][0].reshape(w, PAGE_SIZE).astype(BF16)
        s = jnp.dot(qb, kt, preferred_element_type=F32)
        scores.append(s + ((cn_own + car) + tile_rows(sf_refs[i][0])))
        car = car + tile_rows(tot_refs[i][0])[:, 0:1]
    car_ref[...] = car
    online(scores, [vp_refs[i][0].reshape(w, PAGE_SIZE).astype(BF16) for i in range(g)], _NT)

    @pl.when(step == pl.num_programs(1) - 1)
    def _():
        pad = jnp.zeros((PAGE_SIZE - tt, w), F32)
        kn = jnp.concatenate([kn_ref[...], pad], axis=0).astype(BF16)
        vn = jnp.concatenate([vn_ref[...], pad], axis=0).astype(BF16)
        sn = lax.dot_general(qb, kn, _NT, preferred_element_type=F32)
        sn = jnp.where(lane <= tok, sn + (cn_own - cn), MASK_VALUE)
        online([sn], [vn], (((1,), (0,)), ((), ())))
        o = jnp.where(hmask, acc_ref[...] / l_ref[...], 0.0)
        o_ref[...] = jnp.sum(o.reshape(tt, ATT_HEADS, w), axis=1)


def _fox_sample(z, lfn_t, ckt, cvt, suffix, total, page_table, page_base, nb, tt):
    npg = page_table.shape[1]
    g = _PAGES_PER_STEP
    assert npg % g == 0
    w = ATT_HEADS * ATT_HD
    rows = tt * ATT_HEADS

    def page(i, ndim):
        def index(b, p, pt):
            return (page_base + pt[b * npg + (npg - 1 - (p * g + i))],) + (0,) * (ndim - 1)
        return index

    in_specs = [pl.BlockSpec((tt, w), lambda b, p, pt, zb=zb: (b, zb)) for zb in (ZB_AQ, ZB_AK, ZB_AV)]
    in_specs += [pl.BlockSpec((1, ATT_HEADS, PAGE_SIZE), lambda b, p, pt: (b, 0, 0))]
    in_specs += [pl.BlockSpec((1, ATT_HEADS, ATT_HD, PAGE_SIZE), page(i, 4)) for i in range(g)] * 2
    in_specs += [pl.BlockSpec((1, ATT_HEADS, PAGE_SIZE), page(i, 3)) for i in range(g)] * 2
    grid_spec = pltpu.PrefetchScalarGridSpec(
        num_scalar_prefetch=1,
        grid=(nb, npg // g),
        in_specs=in_specs,
        out_specs=pl.BlockSpec((tt, w), lambda b, p, pt: (b, 0)),
        scratch_shapes=[pltpu.VMEM((rows, w), BF16), pltpu.VMEM((rows, 1), F32), pltpu.VMEM((rows, 1), F32),
                        pltpu.VMEM((rows, w), F32), pltpu.VMEM((rows, 1), F32), pltpu.VMEM((rows, PAGE_SIZE), F32)],
    )
    return pl.pallas_call(
        functools.partial(_fox_sample_kernel, g=g, tt=tt),
        grid_spec=grid_spec,
        out_shape=SDS((nb * tt, w), F32),
        compiler_params=_params("arbitrary", "arbitrary"),
        name="fox_sample",
    )(page_table.reshape(-1), z, z, z, lfn_t, *([ckt] * g), *([cvt] * g), *([suffix] * g), *([total] * g))


def _merge_kernel(x_ref, yc_ref, yh_ref, ya_ref, gt_ref, wb_ref, wo_ref, g2_ref, h_ref, xnt_ref):
    merged = jnp.zeros(x_ref.shape, F32)
    for b, y_ref in enumerate((yc_ref, yh_ref, ya_ref)):
        proj = jnp.dot(y_ref[...].astype(BF16), wb_ref[b], preferred_element_type=F32)
        merged = merged + _sigmoid(gt_ref[:, b * D_MODEL:(b + 1) * D_MODEL]) * proj
    h = x_ref[...] + jnp.dot(merged.astype(BF16), wo_ref[...], preferred_element_type=F32)
    h_ref[...] = h
    xnt_ref[...] = _rms(h, g2_ref[...]).T.astype(BF16)


def _merge(x, yc, yh, ya, z, wb, wo, g2, tm):
    n, d = x.shape
    br = pl.BlockSpec((tm, BRANCH_W), lambda i: (i, 0))
    return pl.pallas_call(
        _merge_kernel,
        grid=(n // tm,),
        in_specs=[
            pl.BlockSpec((tm, d), lambda i: (i, 0)), br, br, br,
            pl.BlockSpec((tm, 3 * d), lambda i: (i, 0)),
            pl.BlockSpec((3, BRANCH_W, d), lambda i: (0, 0, 0)),
            pl.BlockSpec((d, d), lambda i: (0, 0)),
            pl.BlockSpec((1, d), lambda i: (0, 0)),
        ],
        out_specs=[pl.BlockSpec((tm, d), lambda i: (i, 0)), pl.BlockSpec((d, tm), lambda i: (0, i))],
        out_shape=[SDS((n, d), F32), SDS((d, n), BF16)],
        compiler_params=_params("arbitrary"),
        name="merge",
    )(x, yc, yh, ya, z, wb, wo, g2)


def _take_max(work, row_iota, n_rows):
    m = jnp.max(work, axis=0, keepdims=True)
    first = jnp.min(jnp.where(work == m, row_iota, float(n_rows)), axis=0, keepdims=True)
    hit = row_iota == first
    return m, jnp.where(hit, -jnp.inf, work), hit


_N_CAND = -(-len(_PAIRS) // 8) * 8


def _peer_route_kernel(xnt_ref, wqt_ref, keys_ref, cut1_ref, e1_ref, rank2_ref, e2_ref, cand_ref, *, t):
    xnt = xnt_ref[...]
    key_iota = lax.broadcasted_iota(jnp.int32, (N_KEYS, t), 0).astype(F32)
    n_cand = _N_CAND
    cand_iota = lax.broadcasted_iota(jnp.int32, (n_cand, t), 0).astype(F32)
    cand_ref[len(_PAIRS):, :] = jnp.full((n_cand - len(_PAIRS), t), -jnp.inf, F32)

    def head(h, carry):
        dk = 2 * N_KEYS
        qt = jnp.dot(wqt_ref[pl.ds(pl.multiple_of(h * dk, dk), dk), :], xnt, preferred_element_type=F32)
        scores, tops = [], []
        for p in range(2):
            s = jnp.dot(keys_ref[h, p], qt[p * N_KEYS:(p + 1) * N_KEYS].astype(BF16), preferred_element_type=F32)
            scores.append(s)
            vals, work = [], s
            rank = jnp.full((N_KEYS, t), float(PEER_TOPK), F32)
            for i in range(PEER_TOPK):
                m, work, hit = _take_max(work, key_iota, N_KEYS)
                vals.append(m)
                if p == 1:
                    rank = jnp.where(hit, float(i), rank)
            tops.append(vals)
        for i, (a, b) in enumerate(_PAIRS):
            cand_ref[i:i + 1, :] = tops[0][a] + tops[1][b]
        work = cand_ref[...]
        best = []
        for _ in range(PEER_TOPK):
            m, work, _ = _take_max(work, cand_iota, n_cand)
            best.append(m)
        norm = jnp.zeros((1, t), F32)
        for m in best:
            norm = norm + jnp.exp(m - best[0])
        cut = jnp.zeros((N_KEYS, t), F32)
        for b in range(PEER_TOPK):
            cut = cut + jnp.where(scores[0] + tops[1][b] >= best[-1], 1.0, 0.0)
        cut1_ref[h] = cut
        e1_ref[h] = jnp.exp(scores[0] - tops[0][0]) / norm
        rank2_ref[h] = rank.astype(BF16)
        e2_ref[h] = jnp.exp(scores[1] - tops[1][0]).astype(BF16)
        return carry

    lax.fori_loop(0, PEER_HEADS, head, 0)


def _peer_route(xnt, wqt, keys, t):
    d, n = xnt.shape
    big = pl.BlockSpec((PEER_HEADS, N_KEYS, t), lambda i: (0, 0, i))
    wide, narrow = SDS((PEER_HEADS, N_KEYS, n), F32), SDS((PEER_HEADS, N_KEYS, n), BF16)
    return pl.pallas_call(
        functools.partial(_peer_route_kernel, t=t),
        grid=(n // t,),
        in_specs=[
            pl.BlockSpec((d, t), lambda i: (0, i)),
            pl.BlockSpec(wqt.shape, lambda i: (0, 0)),
            pl.BlockSpec(keys.shape, lambda i: (0, 0, 0, 0)),
        ],
        out_specs=[big, big, big, big],
        out_shape=[wide, wide, narrow, narrow],
        scratch_shapes=[pltpu.VMEM((_N_CAND, t), F32)],
        compiler_params=_params("arbitrary"),
        name="peer_route",
    )(xnt, wqt, keys)


_BF16_ROWS = 16
_PEER_CHUNK = 2048


def _peer_dense_kernel(xnt_ref, h_ref, u_ref, vt_ref, cut1_ref, e1_ref, rank2_ref, e2_ref, o_ref,
                       acc_ref, act_ref, cutx_ref, e1x_ref, rank2x_ref, e2x_ref, *, t, ec):
    c = pl.program_id(1)
    n_chunks = pl.num_programs(1) - 1

    @pl.when(c == 0)
    def _():
        acc_ref[...] = jnp.zeros_like(acc_ref)
        act_ref[1] = jnp.zeros((ec, t), BF16)
        rank2x_ref[...] = rank2_ref[...]
        e2x_ref[...] = e2_ref[...]

        def expand(i, carry):
            for h in range(PEER_HEADS):
                cutx_ref[h, i] = jnp.broadcast_to(cut1_ref[h, pl.ds(i, 1), :], (_BF16_ROWS, t)).astype(BF16)
                e1x_ref[h, i] = jnp.broadcast_to(e1_ref[h, pl.ds(i, 1), :], (_BF16_ROWS, t)).astype(BF16)
            return carry

        lax.fori_loop(0, N_KEYS, expand, 0)

    acc_ref[...] += jnp.dot(vt_ref[...], act_ref[(c + 1) & 1], preferred_element_type=F32)

    chunk = jnp.minimum(c, n_chunks - 1)
    n_i1 = ec // N_KEYS
    slot = c & 1
    rep = N_KEYS // _BF16_ROWS
    for ii in range(n_i1):
        i1 = chunk * n_i1 + ii
        rows = slice(ii * N_KEYS, (ii + 1) * N_KEYS)
        hid = jnp.dot(u_ref[rows, :], xnt_ref[...], preferred_element_type=F32)
        wsum = jnp.zeros((N_KEYS, t), BF16)
        for h in range(PEER_HEADS):
            cut = jnp.concatenate([cutx_ref[h, i1]] * rep, axis=0)
            e1 = jnp.concatenate([e1x_ref[h, i1]] * rep, axis=0)
            wsum = wsum + jnp.where(rank2x_ref[h] < cut, e1 * e2x_ref[h], jnp.zeros((), BF16))
        gelu = 0.5 * hid * (1.0 + lax.erf(hid * (2.0 ** -0.5)))
        act_ref[slot, rows, :] = gelu.astype(BF16) * wsum

    @pl.when(c == n_chunks)
    def _():
        o_ref[...] = h_ref[...] + acc_ref[...].T


def _peer_dense(xnt, h, u, vt, route, t, ec):
    d, n = xnt.shape
    nc = u.shape[0] // ec
    big = pl.BlockSpec((PEER_HEADS, N_KEYS, t), lambda i, c: (0, 0, i))
    return pl.pallas_call(
        functools.partial(_peer_dense_kernel, t=t, ec=ec),
        grid=(n // t, nc + 1),
        in_specs=[
            pl.BlockSpec((d, t), lambda i, c: (0, i)),
            pl.BlockSpec((t, d), lambda i, c: (i, 0)),
            pl.BlockSpec((ec, d), lambda i, c: (jnp.minimum(c, nc - 1), 0)),
            pl.BlockSpec((d, ec), lambda i, c: (0, jnp.maximum(c - 1, 0))),
            big, big, big, big,
        ],
        out_specs=pl.BlockSpec((t, d), lambda i, c: (i, 0)),
        out_shape=SDS((n, d), F32),
        scratch_shapes=[pltpu.VMEM((d, t), F32), pltpu.VMEM((2, ec, t), BF16),
                        pltpu.VMEM((PEER_HEADS, N_KEYS, _BF16_ROWS, t), BF16),
                        pltpu.VMEM((PEER_HEADS, N_KEYS, _BF16_ROWS, t), BF16),
                        pltpu.VMEM((PEER_HEADS, N_KEYS, t), BF16), pltpu.VMEM((PEER_HEADS, N_KEYS, t), BF16)],
        compiler_params=_params("arbitrary", "arbitrary", vmem_mb=56),
        name="peer_dense",
    )(xnt, h, u, vt, *route)


def _final_norm_kernel(x_ref, g_ref, o_ref):
    o_ref[...] = _rms(x_ref[...], g_ref[...])


def _final_norm(x, g, tm):
    n, d = x.shape
    return pl.pallas_call(
        _final_norm_kernel,
        grid=(n // tm,),
        in_specs=[pl.BlockSpec((tm, d), lambda i: (i, 0)), pl.BlockSpec((1, d), lambda i: (0, 0))],
        out_specs=pl.BlockSpec((tm, d), lambda i: (i, 0)),
        out_shape=SDS((n, d), F32),
        compiler_params=_params("arbitrary"),
        name="final_norm",
    )(x, g)


def _tile(n, pref):
    return pref if n % pref == 0 else n


def _layer_weights(l, lbs, norm1_g, w_in, conv_dw, conv_db, conv_ln_g, conv_ln_b, hg_norm_g, att_fb,
                   w_branch, w_out, norm2_g, peer_wq, peer_keys, peer_u, peer_v):
    n_main = 2 * BRANCH_W + 4 * HG_HEADS * HG_DK + 3 * ATT_HEADS * ATT_HD
    w = w_in[l]
    row = lambda a: a.reshape(1, -1).astype(F32)
    return dict(
        g1=row(norm1_g[l]),
        w_main=jnp.concatenate([w[:, n_main + ATT_HEADS:], w[:, :n_main]], axis=1).astype(BF16),
        w_f=jnp.pad(w[:, n_main:n_main + ATT_HEADS], ((0, 0), (0, LANES - ATT_HEADS))).astype(BF16),
        fb=jnp.pad(att_fb[l].astype(F32), (0, LANES - ATT_HEADS)).reshape(1, LANES),
        conv_w=conv_dw[l].astype(F32), conv_b=row(conv_db[l]), ln_g=row(conv_ln_g[l]), ln_b=row(conv_ln_b[l]),
        lb=row(lbs[l]), hg_g=row(hg_norm_g[l]),
        wb=w_branch[l].astype(BF16), wo=w_out[l].astype(BF16), g2=row(norm2_g[l]),
        wqt=peer_wq[l].T.astype(BF16), keys=peer_keys[l].astype(BF16),
        u=peer_u[l].astype(BF16), vt=peer_v[l].T.astype(BF16),
    )


def _token_mixers(x, wt, nb, seq, conv_buf, hg_state, attend):
    n = nb * seq
    z, lf = _in_proj(x, wt["g1"], wt["w_main"], wt["w_f"], wt["fb"], _tile(n, 1024))
    y_conv, conv_new = _conv_branch(z, conv_buf, wt["conv_w"], wt["conv_b"], wt["ln_g"], wt["ln_b"],
                                    nb, seq, _tile(seq, 512))
    y_hgrn, hg_new = _hgrn_branch(z, hg_state, wt["lb"], wt["hg_g"], nb, seq, _tile(seq, 128), _tile(seq, 16))
    y_att = attend(z, lf)
    h, xn = _merge(x, y_conv, y_hgrn, y_att, z, wt["wb"], wt["wo"], wt["g2"], _tile(n, 512))
    tp = _tile(n, 256)
    route = _peer_route(xn, wt["wqt"], wt["keys"], tp)
    out = _peer_dense(xn, h, wt["u"], wt["vt"], route, tp, _PEER_CHUNK)
    w = ATT_HEADS * ATT_HD
    k_rows = z[:, ZB_AK * BRANCH_W:ZB_AK * BRANCH_W + w].reshape(nb, seq, ATT_HEADS, ATT_HD)
    v_rows = z[:, ZB_AV * BRANCH_W:ZB_AV * BRANCH_W + w].reshape(nb, seq, ATT_HEADS, ATT_HD)
    logf = lf[:, :ATT_HEADS].reshape(nb, seq, ATT_HEADS)
    return out, conv_new, hg_new, k_rows, v_rows, logf


def kernel(x_prompt, x_sample, cache_k, cache_v, cache_logf, state_conv, state_hgrn, page_table, norm1_g, w_in, conv_dw, conv_db, conv_ln_g, conv_ln_b, hg_lb_logits, hg_norm_g, att_fb, w_branch, w_out, norm2_g, peer_wq, peer_keys, peer_u, peer_v, final_g):
    depth = w_in.shape[0]
    bp, sp, d = x_prompt.shape
    bs, ss, _ = x_sample.shape
    w = ATT_HEADS * ATT_HD
    probs = jax.nn.softmax(hg_lb_logits.astype(F32), axis=0)
    lbs = jnp.cumsum(probs, axis=0) - probs[0:1]
    n_phys = cache_k.shape[1]
    ckt = jnp.transpose(cache_k, (0, 1, 3, 4, 2)).reshape(depth * n_phys, ATT_HEADS, ATT_HD, PAGE_SIZE)
    cvt = jnp.transpose(cache_v, (0, 1, 3, 4, 2)).reshape(depth * n_phys, ATT_HEADS, ATT_HD, PAGE_SIZE)
    lf_rows = jnp.swapaxes(cache_logf.astype(F32), 2, 3).reshape(depth * n_phys * ATT_HEADS, PAGE_SIZE)
    suffix, total = (a.reshape(depth * n_phys, ATT_HEADS, PAGE_SIZE)
                     for a in _page_suffix(lf_rows, _tile(lf_rows.shape[0], 2048)))

    hp = x_prompt.reshape(bp * sp, d)
    hs = x_sample.reshape(bs * ss, d)
    outs_p, outs_s = [], []
    for l in range(depth):
        wt = _layer_weights(l, lbs, norm1_g, w_in, conv_dw, conv_db, conv_ln_g, conv_ln_b, hg_norm_g, att_fb,
                            w_branch, w_out, norm2_g, peer_wq, peer_keys, peer_u, peer_v)

        def attend_prompt(z, lf):
            ct = _logf_cumsum(lf, bp, sp, _tile(sp, 512))
            return _fox_prompt(z, ct, bp, sp, _tile(sp, _FOX_BLOCK))

        def attend_sample(z, lf, l=l):
            lfn_t = jnp.swapaxes(lf[:, :ATT_HEADS].reshape(bs, ss, ATT_HEADS), 1, 2)
            lfn_t = jnp.pad(lfn_t, ((0, 0), (0, 0), (0, PAGE_SIZE - ss)))
            return _fox_sample(z, lfn_t, ckt, cvt, suffix, total, page_table, l * n_phys, bs, ss)

        hp, *rp = _token_mixers(hp, wt, bp, sp, jnp.zeros((bp, CONV_K - 1, BRANCH_W), F32),
                                jnp.zeros((bp, HG_HEADS, HG_DK, HG_DK), F32), attend_prompt)
        hs, *rs = _token_mixers(hs, wt, bs, ss, state_conv[l], state_hgrn[l], attend_sample)
        outs_p.append(rp)
        outs_s.append(rs)

    fg = final_g.reshape(1, d).astype(F32)
    y_prompt = _final_norm(hp, fg, _tile(bp * sp, 1024)).reshape(bp, sp, d)
    y_sample = _final_norm(hs, fg, _tile(bs * ss, 1024)).reshape(bs, ss, d)
    stack = lambda outs, i: jnp.stack([r[i] for r in outs])
    return (y_prompt, y_sample,
            stack(outs_p, 2), stack(outs_p, 3), stack(outs_p, 4), stack(outs_p, 0), stack(outs_p, 1),
            stack(outs_s, 2), stack(outs_s, 3), stack(outs_s, 4), stack(outs_s, 0), stack(outs_s, 1))
```

```python
import functools

import jax
import jax.numpy as jnp
from jax import lax
from jax.experimental import pallas as pl
from jax.experimental.pallas import tpu as pltpu

F32 = jnp.float32
BF16 = jnp.bfloat16
HI = lax.Precision.HIGHEST
SDS = jax.ShapeDtypeStruct

D_MODEL = 1024
BRANCH_W = 512
CONV_K = 31
HG_HEADS = 4
HG_DK = 128
ATT_HEADS = 8
ATT_HD = 64
PEER_HEADS = 8
N_KEYS = 128
PEER_TOPK = 16
PAGE_SIZE = 128
EPS = 1e-6
GATE_FLOOR = 1e-20
MASK_VALUE = -1e30
LANES = 128
_HD_SHIFT = ATT_HD.bit_length() - 1
_HEAD_SHIFT = ATT_HEADS.bit_length() - 1

Z_COLS = 15 * BRANCH_W
ZB_CONV = 3
ZB_HQ, ZB_HF, ZB_HI, ZB_HG, ZB_AQ, ZB_AK, ZB_AV = 8, 9, 10, 11, 12, 13, 14

_PAIRS = tuple((a, b) for a in range(PEER_TOPK) for b in range(PEER_TOPK) if (a + 1) * (b + 1) <= PEER_TOPK)

_NT = (((1,), (1,)), ((), ()))
_TN = (((0,), (0,)), ((), ()))


def _params(*sem, vmem_mb=48):
    return pltpu.CompilerParams(dimension_semantics=sem, vmem_limit_bytes=vmem_mb * 1024 * 1024)


def _sigmoid(x):
    return 1.0 / (1.0 + jnp.exp(-x))


def _log_sigmoid(x):
    return jnp.minimum(x, 0.0) - jnp.log1p(jnp.exp(-jnp.abs(x)))


def _rms(x, g):
    return x * lax.rsqrt(jnp.mean(x * x, axis=-1, keepdims=True) + EPS) * g


def _in_proj_kernel(x_ref, g_ref, w_ref, wf_ref, fb_ref, z_ref, lf_ref, xn_ref):
    @pl.when(pl.program_id(1) == 0)
    def _():
        xn = _rms(x_ref[...], g_ref[...]).astype(BF16)
        xn_ref[...] = xn
        af = jnp.dot(xn, wf_ref[...], preferred_element_type=F32) + fb_ref[...]
        lf_ref[...] = _log_sigmoid(af)

    z_ref[...] = jnp.dot(xn_ref[...], w_ref[...], preferred_element_type=F32)


def _in_proj(x, g, w_main, w_f, fb, tm):
    n, d = x.shape
    tn = 3 * BRANCH_W
    return pl.pallas_call(
        _in_proj_kernel,
        grid=(n // tm, Z_COLS // tn),
        in_specs=[
            pl.BlockSpec((tm, d), lambda i, j: (i, 0)),
            pl.BlockSpec((1, d), lambda i, j: (0, 0)),
            pl.BlockSpec((d, tn), lambda i, j: (0, j)),
            pl.BlockSpec((d, LANES), lambda i, j: (0, 0)),
            pl.BlockSpec((1, LANES), lambda i, j: (0, 0)),
        ],
        out_specs=[
            pl.BlockSpec((tm, tn), lambda i, j: (i, j)),
            pl.BlockSpec((tm, LANES), lambda i, j: (i, 0)),
        ],
        out_shape=[SDS((n, Z_COLS), F32), SDS((n, LANES), F32)],
        scratch_shapes=[pltpu.VMEM((tm, d), BF16)],
        compiler_params=_params("arbitrary", "arbitrary"),
        name="in_proj",
    )(x, g, w_main, w_f, fb)


_CONV_PAD = 32


def _conv_kernel(a_ref, buf_ref, w_ref, b_ref, g_ref, bt_ref, y_ref, nc_ref, ext_ref, *, tl):
    hist = CONV_K - 1
    lo = _CONV_PAD - hist

    @pl.when(pl.program_id(1) == 0)
    def _():
        ext_ref[lo:_CONV_PAD, :] = buf_ref[0]

    a = a_ref[...]
    ext_ref[_CONV_PAD:_CONV_PAD + tl, :] = a[:, :BRANCH_W] * _sigmoid(a[:, BRANCH_W:])
    acc = jnp.zeros((tl, BRANCH_W), F32)
    for j in range(CONV_K):
        acc = acc + w_ref[j:j + 1, :] * ext_ref[lo + j:lo + j + tl, :]
    y = acc + b_ref[...]
    mu = jnp.mean(y, axis=-1, keepdims=True)
    yc = y - mu
    var = jnp.mean(yc * yc, axis=-1, keepdims=True)
    y = yc * lax.rsqrt(var + EPS) * g_ref[...] + bt_ref[...]
    y_ref[...] = y * _sigmoid(y)
    tail = ext_ref[tl + lo:tl + _CONV_PAD, :]
    nc_ref[0] = tail
    ext_ref[lo:_CONV_PAD, :] = tail


def _conv_branch(z, buf, w, b, g, bt, nb, seq, tl):
    nt = seq // tl
    hist = CONV_K - 1
    vec = pl.BlockSpec((1, BRANCH_W), lambda i, t: (0, 0))
    return pl.pallas_call(
        functools.partial(_conv_kernel, tl=tl),
        grid=(nb, nt),
        in_specs=[
            pl.BlockSpec((tl, 2 * BRANCH_W), lambda i, t: (i * nt + t, ZB_CONV)),
            pl.BlockSpec((1, hist, BRANCH_W), lambda i, t: (i, 0, 0)),
            pl.BlockSpec((CONV_K, BRANCH_W), lambda i, t: (0, 0)),
            vec, vec, vec,
        ],
        out_specs=[
            pl.BlockSpec((tl, BRANCH_W), lambda i, t: (i * nt + t, 0)),
            pl.BlockSpec((1, hist, BRANCH_W), lambda i, t: (i, 0, 0)),
        ],
        out_shape=[SDS((nb * seq, BRANCH_W), F32), SDS((nb, hist, BRANCH_W), F32)],
        scratch_shapes=[pltpu.VMEM((_CONV_PAD + tl, BRANCH_W), F32)],
        compiler_params=_params("arbitrary", "arbitrary"),
        name="conv_branch",
    )(z, buf, w, b, g, bt)


def _hgrn_kernel(q_ref, f_ref, i_ref, g_ref, s0_ref, lb_ref, ng_ref, y_ref, sn_ref,
                 st_ref, qs_ref, ks_ref, bs_ref, *, tl, c):
    t = pl.program_id(1)

    @pl.when(t == 0)
    def _():
        for h in range(HG_HEADS):
            st_ref[h] = s0_ref[0, h].T

    lb = jnp.clip(lb_ref[...], 0.0, 1.0)
    f = lb + (1.0 - lb) * _sigmoid(f_ref[...])
    lf = jnp.log(jnp.maximum(f, GATE_FLOOR))
    q = q_ref[...]
    qs_ref[...] = q * _sigmoid(q)
    ks_ref[...] = 1.0 - f
    r = lax.broadcasted_iota(jnp.int32, (tl, tl), 0)
    cc = lax.broadcasted_iota(jnp.int32, (tl, tl), 1)
    tri = jnp.where(cc <= r, jnp.where(cc >= (r & -c), 1.0, 0.0), 0.0).astype(F32)
    bs_ref[...] = jnp.dot(tri, lf, precision=HI, preferred_element_type=F32)
    rowid = lax.broadcasted_iota(jnp.int32, (c, 1), 0)

    def chunk(ci, carry):
        off = pl.multiple_of(ci * c, c)
        for h in range(HG_HEADS):
            sl = slice(h * HG_DK, (h + 1) * HG_DK)
            qc = qs_ref[pl.ds(off, c), sl]
            kc = ks_ref[pl.ds(off, c), sl]
            bc = bs_ref[pl.ds(off, c), sl]
            vc = i_ref[pl.ds(off, c), sl]
            o = jnp.zeros((c, HG_DK), F32)
            for s in range(c):
                k_row = kc[s:s + 1, :]
                b_row = bc[s:s + 1, :]
                v_row = vc[s:s + 1, :]
                p = qc * k_row * jnp.exp(jnp.minimum(bc - b_row, 0.0))
                a = jnp.sum(p, axis=1, keepdims=True)
                o = o + jnp.where(rowid >= s, a, 0.0) * v_row
            st = st_ref[h]
            qt = (qc * jnp.exp(bc)).astype(BF16)
            o = o + lax.dot_general(qt, st.astype(BF16), _NT, preferred_element_type=F32)
            b_last = bc[c - 1:c, :]
            kt = (kc * jnp.exp(b_last - bc)).astype(BF16)
            upd = lax.dot_general(vc.astype(BF16), kt, _TN, preferred_element_type=F32)
            st_ref[h] = st * jnp.exp(b_last) + upd
            o = _rms(o, ng_ref[...])
            y_ref[pl.ds(off, c), sl] = o * _sigmoid(g_ref[pl.ds(off, c), sl])
        return carry

    lax.fori_loop(0, tl // c, chunk, 0)

    @pl.when(t == pl.num_programs(1) - 1)
    def _():
        for h in range(HG_HEADS):
            sn_ref[0, h] = st_ref[h].T


def _hgrn_branch(z, s0, lb, ng, nb, seq, tl, c):
    nt = seq // tl
    w = HG_HEADS * HG_DK

    def col(j):
        return pl.BlockSpec((tl, w), lambda i, t: (i * nt + t, j))

    st_spec = pl.BlockSpec((1, HG_HEADS, HG_DK, HG_DK), lambda i, t: (i, 0, 0, 0))
    return pl.pallas_call(
        functools.partial(_hgrn_kernel, tl=tl, c=c),
        grid=(nb, nt),
        in_specs=[col(ZB_HQ), col(ZB_HF), col(ZB_HI), col(ZB_HG), st_spec,
                  pl.BlockSpec((1, w), lambda i, t: (0, 0)),
                  pl.BlockSpec((1, HG_DK), lambda i, t: (0, 0))],
        out_specs=[pl.BlockSpec((tl, w), lambda i, t: (i * nt + t, 0)), st_spec],
        out_shape=[SDS((nb * seq, w), F32), SDS((nb, HG_HEADS, HG_DK, HG_DK), F32)],
        scratch_shapes=[pltpu.VMEM((HG_HEADS, HG_DK, HG_DK), F32),
                        pltpu.VMEM((tl, w), F32), pltpu.VMEM((tl, w), F32), pltpu.VMEM((tl, w), F32)],
        compiler_params=_params("arbitrary", "arbitrary"),
        name="hgrn_branch",
    )(z, z, z, z, s0, lb, ng)


def _cumsum_kernel(lf_ref, ct_ref, carry_ref, *, tl):
    @pl.when(pl.program_id(1) == 0)
    def _():
        carry_ref[...] = jnp.zeros_like(carry_ref)

    r = lax.broadcasted_iota(jnp.int32, (tl, tl), 0)
    cc = lax.broadcasted_iota(jnp.int32, (tl, tl), 1)
    tri = jnp.where(cc <= r, 1.0, 0.0).astype(F32)
    cs = jnp.dot(tri, lf_ref[...], precision=HI, preferred_element_type=F32) + carry_ref[...]
    carry_ref[...] = cs[tl - 1:tl, :]
    ct_ref[0] = cs.T[:ATT_HEADS, :]


def _logf_cumsum(lf, nb, seq, tl):
    nt = seq // tl
    return pl.pallas_call(
        functools.partial(_cumsum_kernel, tl=tl),
        grid=(nb, nt),
        in_specs=[pl.BlockSpec((tl, LANES), lambda i, t: (i * nt + t, 0))],
        out_specs=pl.BlockSpec((1, ATT_HEADS, tl), lambda i, t: (i, 0, t)),
        out_shape=SDS((nb, ATT_HEADS, seq), F32),
        scratch_shapes=[pltpu.VMEM((1, LANES), F32)],
        compiler_params=_params("arbitrary", "arbitrary"),
        name="logf_cumsum",
    )(lf)


_LOG2E = 1.4426950408889634
_FOX_BLOCK = 1024
_FOX_DIAG_SPLIT = 2


def _fox_kernel(q_ref, k_ref, v_ref, ck_ref, cq_ref, o_ref, qm_ref, m_ref, acc_ref, *, t):
    pair = pl.program_id(1)
    qi = pl.program_id(2)
    ki = pl.program_id(3)
    lane = lax.broadcasted_iota(jnp.int32, (1, LANES), 1)

    @pl.when(ki == 0)
    def _():
        q = q_ref[...] * (ATT_HD ** -0.5 * _LOG2E)
        qm_ref[0] = jnp.where(lane < ATT_HD, q, 0.0).astype(BF16)
        qm_ref[1] = jnp.where(lane >= ATT_HD, q, 0.0).astype(BF16)
        m_ref[...] = jnp.full_like(m_ref, MASK_VALUE)
        acc_ref[...] = jnp.zeros_like(acc_ref)

    def step(diagonal):
        k = k_ref[...].astype(BF16)
        v = jnp.concatenate([v_ref[...].astype(BF16), jnp.ones((t, LANES), BF16)], axis=1)
        head_iota = lax.broadcasted_iota(jnp.int32, (ATT_HEADS, 1), 0)
        for j in range(2):
            own = head_iota == 2 * pair + j
            ck = jnp.sum(jnp.where(own, ck_ref[0], 0.0), axis=0, keepdims=True)
            c0 = jnp.sum(jnp.where(own, cq_ref[0, :, 0:1], 0.0), axis=0, keepdims=True)
            bias = (c0 - ck) * _LOG2E
            qm, m_all, acc_all = qm_ref[j], m_ref[j], acc_ref[j]
            m_out, acc_out = [], []
            sub = t // _FOX_DIAG_SPLIT if diagonal else t
            for r0 in range(0, t, sub):
                nk = r0 + sub if diagonal else t
                rows = slice(r0, r0 + sub)
                s = lax.dot_general(qm[rows], k[:nk], _NT, preferred_element_type=F32) + bias[:, :nk]
                if diagonal:
                    rid = lax.broadcasted_iota(jnp.int32, (sub, nk), 0) + r0
                    cid = lax.broadcasted_iota(jnp.int32, (sub, nk), 1)
                    s = jnp.where(cid <= rid, s, MASK_VALUE)
                m_old = m_all[rows]
                m_new = jnp.maximum(m_old, jnp.max(s, axis=1, keepdims=True))
                alpha = jnp.exp2(m_old - m_new)
                p = jnp.exp2(s - m_new[:, 0:1]).astype(BF16)
                pv = jnp.dot(p, v[:nk], preferred_element_type=F32)
                acc_out.append(jnp.concatenate([alpha, alpha], axis=1) * acc_all[rows] + pv)
                m_out.append(m_new)
            acc_ref[j] = jnp.concatenate(acc_out, axis=0)
            m_ref[j] = jnp.concatenate(m_out, axis=0)

    @pl.when(ki < qi)
    def _():
        step(False)

    @pl.when(ki == qi)
    def _():
        step(True)
        o0 = acc_ref[0, :, :LANES] / acc_ref[0, :, LANES:]
        o1 = acc_ref[1, :, :LANES] / acc_ref[1, :, LANES:]
        o_ref[...] = jnp.where(lane < ATT_HD, o0, o1)


def _fox_prompt(z, ct, nb, seq, t):
    nq = seq // t
    npair = ATT_HEADS // 2
    zq, zk, zv = (zb * (BRANCH_W // LANES) for zb in (ZB_AQ, ZB_AK, ZB_AV))
    return pl.pallas_call(
        functools.partial(_fox_kernel, t=t),
        grid=(nb, npair, nq, nq),
        in_specs=[
            pl.BlockSpec((t, LANES), lambda b, p, qi, ki: (b * nq + qi, zq + p)),
            pl.BlockSpec((t, LANES), lambda b, p, qi, ki: (b * nq + jnp.minimum(ki, qi), zk + p)),
            pl.BlockSpec((t, LANES), lambda b, p, qi, ki: (b * nq + jnp.minimum(ki, qi), zv + p)),
            pl.BlockSpec((1, ATT_HEADS, t), lambda b, p, qi, ki: (b, 0, jnp.minimum(ki, qi))),
            pl.BlockSpec((1, ATT_HEADS, t), lambda b, p, qi, ki: (b, 0, qi)),
        ],
        out_specs=pl.BlockSpec((t, LANES), lambda b, p, qi, ki: (b * nq + qi, p)),
        out_shape=SDS((nb * seq, BRANCH_W), F32),
        scratch_shapes=[pltpu.VMEM((2, t, LANES), BF16), pltpu.VMEM((2, t, LANES), F32),
                        pltpu.VMEM((2, t, 2 * LANES), F32)],
        compiler_params=_params("arbitrary", "arbitrary", "arbitrary", "arbitrary"),
        name="fox_prompt",
    )(z, z, z, ct, ct)


def _page_suffix_kernel(lf_ref, suf_ref, tot_ref):
    r = lax.broadcasted_iota(jnp.int32, (PAGE_SIZE, PAGE_SIZE), 0)
    c = lax.broadcasted_iota(jnp.int32, (PAGE_SIZE, PAGE_SIZE), 1)
    lf = lf_ref[...]
    suf_ref[...] = jnp.dot(lf, jnp.where(r > c, 1.0, 0.0).astype(F32), precision=HI, preferred_element_type=F32)
    tot_ref[...] = jnp.dot(lf, jnp.ones((PAGE_SIZE, PAGE_SIZE), F32), precision=HI, preferred_element_type=F32)


def _page_suffix(lf_rows, tr):
    n = lf_rows.shape[0]
    spec = pl.BlockSpec((tr, PAGE_SIZE), lambda i: (i, 0))
    return pl.pallas_call(
        _page_suffix_kernel,
        grid=(n // tr,),
        in_specs=[spec],
        out_specs=[spec, spec],
        out_shape=[SDS((n, PAGE_SIZE), F32), SDS((n, PAGE_SIZE), F32)],
        compiler_params=_params("arbitrary"),
        name="page_suffix",
    )(lf_rows)


_PAGES_PER_STEP = 8


def _fox_sample_kernel(pt_ref, q_ref, kn_ref, vn_ref, lfn_ref, *rest, g, tt):
    del pt_ref
    kp_refs, vp_refs, sf_refs, tot_refs = rest[:g], rest[g:2 * g], rest[2 * g:3 * g], rest[3 * g:4 * g]
    o_ref, qb_ref, m_ref, l_ref, acc_ref, car_ref, cn_ref = rest[4 * g:]
    step = pl.program_id(1)
    rows = tt * ATT_HEADS
    w = ATT_HEADS * ATT_HD
    rid = lax.broadcasted_iota(jnp.int32, (rows, 1), 0)
    hmask = (lax.broadcasted_iota(jnp.int32, (rows, w), 1) >> _HD_SHIFT) == (rid & (ATT_HEADS - 1))
    tok = rid >> _HEAD_SHIFT
    lane = lax.broadcasted_iota(jnp.int32, (rows, PAGE_SIZE), 1)

    def tile_rows(x):
        return jnp.broadcast_to(x[None], (tt,) + x.shape).reshape(rows, x.shape[-1])

    def online(scores, values, dims):
        m_old = m_ref[...]
        m_new = m_old
        for s in scores:
            m_new = jnp.maximum(m_new, jnp.max(s, axis=1, keepdims=True))
        alpha = jnp.exp(m_old - m_new)
        l_new = alpha * l_ref[...]
        acc = alpha * acc_ref[...]
        for s, v in zip(scores, values):
            p = jnp.exp(s - m_new)
            l_new = l_new + jnp.sum(p, axis=1, keepdims=True)
            acc = acc + lax.dot_general(p.astype(BF16), v, dims, preferred_element_type=F32)
        l_ref[...] = l_new
        acc_ref[...] = acc
        m_ref[...] = m_new

    @pl.when(step == 0)
    def _():
        q = q_ref[...] * (ATT_HD ** -0.5)
        qrep = jnp.broadcast_to(q[:, None, :], (tt, ATT_HEADS, w)).reshape(rows, w)
        qb_ref[...] = jnp.where(hmask, qrep, 0.0).astype(BF16)
        m_ref[...] = jnp.full_like(m_ref, MASK_VALUE)
        l_ref[...] = jnp.zeros_like(l_ref)
        acc_ref[...] = jnp.zeros_like(acc_ref)
        car_ref[...] = jnp.zeros_like(car_ref)
        r2 = lax.broadcasted_iota(jnp.int32, (PAGE_SIZE, PAGE_SIZE), 0)
        c2 = lax.broadcasted_iota(jnp.int32, (PAGE_SIZE, PAGE_SIZE), 1)
        incl = jnp.where(r2 <= c2, 1.0, 0.0).astype(F32)
        cn_ref[...] = jnp.dot(tile_rows(lfn_ref[0]), incl, precision=HI, preferred_element_type=F32)

    qb = qb_ref[...]
    cn = cn_ref[...]
    cn_own = jnp.sum(jnp.where(lane == tok, cn, 0.0), axis=1, keepdims=True)
    car = car_ref[...]
    scores = []
    for i in range(g):
        kt = kp_refs[i][0].reshape(w, PAGE_SIZE).astype(BF16)
        s = jnp.dot(qb, kt, preferred_element_type=F32)
        scores.append(s + ((cn_own + car) + tile_rows(sf_refs[i][0])))
        car = car + tile_rows(tot_refs[i][0])[:, 0:1]
    car_ref[...] = car
    online(scores, [vp_refs[i][0].reshape(w, PAGE_SIZE).astype(BF16) for i in range(g)], _NT)

    @pl.when(step == pl.num_programs(1) - 1)
    def _():
        pad = jnp.zeros((PAGE_SIZE - tt, w), F32)
        kn = jnp.concatenate([kn_ref[...], pad], axis=0).astype(BF16)
        vn = jnp.concatenate([vn_ref[...], pad], axis=0).astype(BF16)
        sn = lax.dot_general(qb, kn, _NT, preferred_element_type=F32)
        sn = jnp.where(lane <= tok, sn + (cn_own - cn), MASK_VALUE)
        online([sn], [vn], (((1,), (0,)), ((), ())))
        o = jnp.where(hmask, acc_ref[...] / l_ref[...], 0.0)
        o_ref[...] = jnp.sum(o.reshape(tt, ATT_HEADS, w), axis=1)


def _fox_sample(z, lfn_t, ckt, cvt, suffix, total, page_table, page_base, nb, tt):
    npg = page_table.shape[1]
    g = _PAGES_PER_STEP
    assert npg % g == 0
    w = ATT_HEADS * ATT_HD
    rows = tt * ATT_HEADS

    def page(i, ndim):
        def index(b, p, pt):
            return (page_base + pt[b * npg + (npg - 1 - (p * g + i))],) + (0,) * (ndim - 1)
        return index

    in_specs = [pl.BlockSpec((tt, w), lambda b, p, pt, zb=zb: (b, zb)) for zb in (ZB_AQ, ZB_AK, ZB_AV)]
    in_specs += [pl.BlockSpec((1, ATT_HEADS, PAGE_SIZE), lambda b, p, pt: (b, 0, 0))]
    in_specs += [pl.BlockSpec((1, ATT_HEADS, ATT_HD, PAGE_SIZE), page(i, 4)) for i in range(g)] * 2
    in_specs += [pl.BlockSpec((1, ATT_HEADS, PAGE_SIZE), page(i, 3)) for i in range(g)] * 2
    grid_spec = pltpu.PrefetchScalarGridSpec(
        num_scalar_prefetch=1,
        grid=(nb, npg // g),
        in_specs=in_specs,
        out_specs=pl.BlockSpec((tt, w), lambda b, p, pt: (b, 0)),
        scratch_shapes=[pltpu.VMEM((rows, w), BF16), pltpu.VMEM((rows, 1), F32), pltpu.VMEM((rows, 1), F32),
                        pltpu.VMEM((rows, w), F32), pltpu.VMEM((rows, 1), F32), pltpu.VMEM((rows, PAGE_SIZE), F32)],
    )
    return pl.pallas_call(
        functools.partial(_fox_sample_kernel, g=g, tt=tt),
        grid_spec=grid_spec,
        out_shape=SDS((nb * tt, w), F32),
        compiler_params=_params("arbitrary", "arbitrary"),
        name="fox_sample",
    )(page_table.reshape(-1), z, z, z, lfn_t, *([ckt] * g), *([cvt] * g), *([suffix] * g), *([total] * g))


def _merge_kernel(x_ref, yc_ref, yh_ref, ya_ref, gt_ref, wb_ref, wo_ref, g2_ref, h_ref, xnt_ref):
    merged = jnp.zeros(x_ref.shape, F32)
    for b, y_ref in enumerate((yc_ref, yh_ref, ya_ref)):
        proj = jnp.dot(y_ref[...].astype(BF16), wb_ref[b], preferred_element_type=F32)
        merged = merged + _sigmoid(gt_ref[:, b * D_MODEL:(b + 1) * D_MODEL]) * proj
    h = x_ref[...] + jnp.dot(merged.astype(BF16), wo_ref[...], preferred_element_type=F32)
    h_ref[...] = h
    xnt_ref[...] = _rms(h, g2_ref[...]).T.astype(BF16)


def _merge(x, yc, yh, ya, z, wb, wo, g2, tm):
    n, d = x.shape
    br = pl.BlockSpec((tm, BRANCH_W), lambda i: (i, 0))
    return pl.pallas_call(
        _merge_kernel,
        grid=(n // tm,),
        in_specs=[
            pl.BlockSpec((tm, d), lambda i: (i, 0)), br, br, br,
            pl.BlockSpec((tm, 3 * d), lambda i: (i, 0)),
            pl.BlockSpec((3, BRANCH_W, d), lambda i: (0, 0, 0)),
            pl.BlockSpec((d, d), lambda i: (0, 0)),
            pl.BlockSpec((1, d), lambda i: (0, 0)),
        ],
        out_specs=[pl.BlockSpec((tm, d), lambda i: (i, 0)), pl.BlockSpec((d, tm), lambda i: (0, i))],
        out_shape=[SDS((n, d), F32), SDS((d, n), BF16)],
        compiler_params=_params("arbitrary"),
        name="merge",
    )(x, yc, yh, ya, z, wb, wo, g2)


def _take_max(work, row_iota, n_rows):
    m = jnp.max(work, axis=0, keepdims=True)
    first = jnp.min(jnp.where(work == m, row_iota, float(n_rows)), axis=0, keepdims=True)
    return m, jnp.where(row_iota == first, -jnp.inf, work)


def _oddeven_merge_sort_pairs(n):
    pairs, p = [], 1
    while p < n:
        k = p
        while k >= 1:
            for j in range(k % p, n - k, 2 * k):
                for i in range(min(k, n - j - k)):
                    if (i + j) // (2 * p) == (i + j + k) // (2 * p):
                        pairs.append((i + j, i + j + k))
            k //= 2
        p *= 2
    return tuple(pairs)


_SUBLANES = 8
_SORT_PAIRS = _oddeven_merge_sort_pairs(PEER_TOPK)


def _exchange(v, a, b):
    v[a], v[b] = jnp.maximum(v[a], v[b]), jnp.minimum(v[a], v[b])


def _top16_sorted(s):
    v = [s[_SUBLANES * i:_SUBLANES * (i + 1), :] for i in range(PEER_TOPK)]
    for a, b in _SORT_PAIRS:
        _exchange(v, a, b)
    for shift in (4, 6, 7):
        w = [pltpu.roll(x, shift, 0) for x in v]
        v = [jnp.maximum(v[k], w[PEER_TOPK - 1 - k]) for k in range(PEER_TOPK)]
        d = PEER_TOPK // 2
        while d >= 1:
            for i in range(PEER_TOPK):
                if not i & d:
                    _exchange(v, i, i + d)
            d //= 2
    return [x[0:1, :] for x in v]


_N_CAND = -(-len(_PAIRS) // 8) * 8


def _peer_route_kernel(xnt_ref, wqt_ref, keys_ref, cut1_ref, e1_ref, rank2_ref, e2_ref, cand_ref, *, t):
    xnt = xnt_ref[...]
    n_cand = _N_CAND
    cand_iota = lax.broadcasted_iota(jnp.int32, (n_cand, t), 0).astype(F32)
    cand_ref[len(_PAIRS):, :] = jnp.full((n_cand - len(_PAIRS), t), -jnp.inf, F32)

    def head(h, carry):
        dk = 2 * N_KEYS
        qt = jnp.dot(wqt_ref[pl.ds(pl.multiple_of(h * dk, dk), dk), :], xnt, preferred_element_type=F32)
        scores, tops = [], []
        for p in range(2):
            s = jnp.dot(keys_ref[h, p], qt[p * N_KEYS:(p + 1) * N_KEYS].astype(BF16), preferred_element_type=F32)
            scores.append(s)
            tops.append(_top16_sorted(s))
        for i, (a, b) in enumerate(_PAIRS):
            cand_ref[i:i + 1, :] = tops[0][a] + tops[1][b]
        work = cand_ref[...]
        best = []
        for _ in range(PEER_TOPK):
            m, work = _take_max(work, cand_iota, n_cand)
            best.append(m)
        norm = jnp.zeros((1, t), F32)
        for m in best:
            norm = norm + jnp.exp(m - best[0])
        cut = jnp.zeros((N_KEYS, t), F32)
        rank = jnp.full((N_KEYS, t), float(PEER_TOPK), F32)
        for b in range(PEER_TOPK):
            cut = jnp.where(scores[0] + tops[1][b] >= best[-1], float(b + 1), cut)
            rb = PEER_TOPK - 1 - b
            rank = jnp.where(scores[1] >= tops[1][rb], float(rb), rank)
        cut1_ref[h] = cut
        e1_ref[h] = jnp.exp(scores[0] - tops[0][0]) / norm
        rank2_ref[h] = rank.astype(BF16)
        e2_ref[h] = jnp.exp(scores[1] - tops[1][0]).astype(BF16)
        return carry

    lax.fori_loop(0, PEER_HEADS, head, 0)


def _peer_route(xnt, wqt, keys, t):
    d, n = xnt.shape
    big = pl.BlockSpec((PEER_HEADS, N_KEYS, t), lambda i: (0, 0, i))
    wide, narrow = SDS((PEER_HEADS, N_KEYS, n), F32), SDS((PEER_HEADS, N_KEYS, n), BF16)
    return pl.pallas_call(
        functools.partial(_peer_route_kernel, t=t),
        grid=(n // t,),
        in_specs=[
            pl.BlockSpec((d, t), lambda i: (0, i)),
            pl.BlockSpec(wqt.shape, lambda i: (0, 0)),
            pl.BlockSpec(keys.shape, lambda i: (0, 0, 0, 0)),
        ],
        out_specs=[big, big, big, big],
        out_shape=[wide, wide, narrow, narrow],
        scratch_shapes=[pltpu.VMEM((_N_CAND, t), F32)],
        compiler_params=_params("arbitrary"),
        name="peer_route",
    )(xnt, wqt, keys)


_BF16_ROWS = 16
_PEER_CHUNK = 2048


def _peer_dense_kernel(xnt_ref, h_ref, u_ref, vt_ref, cut1_ref, e1_ref, rank2_ref, e2_ref, o_ref,
                       acc_ref, act_ref, cutx_ref, e1x_ref, rank2x_ref, e2x_ref, *, t, ec):
    c = pl.program_id(1)
    n_chunks = pl.num_programs(1) - 1

    @pl.when(c == 0)
    def _():
        acc_ref[...] = jnp.zeros_like(acc_ref)
        act_ref[1] = jnp.zeros((ec, t), BF16)
        rank2x_ref[...] = rank2_ref[...]
        e2x_ref[...] = e2_ref[...]

        def expand(i, carry):
            for h in range(PEER_HEADS):
                cutx_ref[h, i] = jnp.broadcast_to(cut1_ref[h, pl.ds(i, 1), :], (_BF16_ROWS, t)).astype(BF16)
                e1x_ref[h, i] = jnp.broadcast_to(e1_ref[h, pl.ds(i, 1), :], (_BF16_ROWS, t)).astype(BF16)
            return carry

        lax.fori_loop(0, N_KEYS, expand, 0)

    acc_ref[...] += jnp.dot(vt_ref[...], act_ref[(c + 1) & 1], preferred_element_type=F32)

    chunk = jnp.minimum(c, n_chunks - 1)
    n_i1 = ec // N_KEYS
    slot = c & 1
    rep = N_KEYS // _BF16_ROWS
    for ii in range(n_i1):
        i1 = chunk * n_i1 + ii
        rows = slice(ii * N_KEYS, (ii + 1) * N_KEYS)
        hid = jnp.dot(u_ref[rows, :], xnt_ref[...], preferred_element_type=F32)
        wsum = jnp.zeros((N_KEYS, t), BF16)
        for h in range(PEER_HEADS):
            cut = jnp.concatenate([cutx_ref[h, i1]] * rep, axis=0)
            e1 = jnp.concatenate([e1x_ref[h, i1]] * rep, axis=0)
            wsum = wsum + jnp.where(rank2x_ref[h] < cut, e1 * e2x_ref[h], jnp.zeros((), BF16))
        gelu = 0.5 * hid * (1.0 + lax.erf(hid * (2.0 ** -0.5)))
        act_ref[slot, rows, :] = gelu.astype(BF16) * wsum

    @pl.when(c == n_chunks)
    def _():
        o_ref[...] = h_ref[...] + acc_ref[...].T


def _peer_dense(xnt, h, u, vt, route, t, ec):
    d, n = xnt.shape
    nc = u.shape[0] // ec
    big = pl.BlockSpec((PEER_HEADS, N_KEYS, t), lambda i, c: (0, 0, i))
    return pl.pallas_call(
        functools.partial(_peer_dense_kernel, t=t, ec=ec),
        grid=(n // t, nc + 1),
        in_specs=[
            pl.BlockSpec((d, t), lambda i, c: (0, i)),
            pl.BlockSpec((t, d), lambda i, c: (i, 0)),
            pl.BlockSpec((ec, d), lambda i, c: (jnp.minimum(c, nc - 1), 0)),
            pl.BlockSpec((d, ec), lambda i, c: (0, jnp.maximum(c - 1, 0))),
            big, big, big, big,
        ],
        out_specs=pl.BlockSpec((t, d), lambda i, c: (i, 0)),
        out_shape=SDS((n, d), F32),
        scratch_shapes=[pltpu.VMEM((d, t), F32), pltpu.VMEM((2, ec, t), BF16),
                        pltpu.VMEM((PEER_HEADS, N_KEYS, _BF16_ROWS, t), BF16),
                        pltpu.VMEM((PEER_HEADS, N_KEYS, _BF16_ROWS, t), BF16),
                        pltpu.VMEM((PEER_HEADS, N_KEYS, t), BF16), pltpu.VMEM((PEER_HEADS, N_KEYS, t), BF16)],
        compiler_params=_params("arbitrary", "arbitrary", vmem_mb=56),
        name="peer_dense",
    )(xnt, h, u, vt, *route)


def _final_norm_kernel(x_ref, g_ref, o_ref):
    o_ref[...] = _rms(x_ref[...], g_ref[...])


def _final_norm(x, g, tm):
    n, d = x.shape
    return pl.pallas_call(
        _final_norm_kernel,
        grid=(n // tm,),
        in_specs=[pl.BlockSpec((tm, d), lambda i: (i, 0)), pl.BlockSpec((1, d), lambda i: (0, 0))],
        out_specs=pl.BlockSpec((tm, d), lambda i: (i, 0)),
        out_shape=SDS((n, d), F32),
        compiler_params=_params("arbitrary"),
        name="final_norm",
    )(x, g)


def _tile(n, pref):
    return pref if n % pref == 0 else n


def _layer_weights(l, lbs, norm1_g, w_in, conv_dw, conv_db, conv_ln_g, conv_ln_b, hg_norm_g, att_fb,
                   w_branch, w_out, norm2_g, peer_wq, peer_keys, peer_u, peer_v):
    n_main = 2 * BRANCH_W + 4 * HG_HEADS * HG_DK + 3 * ATT_HEADS * ATT_HD
    w = w_in[l]
    row = lambda a: a.reshape(1, -1).astype(F32)
    return dict(
        g1=row(norm1_g[l]),
        w_main=jnp.concatenate([w[:, n_main + ATT_HEADS:], w[:, :n_main]], axis=1).astype(BF16),
        w_f=jnp.pad(w[:, n_main:n_main + ATT_HEADS], ((0, 0), (0, LANES - ATT_HEADS))).astype(BF16),
        fb=jnp.pad(att_fb[l].astype(F32), (0, LANES - ATT_HEADS)).reshape(1, LANES),
        conv_w=conv_dw[l].astype(F32), conv_b=row(conv_db[l]), ln_g=row(conv_ln_g[l]), ln_b=row(conv_ln_b[l]),
        lb=row(lbs[l]), hg_g=row(hg_norm_g[l]),
        wb=w_branch[l].astype(BF16), wo=w_out[l].astype(BF16), g2=row(norm2_g[l]),
        wqt=peer_wq[l].T.astype(BF16), keys=peer_keys[l].astype(BF16),
        u=peer_u[l].astype(BF16), vt=peer_v[l].T.astype(BF16),
    )


def _token_mixers(x, wt, nb, seq, conv_buf, hg_state, attend):
    n = nb * seq
    z, lf = _in_proj(x, wt["g1"], wt["w_main"], wt["w_f"], wt["fb"], _tile(n, 1024))
    y_conv, conv_new = _conv_branch(z, conv_buf, wt["conv_w"], wt["conv_b"], wt["ln_g"], wt["ln_b"],
                                    nb, seq, _tile(seq, 512))
    y_hgrn, hg_new = _hgrn_branch(z, hg_state, wt["lb"], wt["hg_g"], nb, seq, _tile(seq, 128), _tile(seq, 16))
    y_att = attend(z, lf)
    h, xn = _merge(x, y_conv, y_hgrn, y_att, z, wt["wb"], wt["wo"], wt["g2"], _tile(n, 512))
    tp = _tile(n, 256)
    route = _peer_route(xn, wt["wqt"], wt["keys"], tp)
    out = _peer_dense(xn, h, wt["u"], wt["vt"], route, tp, _PEER_CHUNK)
    w = ATT_HEADS * ATT_HD
    k_rows = z[:, ZB_AK * BRANCH_W:ZB_AK * BRANCH_W + w].reshape(nb, seq, ATT_HEADS, ATT_HD)
    v_rows = z[:, ZB_AV * BRANCH_W:ZB_AV * BRANCH_W + w].reshape(nb, seq, ATT_HEADS, ATT_HD)
    logf = lf[:, :ATT_HEADS].reshape(nb, seq, ATT_HEADS)
    return out, conv_new, hg_new, k_rows, v_rows, logf


def kernel(x_prompt, x_sample, cache_k, cache_v, cache_logf, state_conv, state_hgrn, page_table, norm1_g, w_in, conv_dw, conv_db, conv_ln_g, conv_ln_b, hg_lb_logits, hg_norm_g, att_fb, w_branch, w_out, norm2_g, peer_wq, peer_keys, peer_u, peer_v, final_g):
    depth = w_in.shape[0]
    bp, sp, d = x_prompt.shape
    bs, ss, _ = x_sample.shape
    w = ATT_HEADS * ATT_HD
    probs = jax.nn.softmax(hg_lb_logits.astype(F32), axis=0)
    lbs = jnp.cumsum(probs, axis=0) - probs[0:1]
    n_phys = cache_k.shape[1]
    ckt = jnp.transpose(cache_k, (0, 1, 3, 4, 2)).reshape(depth * n_phys, ATT_HEADS, ATT_HD, PAGE_SIZE)
    cvt = jnp.transpose(cache_v, (0, 1, 3, 4, 2)).reshape(depth * n_phys, ATT_HEADS, ATT_HD, PAGE_SIZE)
    lf_rows = jnp.swapaxes(cache_logf.astype(F32), 2, 3).reshape(depth * n_phys * ATT_HEADS, PAGE_SIZE)
    suffix, total = (a.reshape(depth * n_phys, ATT_HEADS, PAGE_SIZE)
                     for a in _page_suffix(lf_rows, _tile(lf_rows.shape[0], 2048)))

    hp = x_prompt.reshape(bp * sp, d)
    hs = x_sample.reshape(bs * ss, d)
    outs_p, outs_s = [], []
    for l in range(depth):
        wt = _layer_weights(l, lbs, norm1_g, w_in, conv_dw, conv_db, conv_ln_g, conv_ln_b, hg_norm_g, att_fb,
                            w_branch, w_out, norm2_g, peer_wq, peer_keys, peer_u, peer_v)

        def attend_prompt(z, lf):
            ct = _logf_cumsum(lf, bp, sp, _tile(sp, 512))
            return _fox_prompt(z, ct, bp, sp, _tile(sp, _FOX_BLOCK))

        def attend_sample(z, lf, l=l):
            lfn_t = jnp.swapaxes(lf[:, :ATT_HEADS].reshape(bs, ss, ATT_HEADS), 1, 2)
            lfn_t = jnp.pad(lfn_t, ((0, 0), (0, 0), (0, PAGE_SIZE - ss)))
            return _fox_sample(z, lfn_t, ckt, cvt, suffix, total, page_table, l * n_phys, bs, ss)

        hp, *rp = _token_mixers(hp, wt, bp, sp, jnp.zeros((bp, CONV_K - 1, BRANCH_W), F32),
                                jnp.zeros((bp, HG_HEADS, HG_DK, HG_DK), F32), attend_prompt)
        hs, *rs = _token_mixers(hs, wt, bs, ss, state_conv[l], state_hgrn[l], attend_sample)
        outs_p.append(rp)
        outs_s.append(rs)

    fg = final_g.reshape(1, d).astype(F32)
    y_prompt = _final_norm(hp, fg, _tile(bp * sp, 1024)).reshape(bp, sp, d)
    y_sample = _final_norm(hs, fg, _tile(bs * ss, 1024)).reshape(bs, ss, d)
    stack = lambda outs, i: jnp.stack([r[i] for r in outs])
    return (y_prompt, y_sample,
            stack(outs_p, 2), stack(outs_p, 3), stack(outs_p, 4), stack(outs_p, 0), stack(outs_p, 1),
            stack(outs_s, 2), stack(outs_s, 3), stack(outs_s, 4), stack(outs_s, 0), stack(outs_s, 1))
```

```python
import functools

import jax
import jax.numpy as jnp
from jax import lax
from jax.experimental import pallas as pl
from jax.experimental.pallas import tpu as pltpu

F32 = jnp.float32
BF16 = jnp.bfloat16
HI = lax.Precision.HIGHEST
SDS = jax.ShapeDtypeStruct

D_MODEL = 1024
BRANCH_W = 512
CONV_K = 31
HG_HEADS = 4
HG_DK = 128
ATT_HEADS = 8
ATT_HD = 64
PEER_HEADS = 8
N_KEYS = 128
PEER_TOPK = 16
PAGE_SIZE = 128
EPS = 1e-6
GATE_FLOOR = 1e-20
MASK_VALUE = -1e30
LANES = 128
_HD_SHIFT = ATT_HD.bit_length() - 1
_HEAD_SHIFT = ATT_HEADS.bit_length() - 1

Z_COLS = 15 * BRANCH_W
ZB_CONV = 3
ZB_HQ, ZB_HF, ZB_HI, ZB_HG, ZB_AQ, ZB_AK, ZB_AV = 8, 9, 10, 11, 12, 13, 14

_PAIRS = tuple((a, b) for a in range(PEER_TOPK) for b in range(PEER_TOPK) if (a + 1) * (b + 1) <= PEER_TOPK)

_NT = (((1,), (1,)), ((), ()))
_TN = (((0,), (0,)), ((), ()))


def _params(*sem, vmem_mb=48):
    return pltpu.CompilerParams(dimension_semantics=sem, vmem_limit_bytes=vmem_mb * 1024 * 1024)


def _sigmoid(x):
    return 1.0 / (1.0 + jnp.exp(-x))


def _log_sigmoid(x):
    return jnp.minimum(x, 0.0) - jnp.log1p(jnp.exp(-jnp.abs(x)))


def _rms(x, g):
    return x * lax.rsqrt(jnp.mean(x * x, axis=-1, keepdims=True) + EPS) * g


def _in_proj_kernel(x_ref, g_ref, w_ref, wf_ref, fb_ref, z_ref, lf_ref, xn_ref):
    @pl.when(pl.program_id(1) == 0)
    def _():
        xn = _rms(x_ref[...], g_ref[...]).astype(BF16)
        xn_ref[...] = xn
        af = jnp.dot(xn, wf_ref[...], preferred_element_type=F32) + fb_ref[...]
        lf_ref[...] = _log_sigmoid(af)

    z_ref[...] = jnp.dot(xn_ref[...], w_ref[...], preferred_element_type=F32)


def _in_proj(x, g, w_main, w_f, fb, tm):
    n, d = x.shape
    tn = 3 * BRANCH_W
    return pl.pallas_call(
        _in_proj_kernel,
        grid=(n // tm, Z_COLS // tn),
        in_specs=[
            pl.BlockSpec((tm, d), lambda i, j: (i, 0)),
            pl.BlockSpec((1, d), lambda i, j: (0, 0)),
            pl.BlockSpec((d, tn), lambda i, j: (0, j)),
            pl.BlockSpec((d, LANES), lambda i, j: (0, 0)),
            pl.BlockSpec((1, LANES), lambda i, j: (0, 0)),
        ],
        out_specs=[
            pl.BlockSpec((tm, tn), lambda i, j: (i, j)),
            pl.BlockSpec((tm, LANES), lambda i, j: (i, 0)),
        ],
        out_shape=[SDS((n, Z_COLS), F32), SDS((n, LANES), F32)],
        scratch_shapes=[pltpu.VMEM((tm, d), BF16)],
        compiler_params=_params("arbitrary", "arbitrary"),
        name="in_proj",
    )(x, g, w_main, w_f, fb)


_CONV_PAD = 32


def _conv_kernel(a_ref, buf_ref, w_ref, b_ref, g_ref, bt_ref, y_ref, nc_ref, ext_ref, *, tl):
    hist = CONV_K - 1
    lo = _CONV_PAD - hist

    @pl.when(pl.program_id(1) == 0)
    def _():
        ext_ref[lo:_CONV_PAD, :] = buf_ref[0]

    a = a_ref[...]
    ext_ref[_CONV_PAD:_CONV_PAD + tl, :] = a[:, :BRANCH_W] * _sigmoid(a[:, BRANCH_W:])
    acc = jnp.zeros((tl, BRANCH_W), F32)
    for j in range(CONV_K):
        acc = acc + w_ref[j:j + 1, :] * ext_ref[lo + j:lo + j + tl, :]
    y = acc + b_ref[...]
    mu = jnp.mean(y, axis=-1, keepdims=True)
    yc = y - mu
    var = jnp.mean(yc * yc, axis=-1, keepdims=True)
    y = yc * lax.rsqrt(var + EPS) * g_ref[...] + bt_ref[...]
    y_ref[...] = y * _sigmoid(y)
    tail = ext_ref[tl + lo:tl + _CONV_PAD, :]
    nc_ref[0] = tail
    ext_ref[lo:_CONV_PAD, :] = tail


def _conv_branch(z, buf, w, b, g, bt, nb, seq, tl):
    nt = seq // tl
    hist = CONV_K - 1
    vec = pl.BlockSpec((1, BRANCH_W), lambda i, t: (0, 0))
    return pl.pallas_call(
        functools.partial(_conv_kernel, tl=tl),
        grid=(nb, nt),
        in_specs=[
            pl.BlockSpec((tl, 2 * BRANCH_W), lambda i, t: (i * nt + t, ZB_CONV)),
            pl.BlockSpec((1, hist, BRANCH_W), lambda i, t: (i, 0, 0)),
            pl.BlockSpec((CONV_K, BRANCH_W), lambda i, t: (0, 0)),
            vec, vec, vec,
        ],
        out_specs=[
            pl.BlockSpec((tl, BRANCH_W), lambda i, t: (i * nt + t, 0)),
            pl.BlockSpec((1, hist, BRANCH_W), lambda i, t: (i, 0, 0)),
        ],
        out_shape=[SDS((nb * seq, BRANCH_W), F32), SDS((nb, hist, BRANCH_W), F32)],
        scratch_shapes=[pltpu.VMEM((_CONV_PAD + tl, BRANCH_W), F32)],
        compiler_params=_params("arbitrary", "arbitrary"),
        name="conv_branch",
    )(z, buf, w, b, g, bt)


def _hgrn_kernel(q_ref, f_ref, i_ref, g_ref, s0_ref, lb_ref, ng_ref, y_ref, sn_ref,
                 st_ref, qs_ref, ks_ref, bs_ref, *, tl, c):
    t = pl.program_id(1)

    @pl.when(t == 0)
    def _():
        for h in range(HG_HEADS):
            st_ref[h] = s0_ref[0, h].T

    lb = jnp.clip(lb_ref[...], 0.0, 1.0)
    f = lb + (1.0 - lb) * _sigmoid(f_ref[...])
    lf = jnp.log(jnp.maximum(f, GATE_FLOOR))
    q = q_ref[...]
    qs_ref[...] = q * _sigmoid(q)
    ks_ref[...] = 1.0 - f
    r = lax.broadcasted_iota(jnp.int32, (tl, tl), 0)
    cc = lax.broadcasted_iota(jnp.int32, (tl, tl), 1)
    tri = jnp.where(cc <= r, jnp.where(cc >= (r & -c), 1.0, 0.0), 0.0).astype(F32)
    bs_ref[...] = jnp.dot(tri, lf, precision=HI, preferred_element_type=F32)
    rowid = lax.broadcasted_iota(jnp.int32, (c, 1), 0)

    def chunk(ci, carry):
        off = pl.multiple_of(ci * c, c)
        for h in range(HG_HEADS):
            sl = slice(h * HG_DK, (h + 1) * HG_DK)
            qc = qs_ref[pl.ds(off, c), sl]
            kc = ks_ref[pl.ds(off, c), sl]
            bc = bs_ref[pl.ds(off, c), sl]
            vc = i_ref[pl.ds(off, c), sl]
            o = jnp.zeros((c, HG_DK), F32)
            for s in range(c):
                k_row = kc[s:s + 1, :]
                b_row = bc[s:s + 1, :]
                v_row = vc[s:s + 1, :]
                p = qc * k_row * jnp.exp(jnp.minimum(bc - b_row, 0.0))
                a = jnp.sum(p, axis=1, keepdims=True)
                o = o + jnp.where(rowid >= s, a, 0.0) * v_row
            st = st_ref[h]
            qt = (qc * jnp.exp(bc)).astype(BF16)
            o = o + lax.dot_general(qt, st.astype(BF16), _NT, preferred_element_type=F32)
            b_last = bc[c - 1:c, :]
            kt = (kc * jnp.exp(b_last - bc)).astype(BF16)
            upd = lax.dot_general(vc.astype(BF16), kt, _TN, preferred_element_type=F32)
            st_ref[h] = st * jnp.exp(b_last) + upd
            o = _rms(o, ng_ref[...])
            y_ref[pl.ds(off, c), sl] = o * _sigmoid(g_ref[pl.ds(off, c), sl])
        return carry

    lax.fori_loop(0, tl // c, chunk, 0)

    @pl.when(t == pl.num_programs(1) - 1)
    def _():
        for h in range(HG_HEADS):
            sn_ref[0, h] = st_ref[h].T


def _hgrn_branch(z, s0, lb, ng, nb, seq, tl, c):
    nt = seq // tl
    w = HG_HEADS * HG_DK

    def col(j):
        return pl.BlockSpec((tl, w), lambda i, t: (i * nt + t, j))

    st_spec = pl.BlockSpec((1, HG_HEADS, HG_DK, HG_DK), lambda i, t: (i, 0, 0, 0))
    return pl.pallas_call(
        functools.partial(_hgrn_kernel, tl=tl, c=c),
        grid=(nb, nt),
        in_specs=[col(ZB_HQ), col(ZB_HF), col(ZB_HI), col(ZB_HG), st_spec,
                  pl.BlockSpec((1, w), lambda i, t: (0, 0)),
                  pl.BlockSpec((1, HG_DK), lambda i, t: (0, 0))],
        out_specs=[pl.BlockSpec((tl, w), lambda i, t: (i * nt + t, 0)), st_spec],
        out_shape=[SDS((nb * seq, w), F32), SDS((nb, HG_HEADS, HG_DK, HG_DK), F32)],
        scratch_shapes=[pltpu.VMEM((HG_HEADS, HG_DK, HG_DK), F32),
                        pltpu.VMEM((tl, w), F32), pltpu.VMEM((tl, w), F32), pltpu.VMEM((tl, w), F32)],
        compiler_params=_params("arbitrary", "arbitrary"),
        name="hgrn_branch",
    )(z, z, z, z, s0, lb, ng)


def _cumsum_kernel(lf_ref, ct_ref, carry_ref, *, tl):
    @pl.when(pl.program_id(1) == 0)
    def _():
        carry_ref[...] = jnp.zeros_like(carry_ref)

    r = lax.broadcasted_iota(jnp.int32, (tl, tl), 0)
    cc = lax.broadcasted_iota(jnp.int32, (tl, tl), 1)
    tri = jnp.where(cc <= r, 1.0, 0.0).astype(F32)
    cs = jnp.dot(tri, lf_ref[...], precision=HI, preferred_element_type=F32) + carry_ref[...]
    carry_ref[...] = cs[tl - 1:tl, :]
    ct_ref[0] = cs.T[:ATT_HEADS, :]


def _logf_cumsum(lf, nb, seq, tl):
    nt = seq // tl
    return pl.pallas_call(
        functools.partial(_cumsum_kernel, tl=tl),
        grid=(nb, nt),
        in_specs=[pl.BlockSpec((tl, LANES), lambda i, t: (i * nt + t, 0))],
        out_specs=pl.BlockSpec((1, ATT_HEADS, tl), lambda i, t: (i, 0, t)),
        out_shape=SDS((nb, ATT_HEADS, seq), F32),
        scratch_shapes=[pltpu.VMEM((1, LANES), F32)],
        compiler_params=_params("arbitrary", "arbitrary"),
        name="logf_cumsum",
    )(lf)


_LOG2E = 1.4426950408889634
_FOX_BLOCK = 1024
_FOX_SUB = 512


def _fox_kernel(q_ref, k_ref, v_ref, ck_ref, cq_ref, o_ref, qm_ref, m_ref, acc_ref, *, t):
    pair = pl.program_id(1)
    qi = pl.program_id(2)
    ki = pl.program_id(3)
    lane = lax.broadcasted_iota(jnp.int32, (1, LANES), 1)

    @pl.when(ki == 0)
    def _():
        q = q_ref[...] * (ATT_HD ** -0.5 * _LOG2E)
        qm_ref[0] = jnp.where(lane < ATT_HD, q, 0.0).astype(BF16)
        qm_ref[1] = jnp.where(lane >= ATT_HD, q, 0.0).astype(BF16)
        m_ref[...] = jnp.full_like(m_ref, MASK_VALUE)
        acc_ref[...] = jnp.zeros_like(acc_ref)

    def step(diagonal):
        k = k_ref[...].astype(BF16)
        v = jnp.concatenate([v_ref[...].astype(BF16), jnp.ones((t, LANES), BF16)], axis=1)
        head_iota = lax.broadcasted_iota(jnp.int32, (ATT_HEADS, 1), 0)
        sub = min(t, _FOX_SUB)
        biases, states = [], []
        for j in range(2):
            own = head_iota == 2 * pair + j
            ck = jnp.sum(jnp.where(own, ck_ref[0], 0.0), axis=0, keepdims=True)
            c0 = jnp.sum(jnp.where(own, cq_ref[0, :, 0:1], 0.0), axis=0, keepdims=True)
            biases.append((c0 - ck) * _LOG2E)
            states.append((qm_ref[j], m_ref[j], acc_ref[j]))

        def logits(item):
            j, r0 = item
            nk = r0 + sub if diagonal else t
            s = lax.dot_general(states[j][0][r0:r0 + sub], k[:nk], _NT, preferred_element_type=F32)
            s = s + biases[j][:, :nk]
            if diagonal:
                rid = lax.broadcasted_iota(jnp.int32, (sub, nk), 0) + r0
                cid = lax.broadcasted_iota(jnp.int32, (sub, nk), 1)
                s = jnp.where(cid <= rid, s, MASK_VALUE)
            return s, nk

        items = [(j, r0) for j in range(2) for r0 in range(0, t, sub)]
        m_out, acc_out = ([], []), ([], [])
        nxt = logits(items[0])
        for idx, (j, r0) in enumerate(items):
            s, nk = nxt
            if idx + 1 < len(items):
                nxt = logits(items[idx + 1])
            rows = slice(r0, r0 + sub)
            m_old = states[j][1][rows]
            m_new = jnp.maximum(m_old, jnp.max(s, axis=1, keepdims=True))
            alpha = jnp.exp2(m_old - m_new)
            p = jnp.exp2(s - m_new[:, 0:1]).astype(BF16)
            pv = jnp.dot(p, v[:nk], preferred_element_type=F32)
            acc_out[j].append(jnp.concatenate([alpha, alpha], axis=1) * states[j][2][rows] + pv)
            m_out[j].append(m_new)
        for j in range(2):
            acc_ref[j] = jnp.concatenate(acc_out[j], axis=0)
            m_ref[j] = jnp.concatenate(m_out[j], axis=0)

    @pl.when(ki < qi)
    def _():
        step(False)

    @pl.when(ki == qi)
    def _():
        step(True)
        o0 = acc_ref[0, :, :LANES] / acc_ref[0, :, LANES:]
        o1 = acc_ref[1, :, :LANES] / acc_ref[1, :, LANES:]
        o_ref[...] = jnp.where(lane < ATT_HD, o0, o1)


def _fox_prompt(z, ct, nb, seq, t):
    nq = seq // t
    npair = ATT_HEADS // 2
    zq, zk, zv = (zb * (BRANCH_W // LANES) for zb in (ZB_AQ, ZB_AK, ZB_AV))
    return pl.pallas_call(
        functools.partial(_fox_kernel, t=t),
        grid=(nb, npair, nq, nq),
        in_specs=[
            pl.BlockSpec((t, LANES), lambda b, p, qi, ki: (b * nq + qi, zq + p)),
            pl.BlockSpec((t, LANES), lambda b, p, qi, ki: (b * nq + jnp.minimum(ki, qi), zk + p)),
            pl.BlockSpec((t, LANES), lambda b, p, qi, ki: (b * nq + jnp.minimum(ki, qi), zv + p)),
            pl.BlockSpec((1, ATT_HEADS, t), lambda b, p, qi, ki: (b, 0, jnp.minimum(ki, qi))),
            pl.BlockSpec((1, ATT_HEADS, t), lambda b, p, qi, ki: (b, 0, qi)),
        ],
        out_specs=pl.BlockSpec((t, LANES), lambda b, p, qi, ki: (b * nq + qi, p)),
        out_shape=SDS((nb * seq, BRANCH_W), F32),
        scratch_shapes=[pltpu.VMEM((2, t, LANES), BF16), pltpu.VMEM((2, t, LANES), F32),
                        pltpu.VMEM((2, t, 2 * LANES), F32)],
        compiler_params=_params("arbitrary", "arbitrary", "arbitrary", "arbitrary"),
        name="fox_prompt",
    )(z, z, z, ct, ct)


def _page_suffix_kernel(lf_ref, suf_ref, tot_ref):
    r = lax.broadcasted_iota(jnp.int32, (PAGE_SIZE, PAGE_SIZE), 0)
    c = lax.broadcasted_iota(jnp.int32, (PAGE_SIZE, PAGE_SIZE), 1)
    lf = lf_ref[...]
    suf_ref[...] = jnp.dot(lf, jnp.where(r > c, 1.0, 0.0).astype(F32), precision=HI, preferred_element_type=F32)
    tot_ref[...] = jnp.dot(lf, jnp.ones((PAGE_SIZE, PAGE_SIZE), F32), precision=HI, preferred_element_type=F32)


def _page_suffix(lf_rows, tr):
    n = lf_rows.shape[0]
    spec = pl.BlockSpec((tr, PAGE_SIZE), lambda i: (i, 0))
    return pl.pallas_call(
        _page_suffix_kernel,
        grid=(n // tr,),
        in_specs=[spec],
        out_specs=[spec, spec],
        out_shape=[SDS((n, PAGE_SIZE), F32), SDS((n, PAGE_SIZE), F32)],
        compiler_params=_params("arbitrary"),
        name="page_suffix",
    )(lf_rows)


_PAGES_PER_STEP = 8


def _fox_sample_kernel(pt_ref, q_ref, kn_ref, vn_ref, lfn_ref, *rest, g, tt):
    del pt_ref
    kp_refs, vp_refs, sf_refs, tot_refs = rest[:g], rest[g:2 * g], rest[2 * g:3 * g], rest[3 * g:4 * g]
    o_ref, qb_ref, m_ref, l_ref, acc_ref, car_ref, cn_ref = rest[4 * g:]
    step = pl.program_id(1)
    rows = tt * ATT_HEADS
    w = ATT_HEADS * ATT_HD
    rid = lax.broadcasted_iota(jnp.int32, (rows, 1), 0)
    hmask = (lax.broadcasted_iota(jnp.int32, (rows, w), 1) >> _HD_SHIFT) == (rid & (ATT_HEADS - 1))
    tok = rid >> _HEAD_SHIFT
    lane = lax.broadcasted_iota(jnp.int32, (rows, PAGE_SIZE), 1)

    def tile_rows(x):
        return jnp.broadcast_to(x[None], (tt,) + x.shape).reshape(rows, x.shape[-1])

    def online(scores, values, dims):
        m_old = m_ref[...]
        m_new = m_old
        for s in scores:
            m_new = jnp.maximum(m_new, jnp.max(s, axis=1, keepdims=True))
        alpha = jnp.exp(m_old - m_new)
        l_new = alpha * l_ref[...]
        acc = alpha * acc_ref[...]
        for s, v in zip(scores, values):
            p = jnp.exp(s - m_new)
            l_new = l_new + jnp.sum(p, axis=1, keepdims=True)
            acc = acc + lax.dot_general(p.astype(BF16), v, dims, preferred_element_type=F32)
        l_ref[...] = l_new
        acc_ref[...] = acc
        m_ref[...] = m_new

    @pl.when(step == 0)
    def _():
        q = q_ref[...] * (ATT_HD ** -0.5)
        qrep = jnp.broadcast_to(q[:, None, :], (tt, ATT_HEADS, w)).reshape(rows, w)
        qb_ref[...] = jnp.where(hmask, qrep, 0.0).astype(BF16)
        m_ref[...] = jnp.full_like(m_ref, MASK_VALUE)
        l_ref[...] = jnp.zeros_like(l_ref)
        acc_ref[...] = jnp.zeros_like(acc_ref)
        car_ref[...] = jnp.zeros_like(car_ref)
        r2 = lax.broadcasted_iota(jnp.int32, (PAGE_SIZE, PAGE_SIZE), 0)
        c2 = lax.broadcasted_iota(jnp.int32, (PAGE_SIZE, PAGE_SIZE), 1)
        incl = jnp.where(r2 <= c2, 1.0, 0.0).astype(F32)
        cn_ref[...] = jnp.dot(tile_rows(lfn_ref[0]), incl, precision=HI, preferred_element_type=F32)

    qb = qb_ref[...]
    cn = cn_ref[...]
    cn_own = jnp.sum(jnp.where(lane == tok, cn, 0.0), axis=1, keepdims=True)
    car = car_ref[...]
    scores = []
    for i in range(g):
        kt = kp_refs[i][0].reshape(w, PAGE_SIZE).astype(BF16)
        s = jnp.dot(qb, kt, preferred_element_type=F32)
        scores.append(s + ((cn_own + car) + tile_rows(sf_refs[i][0])))
        car = car + tile_rows(tot_refs[i][0])[:, 0:1]
    car_ref[...] = car
    online(scores, [vp_refs[i][0].reshape(w, PAGE_SIZE).astype(BF16) for i in range(g)], _NT)

    @pl.when(step == pl.num_programs(1) - 1)
    def _():
        pad = jnp.zeros((PAGE_SIZE - tt, w), F32)
        kn = jnp.concatenate([kn_ref[...], pad], axis=0).astype(BF16)
        vn = jnp.concatenate([vn_ref[...], pad], axis=0).astype(BF16)
        sn = lax.dot_general(qb, kn, _NT, preferred_element_type=F32)
        sn = jnp.where(lane <= tok, sn + (cn_own - cn), MASK_VALUE)
        online([sn], [vn], (((1,), (0,)), ((), ())))
        o = jnp.where(hmask, acc_ref[...] / l_ref[...], 0.0)
        o_ref[...] = jnp.sum(o.reshape(tt, ATT_HEADS, w), axis=1)


def _fox_sample(z, lfn_t, ckt, cvt, suffix, total, page_table, page_base, nb, tt):
    npg = page_table.shape[1]
    g = _PAGES_PER_STEP
    assert npg % g == 0
    w = ATT_HEADS * ATT_HD
    rows = tt * ATT_HEADS

    def page(i, ndim):
        def index(b, p, pt):
            return (page_base + pt[b * npg + (npg - 1 - (p * g + i))],) + (0,) * (ndim - 1)
        return index

    in_specs = [pl.BlockSpec((tt, w), lambda b, p, pt, zb=zb: (b, zb)) for zb in (ZB_AQ, ZB_AK, ZB_AV)]
    in_specs += [pl.BlockSpec((1, ATT_HEADS, PAGE_SIZE), lambda b, p, pt: (b, 0, 0))]
    in_specs += [pl.BlockSpec((1, ATT_HEADS, ATT_HD, PAGE_SIZE), page(i, 4)) for i in range(g)] * 2
    in_specs += [pl.BlockSpec((1, ATT_HEADS, PAGE_SIZE), page(i, 3)) for i in range(g)] * 2
    grid_spec = pltpu.PrefetchScalarGridSpec(
        num_scalar_prefetch=1,
        grid=(nb, npg // g),
        in_specs=in_specs,
        out_specs=pl.BlockSpec((tt, w), lambda b, p, pt: (b, 0)),
        scratch_shapes=[pltpu.VMEM((rows, w), BF16), pltpu.VMEM((rows, 1), F32), pltpu.VMEM((rows, 1), F32),
                        pltpu.VMEM((rows, w), F32), pltpu.VMEM((rows, 1), F32), pltpu.VMEM((rows, PAGE_SIZE), F32)],
    )
    return pl.pallas_call(
        functools.partial(_fox_sample_kernel, g=g, tt=tt),
        grid_spec=grid_spec,
        out_shape=SDS((nb * tt, w), F32),
        compiler_params=_params("arbitrary", "arbitrary"),
        name="fox_sample",
    )(page_table.reshape(-1), z, z, z, lfn_t, *([ckt] * g), *([cvt] * g), *([suffix] * g), *([total] * g))


def _merge_kernel(x_ref, yc_ref, yh_ref, ya_ref, gt_ref, wb_ref, wo_ref, g2_ref, h_ref, xnt_ref):
    merged = jnp.zeros(x_ref.shape, F32)
    for b, y_ref in enumerate((yc_ref, yh_ref, ya_ref)):
        proj = jnp.dot(y_ref[...].astype(BF16), wb_ref[b], preferred_element_type=F32)
        merged = merged + _sigmoid(gt_ref[:, b * D_MODEL:(b + 1) * D_MODEL]) * proj
    h = x_ref[...] + jnp.dot(merged.astype(BF16), wo_ref[...], preferred_element_type=F32)
    h_ref[...] = h
    xnt_ref[...] = _rms(h, g2_ref[...]).T.astype(BF16)


def _merge(x, yc, yh, ya, z, wb, wo, g2, tm):
    n, d = x.shape
    br = pl.BlockSpec((tm, BRANCH_W), lambda i: (i, 0))
    return pl.pallas_call(
        _merge_kernel,
        grid=(n // tm,),
        in_specs=[
            pl.BlockSpec((tm, d), lambda i: (i, 0)), br, br, br,
            pl.BlockSpec((tm, 3 * d), lambda i: (i, 0)),
            pl.BlockSpec((3, BRANCH_W, d), lambda i: (0, 0, 0)),
            pl.BlockSpec((d, d), lambda i: (0, 0)),
            pl.BlockSpec((1, d), lambda i: (0, 0)),
        ],
        out_specs=[pl.BlockSpec((tm, d), lambda i: (i, 0)), pl.BlockSpec((d, tm), lambda i: (0, i))],
        out_shape=[SDS((n, d), F32), SDS((d, n), BF16)],
        compiler_params=_params("arbitrary"),
        name="merge",
    )(x, yc, yh, ya, z, wb, wo, g2)


def _take_max(work, row_iota, n_rows):
    m = jnp.max(work, axis=0, keepdims=True)
    first = jnp.min(jnp.where(work == m, row_iota, float(n_rows)), axis=0, keepdims=True)
    return m, jnp.where(row_iota == first, -jnp.inf, work)


def _oddeven_merge_sort_pairs(n):
    pairs, p = [], 1
    while p < n:
        k = p
        while k >= 1:
            for j in range(k % p, n - k, 2 * k):
                for i in range(min(k, n - j - k)):
                    if (i + j) // (2 * p) == (i + j + k) // (2 * p):
                        pairs.append((i + j, i + j + k))
            k //= 2
        p *= 2
    return tuple(pairs)


_SUBLANES = 8
_SORT_PAIRS = _oddeven_merge_sort_pairs(PEER_TOPK)


def _exchange(v, a, b):
    v[a], v[b] = jnp.maximum(v[a], v[b]), jnp.minimum(v[a], v[b])


def _top16_sorted(s):
    v = [s[_SUBLANES * i:_SUBLANES * (i + 1), :] for i in range(PEER_TOPK)]
    for a, b in _SORT_PAIRS:
        _exchange(v, a, b)
    for shift in (4, 6, 7):
        w = [pltpu.roll(x, shift, 0) for x in v]
        v = [jnp.maximum(v[k], w[PEER_TOPK - 1 - k]) for k in range(PEER_TOPK)]
        d = PEER_TOPK // 2
        while d >= 1:
            for i in range(PEER_TOPK):
                if not i & d:
                    _exchange(v, i, i + d)
            d //= 2
    return [x[0:1, :] for x in v]


_N_CAND = -(-len(_PAIRS) // 8) * 8


def _peer_route_kernel(xnt_ref, wqt_ref, keys_ref, cut1_ref, e1_ref, rank2_ref, e2_ref, cand_ref, *, t):
    xnt = xnt_ref[...]
    n_cand = _N_CAND
    cand_iota = lax.broadcasted_iota(jnp.int32, (n_cand, t), 0).astype(F32)
    cand_ref[len(_PAIRS):, :] = jnp.full((n_cand - len(_PAIRS), t), -jnp.inf, F32)

    def head(h, carry):
        dk = 2 * N_KEYS
        qt = jnp.dot(wqt_ref[pl.ds(pl.multiple_of(h * dk, dk), dk), :], xnt, preferred_element_type=F32)
        scores, tops = [], []
        for p in range(2):
            s = jnp.dot(keys_ref[h, p], qt[p * N_KEYS:(p + 1) * N_KEYS].astype(BF16), preferred_element_type=F32)
            scores.append(s)
            tops.append(_top16_sorted(s))
        for i, (a, b) in enumerate(_PAIRS):
            cand_ref[i:i + 1, :] = tops[0][a] + tops[1][b]
        work = cand_ref[...]
        best = []
        for _ in range(PEER_TOPK):
            m, work = _take_max(work, cand_iota, n_cand)
            best.append(m)
        norm = jnp.zeros((1, t), F32)
        for m in best:
            norm = norm + jnp.exp(m - best[0])
        cut = jnp.zeros((N_KEYS, t), F32)
        rank = jnp.full((N_KEYS, t), float(PEER_TOPK), F32)
        for b in range(PEER_TOPK):
            cut = jnp.where(scores[0] + tops[1][b] >= best[-1], float(b + 1), cut)
            rb = PEER_TOPK - 1 - b
            rank = jnp.where(scores[1] >= tops[1][rb], float(rb), rank)
        cut1_ref[h] = cut
        e1_ref[h] = jnp.exp(scores[0] - tops[0][0]) / norm
        rank2_ref[h] = rank.astype(BF16)
        e2_ref[h] = jnp.exp(scores[1] - tops[1][0]).astype(BF16)
        return carry

    lax.fori_loop(0, PEER_HEADS, head, 0)


def _peer_route(xnt, wqt, keys, t):
    d, n = xnt.shape
    big = pl.BlockSpec((PEER_HEADS, N_KEYS, t), lambda i: (0, 0, i))
    wide, narrow = SDS((PEER_HEADS, N_KEYS, n), F32), SDS((PEER_HEADS, N_KEYS, n), BF16)
    return pl.pallas_call(
        functools.partial(_peer_route_kernel, t=t),
        grid=(n // t,),
        in_specs=[
            pl.BlockSpec((d, t), lambda i: (0, i)),
            pl.BlockSpec(wqt.shape, lambda i: (0, 0)),
            pl.BlockSpec(keys.shape, lambda i: (0, 0, 0, 0)),
        ],
        out_specs=[big, big, big, big],
        out_shape=[wide, wide, narrow, narrow],
        scratch_shapes=[pltpu.VMEM((_N_CAND, t), F32)],
        compiler_params=_params("arbitrary"),
        name="peer_route",
    )(xnt, wqt, keys)


_BF16_ROWS = 16
_PEER_CHUNK = 2048


def _peer_dense_kernel(xnt_ref, h_ref, u_ref, vt_ref, cut1_ref, e1_ref, rank2_ref, e2_ref, o_ref,
                       acc_ref, act_ref, cutx_ref, e1x_ref, rank2x_ref, e2x_ref, *, t, ec):
    c = pl.program_id(1)
    n_chunks = pl.num_programs(1) - 1

    @pl.when(c == 0)
    def _():
        acc_ref[...] = jnp.zeros_like(acc_ref)
        act_ref[1] = jnp.zeros((ec, t), BF16)
        rank2x_ref[...] = rank2_ref[...]
        e2x_ref[...] = e2_ref[...]

        def expand(i, carry):
            for h in range(PEER_HEADS):
                cutx_ref[h, i] = jnp.broadcast_to(cut1_ref[h, pl.ds(i, 1), :], (_BF16_ROWS, t)).astype(BF16)
                e1x_ref[h, i] = jnp.broadcast_to(e1_ref[h, pl.ds(i, 1), :], (_BF16_ROWS, t)).astype(BF16)
            return carry

        lax.fori_loop(0, N_KEYS, expand, 0)

    acc_ref[...] += jnp.dot(vt_ref[...], act_ref[(c + 1) & 1], preferred_element_type=F32)

    chunk = jnp.minimum(c, n_chunks - 1)
    n_i1 = ec // N_KEYS
    slot = c & 1
    rep = N_KEYS // _BF16_ROWS
    for ii in range(n_i1):
        i1 = chunk * n_i1 + ii
        rows = slice(ii * N_KEYS, (ii + 1) * N_KEYS)
        hid = jnp.dot(u_ref[rows, :], xnt_ref[...], preferred_element_type=F32)
        wsum = jnp.zeros((N_KEYS, t), BF16)
        for h in range(PEER_HEADS):
            cut = jnp.concatenate([cutx_ref[h, i1]] * rep, axis=0)
            e1 = jnp.concatenate([e1x_ref[h, i1]] * rep, axis=0)
            wsum = wsum + jnp.where(rank2x_ref[h] < cut, e1 * e2x_ref[h], jnp.zeros((), BF16))
        gelu = 0.5 * hid * (1.0 + lax.erf(hid * (2.0 ** -0.5)))
        act_ref[slot, rows, :] = gelu.astype(BF16) * wsum

    @pl.when(c == n_chunks)
    def _():
        o_ref[...] = h_ref[...] + acc_ref[...].T


def _peer_dense(xnt, h, u, vt, route, t, ec):
    d, n = xnt.shape
    nc = u.shape[0] // ec
    big = pl.BlockSpec((PEER_HEADS, N_KEYS, t), lambda i, c: (0, 0, i))
    return pl.pallas_call(
        functools.partial(_peer_dense_kernel, t=t, ec=ec),
        grid=(n // t, nc + 1),
        in_specs=[
            pl.BlockSpec((d, t), lambda i, c: (0, i)),
            pl.BlockSpec((t, d), lambda i, c: (i, 0)),
            pl.BlockSpec((ec, d), lambda i, c: (jnp.minimum(c, nc - 1), 0)),
            pl.BlockSpec((d, ec), lambda i, c: (0, jnp.maximum(c - 1, 0))),
            big, big, big, big,
        ],
        out_specs=pl.BlockSpec((t, d), lambda i, c: (i, 0)),
        out_shape=SDS((n, d), F32),
        scratch_shapes=[pltpu.VMEM((d, t), F32), pltpu.VMEM((2, ec, t), BF16),
                        pltpu.VMEM((PEER_HEADS, N_KEYS, _BF16_ROWS, t), BF16),
                        pltpu.VMEM((PEER_HEADS, N_KEYS, _BF16_ROWS, t), BF16),
                        pltpu.VMEM((PEER_HEADS, N_KEYS, t), BF16), pltpu.VMEM((PEER_HEADS, N_KEYS, t), BF16)],
        compiler_params=_params("arbitrary", "arbitrary", vmem_mb=56),
        name="peer_dense",
    )(xnt, h, u, vt, *route)


def _final_norm_kernel(x_ref, g_ref, o_ref):
    o_ref[...] = _rms(x_ref[...], g_ref[...])


def _final_norm(x, g, tm):
    n, d = x.shape
    return pl.pallas_call(
        _final_norm_kernel,
        grid=(n // tm,),
        in_specs=[pl.BlockSpec((tm, d), lambda i: (i, 0)), pl.BlockSpec((1, d), lambda i: (0, 0))],
        out_specs=pl.BlockSpec((tm, d), lambda i: (i, 0)),
        out_shape=SDS((n, d), F32),
        compiler_params=_params("arbitrary"),
        name="final_norm",
    )(x, g)


def _tile(n, pref):
    return pref if n % pref == 0 else n


def _layer_weights(l, lbs, norm1_g, w_in, conv_dw, conv_db, conv_ln_g, conv_ln_b, hg_norm_g, att_fb,
                   w_branch, w_out, norm2_g, peer_wq, peer_keys, peer_u, peer_v):
    n_main = 2 * BRANCH_W + 4 * HG_HEADS * HG_DK + 3 * ATT_HEADS * ATT_HD
    w = w_in[l]
    row = lambda a: a.reshape(1, -1).astype(F32)
    return dict(
        g1=row(norm1_g[l]),
        w_main=jnp.concatenate([w[:, n_main + ATT_HEADS:], w[:, :n_main]], axis=1).astype(BF16),
        w_f=jnp.pad(w[:, n_main:n_main + ATT_HEADS], ((0, 0), (0, LANES - ATT_HEADS))).astype(BF16),
        fb=jnp.pad(att_fb[l].astype(F32), (0, LANES - ATT_HEADS)).reshape(1, LANES),
        conv_w=conv_dw[l].astype(F32), conv_b=row(conv_db[l]), ln_g=row(conv_ln_g[l]), ln_b=row(conv_ln_b[l]),
        lb=row(lbs[l]), hg_g=row(hg_norm_g[l]),
        wb=w_branch[l].astype(BF16), wo=w_out[l].astype(BF16), g2=row(norm2_g[l]),
        wqt=peer_wq[l].T.astype(BF16), keys=peer_keys[l].astype(BF16),
        u=peer_u[l].astype(BF16), vt=peer_v[l].T.astype(BF16),
    )


def _token_mixers(x, wt, nb, seq, conv_buf, hg_state, attend):
    n = nb * seq
    z, lf = _in_proj(x, wt["g1"], wt["w_main"], wt["w_f"], wt["fb"], _tile(n, 1024))
    y_conv, conv_new = _conv_branch(z, conv_buf, wt["conv_w"], wt["conv_b"], wt["ln_g"], wt["ln_b"],
                                    nb, seq, _tile(seq, 512))
    y_hgrn, hg_new = _hgrn_branch(z, hg_state, wt["lb"], wt["hg_g"], nb, seq, _tile(seq, 128), _tile(seq, 16))
    y_att = attend(z, lf)
    h, xn = _merge(x, y_conv, y_hgrn, y_att, z, wt["wb"], wt["wo"], wt["g2"], _tile(n, 512))
    tp = _tile(n, 256)
    route = _peer_route(xn, wt["wqt"], wt["keys"], tp)
    out = _peer_dense(xn, h, wt["u"], wt["vt"], route, tp, _PEER_CHUNK)
    w = ATT_HEADS * ATT_HD
    k_rows = z[:, ZB_AK * BRANCH_W:ZB_AK * BRANCH_W + w].reshape(nb, seq, ATT_HEADS, ATT_HD)
    v_rows = z[:, ZB_AV * BRANCH_W:ZB_AV * BRANCH_W + w].reshape(nb, seq, ATT_HEADS, ATT_HD)
    logf = lf[:, :ATT_HEADS].reshape(nb, seq, ATT_HEADS)
    return out, conv_new, hg_new, k_rows, v_rows, logf


def kernel(x_prompt, x_sample, cache_k, cache_v, cache_logf, state_conv, state_hgrn, page_table, norm1_g, w_in, conv_dw, conv_db, conv_ln_g, conv_ln_b, hg_lb_logits, hg_norm_g, att_fb, w_branch, w_out, norm2_g, peer_wq, peer_keys, peer_u, peer_v, final_g):
    depth = w_in.shape[0]
    bp, sp, d = x_prompt.shape
    bs, ss, _ = x_sample.shape
    w = ATT_HEADS * ATT_HD
    probs = jax.nn.softmax(hg_lb_logits.astype(F32), axis=0)
    lbs = jnp.cumsum(probs, axis=0) - probs[0:1]
    n_phys = cache_k.shape[1]
    ckt = jnp.transpose(cache_k, (0, 1, 3, 4, 2)).reshape(depth * n_phys, ATT_HEADS, ATT_HD, PAGE_SIZE)
    cvt = jnp.transpose(cache_v, (0, 1, 3, 4, 2)).reshape(depth * n_phys, ATT_HEADS, ATT_HD, PAGE_SIZE)
    lf_rows = jnp.swapaxes(cache_logf.astype(F32), 2, 3).reshape(depth * n_phys * ATT_HEADS, PAGE_SIZE)
    suffix, total = (a.reshape(depth * n_phys, ATT_HEADS, PAGE_SIZE)
                     for a in _page_suffix(lf_rows, _tile(lf_rows.shape[0], 2048)))

    hp = x_prompt.reshape(bp * sp, d)
    hs = x_sample.reshape(bs * ss, d)
    outs_p, outs_s = [], []
    for l in range(depth):
        wt = _layer_weights(l, lbs, norm1_g, w_in, conv_dw, conv_db, conv_ln_g, conv_ln_b, hg_norm_g, att_fb,
                            w_branch, w_out, norm2_g, peer_wq, peer_keys, peer_u, peer_v)

        def attend_prompt(z, lf):
            ct = _logf_cumsum(lf, bp, sp, _tile(sp, 512))
            return _fox_prompt(z, ct, bp, sp, _tile(sp, _FOX_BLOCK))

        def attend_sample(z, lf, l=l):
            lfn_t = jnp.swapaxes(lf[:, :ATT_HEADS].reshape(bs, ss, ATT_HEADS), 1, 2)
            lfn_t = jnp.pad(lfn_t, ((0, 0), (0, 0), (0, PAGE_SIZE - ss)))
            return _fox_sample(z, lfn_t, ckt, cvt, suffix, total, page_table, l * n_phys, bs, ss)

        hp, *rp = _token_mixers(hp, wt, bp, sp, jnp.zeros((bp, CONV_K - 1, BRANCH_W), F32),
                                jnp.zeros((bp, HG_HEADS, HG_DK, HG_DK), F32), attend_prompt)
        hs, *rs = _token_mixers(hs, wt, bs, ss, state_conv[l], state_hgrn[l], attend_sample)
        outs_p.append(rp)
        outs_s.append(rs)

    fg = final_g.reshape(1, d).astype(F32)
    y_prompt = _final_norm(hp, fg, _tile(bp * sp, 1024)).reshape(bp, sp, d)
    y_sample = _final_norm(hs, fg, _tile(bs * ss, 1024)).reshape(bs, ss, d)
    stack = lambda outs, i: jnp.stack([r[i] for r in outs])
    return (y_prompt, y_sample,
            stack(outs_p, 2), stack(outs_p, 3), stack(outs_p, 4), stack(outs_p, 0), stack(outs_p, 1),
            stack(outs_s, 2), stack(outs_s, 3), stack(outs_s, 4), stack(outs_s, 0), stack(outs_s, 1))
```

```python
import functools

import jax
import jax.numpy as jnp
from jax import lax
from jax.experimental import pallas as pl
from jax.experimental.pallas import tpu as pltpu

F32 = jnp.float32
BF16 = jnp.bfloat16
HI = lax.Precision.HIGHEST
SDS = jax.ShapeDtypeStruct

D_MODEL = 1024
BRANCH_W = 512
CONV_K = 31
HG_HEADS = 4
HG_DK = 128
ATT_HEADS = 8
ATT_HD = 64
PEER_HEADS = 8
N_KEYS = 128
PEER_TOPK = 16
PAGE_SIZE = 128
EPS = 1e-6
GATE_FLOOR = 1e-20
MASK_VALUE = -1e30
LANES = 128
_HD_SHIFT = ATT_HD.bit_length() - 1
_HEAD_SHIFT = ATT_HEADS.bit_length() - 1

Z_COLS = 15 * BRANCH_W
ZB_CONV = 3
ZB_HQ, ZB_HF, ZB_HI, ZB_HG, ZB_AQ, ZB_AK, ZB_AV = 8, 9, 10, 11, 12, 13, 14

_PAIRS = tuple((a, b) for a in range(PEER_TOPK) for b in range(PEER_TOPK) if (a + 1) * (b + 1) <= PEER_TOPK)

_NT = (((1,), (1,)), ((), ()))
_TN = (((0,), (0,)), ((), ()))


def _params(*sem, vmem_mb=48):
    return pltpu.CompilerParams(dimension_semantics=sem, vmem_limit_bytes=vmem_mb * 1024 * 1024)


def _sigmoid(x):
    return 1.0 / (1.0 + jnp.exp(-x))


def _log_sigmoid(x):
    return jnp.minimum(x, 0.0) - jnp.log1p(jnp.exp(-jnp.abs(x)))


def _rms(x, g):
    return x * lax.rsqrt(jnp.mean(x * x, axis=-1, keepdims=True) + EPS) * g


def _in_proj_kernel(x_ref, g_ref, w_ref, wf_ref, fb_ref, z_ref, lf_ref, xn_ref):
    @pl.when(pl.program_id(1) == 0)
    def _():
        xn = _rms(x_ref[...], g_ref[...]).astype(BF16)
        xn_ref[...] = xn
        af = jnp.dot(xn, wf_ref[...], preferred_element_type=F32) + fb_ref[...]
        lf_ref[...] = _log_sigmoid(af)

    z_ref[...] = jnp.dot(xn_ref[...], w_ref[...], preferred_element_type=F32)


def _in_proj(x, g, w_main, w_f, fb, tm):
    n, d = x.shape
    tn = 3 * BRANCH_W
    return pl.pallas_call(
        _in_proj_kernel,
        grid=(n // tm, Z_COLS // tn),
        in_specs=[
            pl.BlockSpec((tm, d), lambda i, j: (i, 0)),
            pl.BlockSpec((1, d), lambda i, j: (0, 0)),
            pl.BlockSpec((d, tn), lambda i, j: (0, j)),
            pl.BlockSpec((d, LANES), lambda i, j: (0, 0)),
            pl.BlockSpec((1, LANES), lambda i, j: (0, 0)),
        ],
        out_specs=[
            pl.BlockSpec((tm, tn), lambda i, j: (i, j)),
            pl.BlockSpec((tm, LANES), lambda i, j: (i, 0)),
        ],
        out_shape=[SDS((n, Z_COLS), F32), SDS((n, LANES), F32)],
        scratch_shapes=[pltpu.VMEM((tm, d), BF16)],
        compiler_params=_params("arbitrary", "arbitrary"),
        name="in_proj",
    )(x, g, w_main, w_f, fb)


_CONV_PAD = 32


def _conv_kernel(a_ref, buf_ref, w_ref, b_ref, g_ref, bt_ref, y_ref, nc_ref, ext_ref, *, tl):
    hist = CONV_K - 1
    lo = _CONV_PAD - hist

    @pl.when(pl.program_id(1) == 0)
    def _():
        ext_ref[lo:_CONV_PAD, :] = buf_ref[0]

    a = a_ref[...]
    ext_ref[_CONV_PAD:_CONV_PAD + tl, :] = a[:, :BRANCH_W] * _sigmoid(a[:, BRANCH_W:])
    acc = jnp.zeros((tl, BRANCH_W), F32)
    for j in range(CONV_K):
        acc = acc + w_ref[j:j + 1, :] * ext_ref[lo + j:lo + j + tl, :]
    y = acc + b_ref[...]
    mu = jnp.mean(y, axis=-1, keepdims=True)
    yc = y - mu
    var = jnp.mean(yc * yc, axis=-1, keepdims=True)
    y = yc * lax.rsqrt(var + EPS) * g_ref[...] + bt_ref[...]
    y_ref[...] = y * _sigmoid(y)
    tail = ext_ref[tl + lo:tl + _CONV_PAD, :]
    nc_ref[0] = tail
    ext_ref[lo:_CONV_PAD, :] = tail


def _conv_branch(z, buf, w, b, g, bt, nb, seq, tl):
    nt = seq // tl
    hist = CONV_K - 1
    vec = pl.BlockSpec((1, BRANCH_W), lambda i, t: (0, 0))
    return pl.pallas_call(
        functools.partial(_conv_kernel, tl=tl),
        grid=(nb, nt),
        in_specs=[
            pl.BlockSpec((tl, 2 * BRANCH_W), lambda i, t: (i * nt + t, ZB_CONV)),
            pl.BlockSpec((1, hist, BRANCH_W), lambda i, t: (i, 0, 0)),
            pl.BlockSpec((CONV_K, BRANCH_W), lambda i, t: (0, 0)),
            vec, vec, vec,
        ],
        out_specs=[
            pl.BlockSpec((tl, BRANCH_W), lambda i, t: (i * nt + t, 0)),
            pl.BlockSpec((1, hist, BRANCH_W), lambda i, t: (i, 0, 0)),
        ],
        out_shape=[SDS((nb * seq, BRANCH_W), F32), SDS((nb, hist, BRANCH_W), F32)],
        scratch_shapes=[pltpu.VMEM((_CONV_PAD + tl, BRANCH_W), F32)],
        compiler_params=_params("arbitrary", "arbitrary"),
        name="conv_branch",
    )(z, buf, w, b, g, bt)


def _hgrn_kernel(q_ref, f_ref, i_ref, g_ref, s0_ref, lb_ref, ng_ref, y_ref, sn_ref,
                 st_ref, qs_ref, ks_ref, bs_ref, *, tl, c):
    t = pl.program_id(1)

    @pl.when(t == 0)
    def _():
        for h in range(HG_HEADS):
            st_ref[h] = s0_ref[0, h].T

    lb = jnp.clip(lb_ref[...], 0.0, 1.0)
    f = lb + (1.0 - lb) * _sigmoid(f_ref[...])
    lf = jnp.log(jnp.maximum(f, GATE_FLOOR))
    q = q_ref[...]
    qs_ref[...] = q * _sigmoid(q)
    ks_ref[...] = 1.0 - f
    r = lax.broadcasted_iota(jnp.int32, (tl, tl), 0)
    cc = lax.broadcasted_iota(jnp.int32, (tl, tl), 1)
    tri = jnp.where(cc <= r, jnp.where(cc >= (r & -c), 1.0, 0.0), 0.0).astype(F32)
    bs_ref[...] = jnp.dot(tri, lf, precision=HI, preferred_element_type=F32)
    rowid = lax.broadcasted_iota(jnp.int32, (c, 1), 0)

    def chunk(ci, carry):
        off = pl.multiple_of(ci * c, c)
        for h in range(HG_HEADS):
            sl = slice(h * HG_DK, (h + 1) * HG_DK)
            qc = qs_ref[pl.ds(off, c), sl]
            kc = ks_ref[pl.ds(off, c), sl]
            bc = bs_ref[pl.ds(off, c), sl]
            vc = i_ref[pl.ds(off, c), sl]
            o = jnp.zeros((c, HG_DK), F32)
            for s in range(c):
                k_row = kc[s:s + 1, :]
                b_row = bc[s:s + 1, :]
                v_row = vc[s:s + 1, :]
                p = qc * k_row * jnp.exp(jnp.minimum(bc - b_row, 0.0))
                a = jnp.sum(p, axis=1, keepdims=True)
                o = o + jnp.where(rowid >= s, a, 0.0) * v_row
            st = st_ref[h]
            qt = (qc * jnp.exp(bc)).astype(BF16)
            o = o + lax.dot_general(qt, st.astype(BF16), _NT, preferred_element_type=F32)
            b_last = bc[c - 1:c, :]
            kt = (kc * jnp.exp(b_last - bc)).astype(BF16)
            upd = lax.dot_general(vc.astype(BF16), kt, _TN, preferred_element_type=F32)
            st_ref[h] = st * jnp.exp(b_last) + upd
            o = _rms(o, ng_ref[...])
            y_ref[pl.ds(off, c), sl] = o * _sigmoid(g_ref[pl.ds(off, c), sl])
        return carry

    lax.fori_loop(0, tl // c, chunk, 0)

    @pl.when(t == pl.num_programs(1) - 1)
    def _():
        for h in range(HG_HEADS):
            sn_ref[0, h] = st_ref[h].T


def _hgrn_branch(z, s0, lb, ng, nb, seq, tl, c):
    nt = seq // tl
    w = HG_HEADS * HG_DK

    def col(j):
        return pl.BlockSpec((tl, w), lambda i, t: (i * nt + t, j))

    st_spec = pl.BlockSpec((1, HG_HEADS, HG_DK, HG_DK), lambda i, t: (i, 0, 0, 0))
    return pl.pallas_call(
        functools.partial(_hgrn_kernel, tl=tl, c=c),
        grid=(nb, nt),
        in_specs=[col(ZB_HQ), col(ZB_HF), col(ZB_HI), col(ZB_HG), st_spec,
                  pl.BlockSpec((1, w), lambda i, t: (0, 0)),
                  pl.BlockSpec((1, HG_DK), lambda i, t: (0, 0))],
        out_specs=[pl.BlockSpec((tl, w), lambda i, t: (i * nt + t, 0)), st_spec],
        out_shape=[SDS((nb * seq, w), F32), SDS((nb, HG_HEADS, HG_DK, HG_DK), F32)],
        scratch_shapes=[pltpu.VMEM((HG_HEADS, HG_DK, HG_DK), F32),
                        pltpu.VMEM((tl, w), F32), pltpu.VMEM((tl, w), F32), pltpu.VMEM((tl, w), F32)],
        compiler_params=_params("arbitrary", "arbitrary"),
        name="hgrn_branch",
    )(z, z, z, z, s0, lb, ng)


def _cumsum_kernel(lf_ref, ct_ref, carry_ref, *, tl):
    @pl.when(pl.program_id(1) == 0)
    def _():
        carry_ref[...] = jnp.zeros_like(carry_ref)

    r = lax.broadcasted_iota(jnp.int32, (tl, tl), 0)
    cc = lax.broadcasted_iota(jnp.int32, (tl, tl), 1)
    tri = jnp.where(cc <= r, 1.0, 0.0).astype(F32)
    cs = jnp.dot(tri, lf_ref[...], precision=HI, preferred_element_type=F32) + carry_ref[...]
    carry_ref[...] = cs[tl - 1:tl, :]
    ct_ref[0] = cs.T[:ATT_HEADS, :]


def _logf_cumsum(lf, nb, seq, tl):
    nt = seq // tl
    return pl.pallas_call(
        functools.partial(_cumsum_kernel, tl=tl),
        grid=(nb, nt),
        in_specs=[pl.BlockSpec((tl, LANES), lambda i, t: (i * nt + t, 0))],
        out_specs=pl.BlockSpec((1, ATT_HEADS, tl), lambda i, t: (i, 0, t)),
        out_shape=SDS((nb, ATT_HEADS, seq), F32),
        scratch_shapes=[pltpu.VMEM((1, LANES), F32)],
        compiler_params=_params("arbitrary", "arbitrary"),
        name="logf_cumsum",
    )(lf)


_LOG2E = 1.4426950408889634
_FOX_BLOCK = 1024
_FOX_SUB = 512


def _fox_kernel(q_ref, k_ref, v_ref, ck_ref, cq_ref, o_ref, qm_ref, m_ref, acc_ref, *, t):
    pair = pl.program_id(1)
    qi = pl.program_id(2)
    ki = pl.program_id(3)
    lane = lax.broadcasted_iota(jnp.int32, (1, LANES), 1)

    @pl.when(ki == 0)
    def _():
        q = q_ref[...] * (ATT_HD ** -0.5 * _LOG2E)
        qm_ref[0] = jnp.where(lane < ATT_HD, q, 0.0).astype(BF16)
        qm_ref[1] = jnp.where(lane >= ATT_HD, q, 0.0).astype(BF16)
        m_ref[...] = jnp.full_like(m_ref, MASK_VALUE)
        acc_ref[...] = jnp.zeros_like(acc_ref)

    def step(diagonal):
        k = k_ref[...].astype(BF16)
        v = jnp.concatenate([v_ref[...].astype(BF16), jnp.ones((t, LANES), BF16)], axis=1)
        head_iota = lax.broadcasted_iota(jnp.int32, (ATT_HEADS, 1), 0)
        sub = min(t, _FOX_SUB)
        biases, states = [], []
        for j in range(2):
            own = head_iota == 2 * pair + j
            ck = jnp.sum(jnp.where(own, ck_ref[0], 0.0), axis=0, keepdims=True)
            c0 = jnp.sum(jnp.where(own, cq_ref[0, :, 0:1], 0.0), axis=0, keepdims=True)
            biases.append((c0 - ck) * _LOG2E)
            states.append((qm_ref[j], m_ref[j], acc_ref[j]))

        def logits(item):
            j, r0 = item
            nk = r0 + sub if diagonal else t
            s = lax.dot_general(states[j][0][r0:r0 + sub], k[:nk], _NT, preferred_element_type=F32)
            s = s + biases[j][:, :nk]
            if diagonal:
                rid = lax.broadcasted_iota(jnp.int32, (sub, nk), 0) + r0
                cid = lax.broadcasted_iota(jnp.int32, (sub, nk), 1)
                s = jnp.where(cid <= rid, s, MASK_VALUE)
            return s, nk

        items = [(j, r0) for j in range(2) for r0 in range(0, t, sub)]
        m_out, acc_out = ([], []), ([], [])
        nxt = logits(items[0])
        for idx, (j, r0) in enumerate(items):
            s, nk = nxt
            if idx + 1 < len(items):
                nxt = logits(items[idx + 1])
            rows = slice(r0, r0 + sub)
            m_old = states[j][1][rows]
            m_new = jnp.maximum(m_old, jnp.max(s, axis=1, keepdims=True))
            alpha = jnp.exp2(m_old - m_new)
            p = jnp.exp2(s - m_new[:, 0:1]).astype(BF16)
            pv = jnp.dot(p, v[:nk], preferred_element_type=F32)
            acc_out[j].append(jnp.concatenate([alpha, alpha], axis=1) * states[j][2][rows] + pv)
            m_out[j].append(m_new)
        for j in range(2):
            acc_ref[j] = jnp.concatenate(acc_out[j], axis=0)
            m_ref[j] = jnp.concatenate(m_out[j], axis=0)

    @pl.when(ki < qi)
    def _():
        step(False)

    @pl.when(ki == qi)
    def _():
        step(True)
        o0 = acc_ref[0, :, :LANES] / acc_ref[0, :, LANES:]
        o1 = acc_ref[1, :, :LANES] / acc_ref[1, :, LANES:]
        o_ref[...] = jnp.where(lane < ATT_HD, o0, o1)


def _fox_prompt(z, ct, nb, seq, t):
    nq = seq // t
    npair = ATT_HEADS // 2
    zq, zk, zv = (zb * (BRANCH_W // LANES) for zb in (ZB_AQ, ZB_AK, ZB_AV))
    return pl.pallas_call(
        functools.partial(_fox_kernel, t=t),
        grid=(nb, npair, nq, nq),
        in_specs=[
            pl.BlockSpec((t, LANES), lambda b, p, qi, ki: (b * nq + qi, zq + p)),
            pl.BlockSpec((t, LANES), lambda b, p, qi, ki: (b * nq + jnp.minimum(ki, qi), zk + p)),
            pl.BlockSpec((t, LANES), lambda b, p, qi, ki: (b * nq + jnp.minimum(ki, qi), zv + p)),
            pl.BlockSpec((1, ATT_HEADS, t), lambda b, p, qi, ki: (b, 0, jnp.minimum(ki, qi))),
            pl.BlockSpec((1, ATT_HEADS, t), lambda b, p, qi, ki: (b, 0, qi)),
        ],
        out_specs=pl.BlockSpec((t, LANES), lambda b, p, qi, ki: (b * nq + qi, p)),
        out_shape=SDS((nb * seq, BRANCH_W), F32),
        scratch_shapes=[pltpu.VMEM((2, t, LANES), BF16), pltpu.VMEM((2, t, LANES), F32),
                        pltpu.VMEM((2, t, 2 * LANES), F32)],
        compiler_params=_params("arbitrary", "arbitrary", "arbitrary", "arbitrary"),
        name="fox_prompt",
    )(z, z, z, ct, ct)


def _page_suffix_kernel(lf_ref, suf_ref, tot_ref):
    r = lax.broadcasted_iota(jnp.int32, (PAGE_SIZE, PAGE_SIZE), 0)
    c = lax.broadcasted_iota(jnp.int32, (PAGE_SIZE, PAGE_SIZE), 1)
    lf = lf_ref[...]
    suf_ref[...] = jnp.dot(lf, jnp.where(r > c, 1.0, 0.0).astype(F32), precision=HI, preferred_element_type=F32)
    tot_ref[...] = jnp.dot(lf, jnp.ones((PAGE_SIZE, PAGE_SIZE), F32), precision=HI, preferred_element_type=F32)


def _page_suffix(lf_rows, tr):
    n = lf_rows.shape[0]
    spec = pl.BlockSpec((tr, PAGE_SIZE), lambda i: (i, 0))
    return pl.pallas_call(
        _page_suffix_kernel,
        grid=(n // tr,),
        in_specs=[spec],
        out_specs=[spec, spec],
        out_shape=[SDS((n, PAGE_SIZE), F32), SDS((n, PAGE_SIZE), F32)],
        compiler_params=_params("arbitrary"),
        name="page_suffix",
    )(lf_rows)


_PAGES_PER_STEP = 8


def _fox_sample_kernel(pt_ref, q_ref, kn_ref, vn_ref, lfn_ref, *rest, g, tt):
    del pt_ref
    kp_refs, vp_refs, sf_refs, tot_refs = rest[:g], rest[g:2 * g], rest[2 * g:3 * g], rest[3 * g:4 * g]
    o_ref, qb_ref, m_ref, l_ref, acc_ref, car_ref, cn_ref = rest[4 * g:]
    step = pl.program_id(1)
    rows = tt * ATT_HEADS
    w = ATT_HEADS * ATT_HD
    rid = lax.broadcasted_iota(jnp.int32, (rows, 1), 0)
    hmask = (lax.broadcasted_iota(jnp.int32, (rows, w), 1) >> _HD_SHIFT) == (rid & (ATT_HEADS - 1))
    tok = rid >> _HEAD_SHIFT
    lane = lax.broadcasted_iota(jnp.int32, (rows, PAGE_SIZE), 1)

    def tile_rows(x):
        return jnp.broadcast_to(x[None], (tt,) + x.shape).reshape(rows, x.shape[-1])

    def online(scores, values, dims):
        m_old = m_ref[...]
        m_new = m_old
        for s in scores:
            m_new = jnp.maximum(m_new, jnp.max(s, axis=1, keepdims=True))
        alpha = jnp.exp(m_old - m_new)
        l_new = alpha * l_ref[...]
        acc = alpha * acc_ref[...]
        for s, v in zip(scores, values):
            p = jnp.exp(s - m_new)
            l_new = l_new + jnp.sum(p, axis=1, keepdims=True)
            acc = acc + lax.dot_general(p.astype(BF16), v, dims, preferred_element_type=F32)
        l_ref[...] = l_new
        acc_ref[...] = acc
        m_ref[...] = m_new

    @pl.when(step == 0)
    def _():
        q = q_ref[...] * (ATT_HD ** -0.5)
        qrep = jnp.broadcast_to(q[:, None, :], (tt, ATT_HEADS, w)).reshape(rows, w)
        qb_ref[...] = jnp.where(hmask, qrep, 0.0).astype(BF16)
        m_ref[...] = jnp.full_like(m_ref, MASK_VALUE)
        l_ref[...] = jnp.zeros_like(l_ref)
        acc_ref[...] = jnp.zeros_like(acc_ref)
        car_ref[...] = jnp.zeros_like(car_ref)
        r2 = lax.broadcasted_iota(jnp.int32, (PAGE_SIZE, PAGE_SIZE), 0)
        c2 = lax.broadcasted_iota(jnp.int32, (PAGE_SIZE, PAGE_SIZE), 1)
        incl = jnp.where(r2 <= c2, 1.0, 0.0).astype(F32)
        cn_ref[...] = jnp.dot(tile_rows(lfn_ref[0]), incl, precision=HI, preferred_element_type=F32)

    qb = qb_ref[...]
    cn = cn_ref[...]
    cn_own = jnp.sum(jnp.where(lane == tok, cn, 0.0), axis=1, keepdims=True)
    car = car_ref[...]
    scores = []
    for i in range(g):
        kt = kp_refs[i][0].reshape(w, PAGE_SIZE).astype(BF16)
        s = jnp.dot(qb, kt, preferred_element_type=F32)
        scores.append(s + ((cn_own + car) + tile_rows(sf_refs[i][0])))
        car = car + tile_rows(tot_refs[i][0])[:, 0:1]
    car_ref[...] = car
    online(scores, [vp_refs[i][0].reshape(w, PAGE_SIZE).astype(BF16) for i in range(g)], _NT)

    @pl.when(step == pl.num_programs(1) - 1)
    def _():
        pad = jnp.zeros((PAGE_SIZE - tt, w), F32)
        kn = jnp.concatenate([kn_ref[...], pad], axis=0).astype(BF16)
        vn = jnp.concatenate([vn_ref[...], pad], axis=0).astype(BF16)
        sn = lax.dot_general(qb, kn, _NT, preferred_element_type=F32)
        sn = jnp.where(lane <= tok, sn + (cn_own - cn), MASK_VALUE)
        online([sn], [vn], (((1,), (0,)), ((), ())))
        o = jnp.where(hmask, acc_ref[...] / l_ref[...], 0.0)
        o_ref[...] = jnp.sum(o.reshape(tt, ATT_HEADS, w), axis=1)


def _fox_sample(z, lfn_t, ckt, cvt, suffix, total, page_table, page_base, nb, tt):
    npg = page_table.shape[1]
    g = _PAGES_PER_STEP
    assert npg % g == 0
    w = ATT_HEADS * ATT_HD
    rows = tt * ATT_HEADS

    def page(i, ndim):
        def index(b, p, pt):
            return (page_base + pt[b * npg + (npg - 1 - (p * g + i))],) + (0,) * (ndim - 1)
        return index

    in_specs = [pl.BlockSpec((tt, w), lambda b, p, pt, zb=zb: (b, zb)) for zb in (ZB_AQ, ZB_AK, ZB_AV)]
    in_specs += [pl.BlockSpec((1, ATT_HEADS, PAGE_SIZE), lambda b, p, pt: (b, 0, 0))]
    in_specs += [pl.BlockSpec((1, ATT_HEADS, ATT_HD, PAGE_SIZE), page(i, 4)) for i in range(g)] * 2
    in_specs += [pl.BlockSpec((1, ATT_HEADS, PAGE_SIZE), page(i, 3)) for i in range(g)] * 2
    grid_spec = pltpu.PrefetchScalarGridSpec(
        num_scalar_prefetch=1,
        grid=(nb, npg // g),
        in_specs=in_specs,
        out_specs=pl.BlockSpec((tt, w), lambda b, p, pt: (b, 0)),
        scratch_shapes=[pltpu.VMEM((rows, w), BF16), pltpu.VMEM((rows, 1), F32), pltpu.VMEM((rows, 1), F32),
                        pltpu.VMEM((rows, w), F32), pltpu.VMEM((rows, 1), F32), pltpu.VMEM((rows, PAGE_SIZE), F32)],
    )
    return pl.pallas_call(
        functools.partial(_fox_sample_kernel, g=g, tt=tt),
        grid_spec=grid_spec,
        out_shape=SDS((nb * tt, w), F32),
        compiler_params=_params("arbitrary", "arbitrary"),
        name="fox_sample",
    )(page_table.reshape(-1), z, z, z, lfn_t, *([ckt] * g), *([cvt] * g), *([suffix] * g), *([total] * g))


def _merge_kernel(x_ref, yc_ref, yh_ref, ya_ref, gt_ref, wb_ref, wo_ref, g2_ref, h_ref, xnt_ref):
    merged = jnp.zeros(x_ref.shape, F32)
    for b, y_ref in enumerate((yc_ref, yh_ref, ya_ref)):
        proj = jnp.dot(y_ref[...].astype(BF16), wb_ref[b], preferred_element_type=F32)
        merged = merged + _sigmoid(gt_ref[:, b * D_MODEL:(b + 1) * D_MODEL]) * proj
    h = x_ref[...] + jnp.dot(merged.astype(BF16), wo_ref[...], preferred_element_type=F32)
    h_ref[...] = h
    xnt = _rms(h, g2_ref[...]).T.astype(BF16)
    for k in range(xnt_ref.shape[0]):
        xnt_ref[k] = xnt[:, k * _PEER_T:(k + 1) * _PEER_T]


_PEER_T = 256


def _merge(x, yc, yh, ya, z, wb, wo, g2, tm):
    n, d = x.shape
    assert tm % _PEER_T == 0
    br = pl.BlockSpec((tm, BRANCH_W), lambda i: (i, 0))
    return pl.pallas_call(
        _merge_kernel,
        grid=(n // tm,),
        in_specs=[
            pl.BlockSpec((tm, d), lambda i: (i, 0)), br, br, br,
            pl.BlockSpec((tm, 3 * d), lambda i: (i, 0)),
            pl.BlockSpec((3, BRANCH_W, d), lambda i: (0, 0, 0)),
            pl.BlockSpec((d, d), lambda i: (0, 0)),
            pl.BlockSpec((1, d), lambda i: (0, 0)),
        ],
        out_specs=[pl.BlockSpec((tm, d), lambda i: (i, 0)),
                   pl.BlockSpec((tm // _PEER_T, d, _PEER_T), lambda i: (i, 0, 0))],
        out_shape=[SDS((n, d), F32), SDS((n // _PEER_T, d, _PEER_T), BF16)],
        compiler_params=_params("arbitrary"),
        name="merge",
    )(x, yc, yh, ya, z, wb, wo, g2)


def _take_max(work, row_iota, n_rows):
    m = jnp.max(work, axis=0, keepdims=True)
    first = jnp.min(jnp.where(work == m, row_iota, float(n_rows)), axis=0, keepdims=True)
    return m, jnp.where(row_iota == first, -jnp.inf, work)


def _oddeven_merge_sort_pairs(n):
    pairs, p = [], 1
    while p < n:
        k = p
        while k >= 1:
            for j in range(k % p, n - k, 2 * k):
                for i in range(min(k, n - j - k)):
                    if (i + j) // (2 * p) == (i + j + k) // (2 * p):
                        pairs.append((i + j, i + j + k))
            k //= 2
        p *= 2
    return tuple(pairs)


_SUBLANES = 8
_SORT_PAIRS = _oddeven_merge_sort_pairs(PEER_TOPK)


def _exchange(v, a, b):
    v[a], v[b] = jnp.maximum(v[a], v[b]), jnp.minimum(v[a], v[b])


def _top16_sorted(s):
    v = [s[_SUBLANES * i:_SUBLANES * (i + 1), :] for i in range(PEER_TOPK)]
    for a, b in _SORT_PAIRS:
        _exchange(v, a, b)
    for shift in (4, 6, 7):
        w = [pltpu.roll(x, shift, 0) for x in v]
        v = [jnp.maximum(v[k], w[PEER_TOPK - 1 - k]) for k in range(PEER_TOPK)]
        d = PEER_TOPK // 2
        while d >= 1:
            for i in range(PEER_TOPK):
                if not i & d:
                    _exchange(v, i, i + d)
            d //= 2
    return [x[0:1, :] for x in v]


_N_CAND = -(-len(_PAIRS) // 8) * 8


def _peer_route_kernel(xnt_ref, wqt_ref, keys_ref, cut1_ref, e1_ref, rank2_ref, e2_ref, cand_ref, *, t):
    xnt = xnt_ref[0]
    n_cand = _N_CAND
    cand_iota = lax.broadcasted_iota(jnp.int32, (n_cand, t), 0).astype(F32)
    cand_ref[len(_PAIRS):, :] = jnp.full((n_cand - len(_PAIRS), t), -jnp.inf, F32)

    def head(h, carry):
        dk = 2 * N_KEYS
        qt = jnp.dot(wqt_ref[pl.ds(pl.multiple_of(h * dk, dk), dk), :], xnt, preferred_element_type=F32)
        scores, tops = [], []
        for p in range(2):
            s = jnp.dot(keys_ref[h, p], qt[p * N_KEYS:(p + 1) * N_KEYS].astype(BF16), preferred_element_type=F32)
            scores.append(s)
            tops.append(_top16_sorted(s))
        for i, (a, b) in enumerate(_PAIRS):
            cand_ref[i:i + 1, :] = tops[0][a] + tops[1][b]
        work = cand_ref[...]
        best = []
        for _ in range(PEER_TOPK):
            m, work = _take_max(work, cand_iota, n_cand)
            best.append(m)
        norm = jnp.zeros((1, t), F32)
        for m in best:
            norm = norm + jnp.exp(m - best[0])
        cut = jnp.zeros((N_KEYS, t), F32)
        rank = jnp.full((N_KEYS, t), float(PEER_TOPK), F32)
        for b in range(PEER_TOPK):
            cut = jnp.where(scores[0] + tops[1][b] >= best[-1], float(b + 1), cut)
            rb = PEER_TOPK - 1 - b
            rank = jnp.where(scores[1] >= tops[1][rb], float(rb), rank)
        cut1_ref[0, h] = cut
        e1_ref[0, h] = jnp.exp(scores[0] - tops[0][0]) / norm
        rank2_ref[0, h] = rank.astype(BF16)
        e2_ref[0, h] = jnp.exp(scores[1] - tops[1][0]).astype(BF16)
        return carry

    lax.fori_loop(0, PEER_HEADS, head, 0)


def _peer_route(xnt, wqt, keys):
    nt, d, t = xnt.shape
    big = pl.BlockSpec((1, PEER_HEADS, N_KEYS, t), lambda i: (i, 0, 0, 0))
    wide, narrow = SDS((nt, PEER_HEADS, N_KEYS, t), F32), SDS((nt, PEER_HEADS, N_KEYS, t), BF16)
    return pl.pallas_call(
        functools.partial(_peer_route_kernel, t=t),
        grid=(nt,),
        in_specs=[
            pl.BlockSpec((1, d, t), lambda i: (i, 0, 0)),
            pl.BlockSpec(wqt.shape, lambda i: (0, 0)),
            pl.BlockSpec(keys.shape, lambda i: (0, 0, 0, 0)),
        ],
        out_specs=[big, big, big, big],
        out_shape=[wide, wide, narrow, narrow],
        scratch_shapes=[pltpu.VMEM((_N_CAND, t), F32)],
        compiler_params=_params("arbitrary"),
        name="peer_route",
    )(xnt, wqt, keys)


_BF16_ROWS = 16
_PEER_CHUNK = 2048


def _peer_dense_kernel(xnt_ref, h_ref, u_ref, vt_ref, cut1_ref, e1_ref, rank2_ref, e2_ref, o_ref,
                       acc_ref, act_ref, cutx_ref, e1x_ref, rank2x_ref, e2x_ref, *, t, ec):
    c = pl.program_id(1)
    tile = pl.program_id(2)
    n_chunks = pl.num_programs(1) - 1

    @pl.when(c == 0)
    def _():
        acc_ref[tile] = jnp.zeros(acc_ref.shape[1:], F32)
        act_ref[tile, 1] = jnp.zeros((ec, t), BF16)
        rank2x_ref[tile] = rank2_ref[tile]
        e2x_ref[tile] = e2_ref[tile]

    acc_ref[tile] += jnp.dot(vt_ref[...], act_ref[tile, (c + 1) & 1], preferred_element_type=F32)

    chunk = jnp.minimum(c, n_chunks - 1)
    n_i1 = ec // N_KEYS
    slot = c & 1
    rep = N_KEYS // _BF16_ROWS
    for ii in range(n_i1):
        row = pl.ds(chunk * n_i1 + ii, 1)
        for h in range(PEER_HEADS):
            cutx_ref[h, ii] = jnp.broadcast_to(cut1_ref[tile, h, row, :], (_BF16_ROWS, t)).astype(BF16)
            e1x_ref[h, ii] = jnp.broadcast_to(e1_ref[tile, h, row, :], (_BF16_ROWS, t)).astype(BF16)
    xnt = xnt_ref[tile]
    for ii in range(n_i1):
        rows = slice(ii * N_KEYS, (ii + 1) * N_KEYS)
        hid = jnp.dot(u_ref[rows, :], xnt, preferred_element_type=F32)
        wsum = jnp.zeros((N_KEYS, t), BF16)
        for h in range(PEER_HEADS):
            cut = jnp.concatenate([cutx_ref[h, ii]] * rep, axis=0)
            e1 = jnp.concatenate([e1x_ref[h, ii]] * rep, axis=0)
            wsum = wsum + jnp.where(rank2x_ref[tile, h] < cut, e1 * e2x_ref[tile, h], jnp.zeros((), BF16))
        gelu = 0.5 * hid * (1.0 + lax.erf(hid * (2.0 ** -0.5)))
        act_ref[tile, slot, rows, :] = gelu.astype(BF16) * wsum

    @pl.when(c == n_chunks)
    def _():
        tok = pl.ds(pl.multiple_of(tile * t, t), t)
        o_ref[tok, :] = h_ref[tok, :] + acc_ref[tile].T


_PEER_GROUP = 2


def _peer_dense(xnt, h, u, vt, route, ec):
    nt, d, t = xnt.shape
    g = _PEER_GROUP if nt % _PEER_GROUP == 0 else 1
    nc = u.shape[0] // ec
    n_i1 = ec // N_KEYS
    big = pl.BlockSpec((g, PEER_HEADS, N_KEYS, t), lambda i, c, k: (i, 0, 0, 0))
    tokens = pl.BlockSpec((g * t, d), lambda i, c, k: (i, 0))
    return pl.pallas_call(
        functools.partial(_peer_dense_kernel, t=t, ec=ec),
        grid=(nt // g, nc + 1, g),
        in_specs=[
            pl.BlockSpec((g, d, t), lambda i, c, k: (i, 0, 0)),
            tokens,
            pl.BlockSpec((ec, d), lambda i, c, k: (jnp.minimum(c, nc - 1), 0)),
            pl.BlockSpec((d, ec), lambda i, c, k: (0, jnp.maximum(c - 1, 0))),
            big, big, big, big,
        ],
        out_specs=tokens,
        out_shape=SDS((nt * t, d), F32),
        scratch_shapes=[pltpu.VMEM((g, d, t), F32), pltpu.VMEM((g, 2, ec, t), BF16),
                        pltpu.VMEM((PEER_HEADS, n_i1, _BF16_ROWS, t), BF16),
                        pltpu.VMEM((PEER_HEADS, n_i1, _BF16_ROWS, t), BF16),
                        pltpu.VMEM((g, PEER_HEADS, N_KEYS, t), BF16), pltpu.VMEM((g, PEER_HEADS, N_KEYS, t), BF16)],
        compiler_params=_params("arbitrary", "arbitrary", "arbitrary", vmem_mb=56),
        name="peer_dense",
    )(xnt, h, u, vt, *route)


def _final_norm_kernel(x_ref, g_ref, o_ref):
    o_ref[...] = _rms(x_ref[...], g_ref[...])


def _final_norm(x, g, tm):
    n, d = x.shape
    return pl.pallas_call(
        _final_norm_kernel,
        grid=(n // tm,),
        in_specs=[pl.BlockSpec((tm, d), lambda i: (i, 0)), pl.BlockSpec((1, d), lambda i: (0, 0))],
        out_specs=pl.BlockSpec((tm, d), lambda i: (i, 0)),
        out_shape=SDS((n, d), F32),
        compiler_params=_params("arbitrary"),
        name="final_norm",
    )(x, g)


def _tile(n, pref):
    return pref if n % pref == 0 else n


def _layer_weights(l, lbs, norm1_g, w_in, conv_dw, conv_db, conv_ln_g, conv_ln_b, hg_norm_g, att_fb,
                   w_branch, w_out, norm2_g, peer_wq, peer_keys, peer_u, peer_v):
    n_main = 2 * BRANCH_W + 4 * HG_HEADS * HG_DK + 3 * ATT_HEADS * ATT_HD
    w = w_in[l]
    row = lambda a: a.reshape(1, -1).astype(F32)
    return dict(
        g1=row(norm1_g[l]),
        w_main=jnp.concatenate([w[:, n_main + ATT_HEADS:], w[:, :n_main]], axis=1).astype(BF16),
        w_f=jnp.pad(w[:, n_main:n_main + ATT_HEADS], ((0, 0), (0, LANES - ATT_HEADS))).astype(BF16),
        fb=jnp.pad(att_fb[l].astype(F32), (0, LANES - ATT_HEADS)).reshape(1, LANES),
        conv_w=conv_dw[l].astype(F32), conv_b=row(conv_db[l]), ln_g=row(conv_ln_g[l]), ln_b=row(conv_ln_b[l]),
        lb=row(lbs[l]), hg_g=row(hg_norm_g[l]),
        wb=w_branch[l].astype(BF16), wo=w_out[l].astype(BF16), g2=row(norm2_g[l]),
        wqt=peer_wq[l].T.astype(BF16), keys=peer_keys[l].astype(BF16),
        u=peer_u[l].astype(BF16), vt=peer_v[l].T.astype(BF16),
    )


def _token_mixers(x, wt, nb, seq, conv_buf, hg_state, attend):
    n = nb * seq
    z, lf = _in_proj(x, wt["g1"], wt["w_main"], wt["w_f"], wt["fb"], _tile(n, 1024))
    y_conv, conv_new = _conv_branch(z, conv_buf, wt["conv_w"], wt["conv_b"], wt["ln_g"], wt["ln_b"],
                                    nb, seq, _tile(seq, 512))
    y_hgrn, hg_new = _hgrn_branch(z, hg_state, wt["lb"], wt["hg_g"], nb, seq, _tile(seq, 128), _tile(seq, 16))
    y_att = attend(z, lf)
    h, xn = _merge(x, y_conv, y_hgrn, y_att, z, wt["wb"], wt["wo"], wt["g2"], _tile(n, 512))
    route = _peer_route(xn, wt["wqt"], wt["keys"])
    out = _peer_dense(xn, h, wt["u"], wt["vt"], route, _PEER_CHUNK)
    w = ATT_HEADS * ATT_HD
    k_rows = z[:, ZB_AK * BRANCH_W:ZB_AK * BRANCH_W + w].reshape(nb, seq, ATT_HEADS, ATT_HD)
    v_rows = z[:, ZB_AV * BRANCH_W:ZB_AV * BRANCH_W + w].reshape(nb, seq, ATT_HEADS, ATT_HD)
    logf = lf[:, :ATT_HEADS].reshape(nb, seq, ATT_HEADS)
    return out, conv_new, hg_new, k_rows, v_rows, logf


def kernel(x_prompt, x_sample, cache_k, cache_v, cache_logf, state_conv, state_hgrn, page_table, norm1_g, w_in, conv_dw, conv_db, conv_ln_g, conv_ln_b, hg_lb_logits, hg_norm_g, att_fb, w_branch, w_out, norm2_g, peer_wq, peer_keys, peer_u, peer_v, final_g):
    depth = w_in.shape[0]
    bp, sp, d = x_prompt.shape
    bs, ss, _ = x_sample.shape
    w = ATT_HEADS * ATT_HD
    probs = jax.nn.softmax(hg_lb_logits.astype(F32), axis=0)
    lbs = jnp.cumsum(probs, axis=0) - probs[0:1]
    n_phys = cache_k.shape[1]
    ckt = jnp.transpose(cache_k, (0, 1, 3, 4, 2)).reshape(depth * n_phys, ATT_HEADS, ATT_HD, PAGE_SIZE)
    cvt = jnp.transpose(cache_v, (0, 1, 3, 4, 2)).reshape(depth * n_phys, ATT_HEADS, ATT_HD, PAGE_SIZE)
    lf_rows = jnp.swapaxes(cache_logf.astype(F32), 2, 3).reshape(depth * n_phys * ATT_HEADS, PAGE_SIZE)
    suffix, total = (a.reshape(depth * n_phys, ATT_HEADS, PAGE_SIZE)
                     for a in _page_suffix(lf_rows, _tile(lf_rows.shape[0], 2048)))

    hp = x_prompt.reshape(bp * sp, d)
    hs = x_sample.reshape(bs * ss, d)
    outs_p, outs_s = [], []
    for l in range(depth):
        wt = _layer_weights(l, lbs, norm1_g, w_in, conv_dw, conv_db, conv_ln_g, conv_ln_b, hg_norm_g, att_fb,
                            w_branch, w_out, norm2_g, peer_wq, peer_keys, peer_u, peer_v)

        def attend_prompt(z, lf):
            ct = _logf_cumsum(lf, bp, sp, _tile(sp, 512))
            return _fox_prompt(z, ct, bp, sp, _tile(sp, _FOX_BLOCK))

        def attend_sample(z, lf, l=l):
            lfn_t = jnp.swapaxes(lf[:, :ATT_HEADS].reshape(bs, ss, ATT_HEADS), 1, 2)
            lfn_t = jnp.pad(lfn_t, ((0, 0), (0, 0), (0, PAGE_SIZE - ss)))
            return _fox_sample(z, lfn_t, ckt, cvt, suffix, total, page_table, l * n_phys, bs, ss)

        hp, *rp = _token_mixers(hp, wt, bp, sp, jnp.zeros((bp, CONV_K - 1, BRANCH_W), F32),
                                jnp.zeros((bp, HG_HEADS, HG_DK, HG_DK), F32), attend_prompt)
        hs, *rs = _token_mixers(hs, wt, bs, ss, state_conv[l], state_hgrn[l], attend_sample)
        outs_p.append(rp)
        outs_s.append(rs)

    fg = final_g.reshape(1, d).astype(F32)
    y_prompt = _final_norm(hp, fg, _tile(bp * sp, 1024)).reshape(bp, sp, d)
    y_sample = _final_norm(hs, fg, _tile(bs * ss, 1024)).reshape(bs, ss, d)
    stack = lambda outs, i: jnp.stack([r[i] for r in outs])
    return (y_prompt, y_sample,
            stack(outs_p, 2), stack(outs_p, 3), stack(outs_p, 4), stack(outs_p, 0), stack(outs_p, 1),
            stack(outs_s, 2), stack(outs_s, 3), stack(outs_s, 4), stack(outs_s, 0), stack(outs_s, 1))
```

```python
import functools

import jax
import jax.numpy as jnp
from jax import lax
from jax.experimental import pallas as pl
from jax.experimental.pallas import tpu as pltpu

F32 = jnp.float32
BF16 = jnp.bfloat16
HI = lax.Precision.HIGHEST
SDS = jax.ShapeDtypeStruct

D_MODEL = 1024
BRANCH_W = 512
CONV_K = 31
HG_HEADS = 4
HG_DK = 128
ATT_HEADS = 8
ATT_HD = 64
PEER_HEADS = 8
N_KEYS = 128
PEER_TOPK = 16
PAGE_SIZE = 128
EPS = 1e-6
GATE_FLOOR = 1e-20
MASK_VALUE = -1e30
LANES = 128
_SUBLANES = 8
_HD_SHIFT = ATT_HD.bit_length() - 1
_HEAD_SHIFT = ATT_HEADS.bit_length() - 1

Z_COLS = 15 * BRANCH_W
ZB_CONV = 3
ZB_HQ, ZB_HF, ZB_HI, ZB_HG, ZB_AQ, ZB_AK, ZB_AV = 8, 9, 10, 11, 12, 13, 14

_PAIRS = tuple((a, b) for a in range(PEER_TOPK) for b in range(PEER_TOPK) if (a + 1) * (b + 1) <= PEER_TOPK)

_NT = (((1,), (1,)), ((), ()))
_TN = (((0,), (0,)), ((), ()))


def _params(*sem, vmem_mb=48):
    return pltpu.CompilerParams(dimension_semantics=sem, vmem_limit_bytes=vmem_mb * 1024 * 1024)


def _sigmoid(x):
    return 1.0 / (1.0 + jnp.exp(-x))


def _log_sigmoid(x):
    return jnp.minimum(x, 0.0) - jnp.log1p(jnp.exp(-jnp.abs(x)))


def _rms(x, g):
    return x * lax.rsqrt(jnp.mean(x * x, axis=-1, keepdims=True) + EPS) * g


def _in_proj_kernel(x_ref, g_ref, w_ref, wf_ref, fb_ref, z_ref, lf_ref, *rest):
    xn_ref = rest[-1]

    @pl.when(pl.program_id(1) == 0)
    def _():
        xn = _rms(x_ref[...], g_ref[...]).astype(BF16)
        xn_ref[...] = xn
        af = jnp.dot(xn, wf_ref[...], preferred_element_type=F32) + fb_ref[...]
        lf_ref[...] = _log_sigmoid(af)

    z = jnp.dot(xn_ref[...], w_ref[...], preferred_element_type=F32)
    z_ref[...] = z

    if len(rest) == 3:
        @pl.when(pl.program_id(1) == pl.num_programs(1) - 1)
        def _():
            kt_ref, vt_ref = rest[:2]
            kt_ref[0] = z[:, BRANCH_W:2 * BRANCH_W].T
            vt_ref[0] = z[:, 2 * BRANCH_W:].T


def _in_proj(x, g, w_main, w_f, fb, tm, seq=None):
    n, d = x.shape
    tn = 3 * BRANCH_W
    assert ZB_AK * BRANCH_W == Z_COLS - 2 * BRANCH_W and ZB_AV * BRANCH_W == Z_COLS - BRANCH_W
    out_specs = [pl.BlockSpec((tm, tn), lambda i, j: (i, j)), pl.BlockSpec((tm, LANES), lambda i, j: (i, 0))]
    out_shape = [SDS((n, Z_COLS), F32), SDS((n, LANES), F32)]
    if seq is not None:
        nlb = seq // tm
        tr = pl.BlockSpec((1, BRANCH_W, tm), lambda i, j: (i // nlb, 0, i % nlb))
        out_specs += [tr, tr]
        out_shape += [SDS((n // seq, BRANCH_W, seq), F32)] * 2
    return pl.pallas_call(
        _in_proj_kernel,
        grid=(n // tm, Z_COLS // tn),
        in_specs=[
            pl.BlockSpec((tm, d), lambda i, j: (i, 0)),
            pl.BlockSpec((1, d), lambda i, j: (0, 0)),
            pl.BlockSpec((d, tn), lambda i, j: (0, j)),
            pl.BlockSpec((d, LANES), lambda i, j: (0, 0)),
            pl.BlockSpec((1, LANES), lambda i, j: (0, 0)),
        ],
        out_specs=out_specs,
        out_shape=out_shape,
        scratch_shapes=[pltpu.VMEM((tm, d), BF16)],
        compiler_params=_params("arbitrary", "arbitrary"),
        name="in_proj",
    )(x, g, w_main, w_f, fb)


_CONV_PAD = 32


def _conv_kernel(a_ref, buf_ref, w_ref, b_ref, g_ref, bt_ref, y_ref, nc_ref, ext_ref, sh_ref, *, tl):
    hist = CONV_K - 1
    lo = _CONV_PAD - hist

    @pl.when(pl.program_id(1) == 0)
    def _():
        ext_ref[lo:_CONV_PAD, :] = buf_ref[0]

    a = a_ref[...]
    ext_ref[_CONV_PAD:_CONV_PAD + tl, :] = a[:, :BRANCH_W] * _sigmoid(a[:, BRANCH_W:])
    acc = jnp.zeros((tl, BRANCH_W), F32)
    for r in range(_SUBLANES):
        taps = [j for j in range(CONV_K) if (lo + j) % _SUBLANES == r]
        if not taps:
            continue
        last = (lo + taps[-1]) - r
        sh_ref[0:last + tl, :] = ext_ref[r:r + last + tl, :]
        for j in taps:
            q0 = lo + j - r
            acc = acc + w_ref[j:j + 1, :] * sh_ref[q0:q0 + tl, :]
    y = acc + b_ref[...]
    mu = jnp.mean(y, axis=-1, keepdims=True)
    yc = y - mu
    var = jnp.mean(yc * yc, axis=-1, keepdims=True)
    y = yc * lax.rsqrt(var + EPS) * g_ref[...] + bt_ref[...]
    y_ref[...] = y * _sigmoid(y)
    tail = ext_ref[tl + lo:tl + _CONV_PAD, :]
    nc_ref[0] = tail
    ext_ref[lo:_CONV_PAD, :] = tail


def _conv_branch(z, buf, w, b, g, bt, nb, seq, tl):
    nt = seq // tl
    hist = CONV_K - 1
    vec = pl.BlockSpec((1, BRANCH_W), lambda i, t: (0, 0))
    return pl.pallas_call(
        functools.partial(_conv_kernel, tl=tl),
        grid=(nb, nt),
        in_specs=[
            pl.BlockSpec((tl, 2 * BRANCH_W), lambda i, t: (i * nt + t, ZB_CONV)),
            pl.BlockSpec((1, hist, BRANCH_W), lambda i, t: (i, 0, 0)),
            pl.BlockSpec((CONV_K, BRANCH_W), lambda i, t: (0, 0)),
            vec, vec, vec,
        ],
        out_specs=[
            pl.BlockSpec((tl, BRANCH_W), lambda i, t: (i * nt + t, 0)),
            pl.BlockSpec((1, hist, BRANCH_W), lambda i, t: (i, 0, 0)),
        ],
        out_shape=[SDS((nb * seq, BRANCH_W), F32), SDS((nb, hist, BRANCH_W), F32)],
        scratch_shapes=[pltpu.VMEM((_CONV_PAD + tl, BRANCH_W), F32), pltpu.VMEM((_CONV_PAD + tl, BRANCH_W), F32)],
        compiler_params=_params("arbitrary", "arbitrary"),
        name="conv_branch",
    )(z, buf, w, b, g, bt)


def _hgrn_kernel(q_ref, f_ref, i_ref, g_ref, s0_ref, lb_ref, ng_ref, y_ref, sn_ref,
                 st_ref, qs_ref, ks_ref, bs_ref, *, tl, c):
    t = pl.program_id(1)

    @pl.when(t == 0)
    def _():
        for h in range(HG_HEADS):
            st_ref[h] = s0_ref[0, h].T

    lb = jnp.clip(lb_ref[...], 0.0, 1.0)
    f = lb + (1.0 - lb) * _sigmoid(f_ref[...])
    lf = jnp.log(jnp.maximum(f, GATE_FLOOR))
    q = q_ref[...]
    qs_ref[...] = q * _sigmoid(q)
    ks_ref[...] = 1.0 - f
    r = lax.broadcasted_iota(jnp.int32, (tl, tl), 0)
    cc = lax.broadcasted_iota(jnp.int32, (tl, tl), 1)
    tri = jnp.where(cc <= r, jnp.where(cc >= (r & -c), 1.0, 0.0), 0.0).astype(F32)
    bs_ref[...] = jnp.dot(tri, lf, precision=HI, preferred_element_type=F32)
    rowid = lax.broadcasted_iota(jnp.int32, (c, 1), 0)

    def chunk(ci, carry):
        off = pl.multiple_of(ci * c, c)
        for h in range(HG_HEADS):
            sl = slice(h * HG_DK, (h + 1) * HG_DK)
            qc = qs_ref[pl.ds(off, c), sl]
            kc = ks_ref[pl.ds(off, c), sl]
            bc = bs_ref[pl.ds(off, c), sl]
            vc = i_ref[pl.ds(off, c), sl]
            o = jnp.zeros((c, HG_DK), F32)
            for s in range(c):
                k_row = kc[s:s + 1, :]
                b_row = bc[s:s + 1, :]
                v_row = vc[s:s + 1, :]
                p = qc * k_row * jnp.exp(jnp.minimum(bc - b_row, 0.0))
                a = jnp.sum(p, axis=1, keepdims=True)
                o = o + jnp.where(rowid >= s, a, 0.0) * v_row
            st = st_ref[h]
            qt = (qc * jnp.exp(bc)).astype(BF16)
            o = o + lax.dot_general(qt, st.astype(BF16), _NT, preferred_element_type=F32)
            b_last = bc[c - 1:c, :]
            kt = (kc * jnp.exp(b_last - bc)).astype(BF16)
            upd = lax.dot_general(vc.astype(BF16), kt, _TN, preferred_element_type=F32)
            st_ref[h] = st * jnp.exp(b_last) + upd
            o = _rms(o, ng_ref[...])
            y_ref[pl.ds(off, c), sl] = o * _sigmoid(g_ref[pl.ds(off, c), sl])
        return carry

    lax.fori_loop(0, tl // c, chunk, 0)

    @pl.when(t == pl.num_programs(1) - 1)
    def _():
        for h in range(HG_HEADS):
            sn_ref[0, h] = st_ref[h].T


def _hgrn_branch(z, s0, lb, ng, nb, seq, tl, c):
    nt = seq // tl
    w = HG_HEADS * HG_DK

    def col(j):
        return pl.BlockSpec((tl, w), lambda i, t: (i * nt + t, j))

    st_spec = pl.BlockSpec((1, HG_HEADS, HG_DK, HG_DK), lambda i, t: (i, 0, 0, 0))
    return pl.pallas_call(
        functools.partial(_hgrn_kernel, tl=tl, c=c),
        grid=(nb, nt),
        in_specs=[col(ZB_HQ), col(ZB_HF), col(ZB_HI), col(ZB_HG), st_spec,
                  pl.BlockSpec((1, w), lambda i, t: (0, 0)),
                  pl.BlockSpec((1, HG_DK), lambda i, t: (0, 0))],
        out_specs=[pl.BlockSpec((tl, w), lambda i, t: (i * nt + t, 0)), st_spec],
        out_shape=[SDS((nb * seq, w), F32), SDS((nb, HG_HEADS, HG_DK, HG_DK), F32)],
        scratch_shapes=[pltpu.VMEM((HG_HEADS, HG_DK, HG_DK), F32),
                        pltpu.VMEM((tl, w), F32), pltpu.VMEM((tl, w), F32), pltpu.VMEM((tl, w), F32)],
        compiler_params=_params("arbitrary", "arbitrary"),
        name="hgrn_branch",
    )(z, z, z, z, s0, lb, ng)


def _cumsum_kernel(lf_ref, ct_ref, carry_ref, *, tl):
    @pl.when(pl.program_id(1) == 0)
    def _():
        carry_ref[...] = jnp.zeros_like(carry_ref)

    r = lax.broadcasted_iota(jnp.int32, (tl, tl), 0)
    cc = lax.broadcasted_iota(jnp.int32, (tl, tl), 1)
    tri = jnp.where(cc <= r, 1.0, 0.0).astype(F32)
    cs = jnp.dot(tri, lf_ref[...], precision=HI, preferred_element_type=F32) + carry_ref[...]
    carry_ref[...] = cs[tl - 1:tl, :]
    ct_ref[0] = cs.T[:ATT_HEADS, :]


def _logf_cumsum(lf, nb, seq, tl):
    nt = seq // tl
    return pl.pallas_call(
        functools.partial(_cumsum_kernel, tl=tl),
        grid=(nb, nt),
        in_specs=[pl.BlockSpec((tl, LANES), lambda i, t: (i * nt + t, 0))],
        out_specs=pl.BlockSpec((1, ATT_HEADS, tl), lambda i, t: (i, 0, t)),
        out_shape=SDS((nb, ATT_HEADS, seq), F32),
        scratch_shapes=[pltpu.VMEM((1, LANES), F32)],
        compiler_params=_params("arbitrary", "arbitrary"),
        name="logf_cumsum",
    )(lf)


_LOG2E = 1.4426950408889634
_FOX_BLOCK = 1024
_FOX_SUB = 512


def _fox_kernel(q_ref, k_ref, v_ref, ck_ref, cq_ref, o_ref, qm_ref, m_ref, acc_ref, *, t):
    pair = pl.program_id(1)
    qi = pl.program_id(2)
    ki = pl.program_id(3)
    lane = lax.broadcasted_iota(jnp.int32, (1, LANES), 1)

    @pl.when(ki == 0)
    def _():
        q = q_ref[...] * (ATT_HD ** -0.5 * _LOG2E)
        qm_ref[0] = jnp.where(lane < ATT_HD, q, 0.0).astype(BF16)
        qm_ref[1] = jnp.where(lane >= ATT_HD, q, 0.0).astype(BF16)
        m_ref[...] = jnp.full_like(m_ref, MASK_VALUE)
        acc_ref[...] = jnp.zeros_like(acc_ref)

    def step(diagonal):
        k = k_ref[...].astype(BF16)
        v = jnp.concatenate([v_ref[...].astype(BF16), jnp.ones((t, LANES), BF16)], axis=1)
        head_iota = lax.broadcasted_iota(jnp.int32, (ATT_HEADS, 1), 0)
        sub = min(t, _FOX_SUB)
        biases, states = [], []
        for j in range(2):
            own = head_iota == 2 * pair + j
            ck = jnp.sum(jnp.where(own, ck_ref[0], 0.0), axis=0, keepdims=True)
            c0 = jnp.sum(jnp.where(own, cq_ref[0, :, 0:1], 0.0), axis=0, keepdims=True)
            biases.append((c0 - ck) * _LOG2E)
            states.append((qm_ref[j], m_ref[j], acc_ref[j]))

        def logits(item):
            j, r0 = item
            nk = r0 + sub if diagonal else t
            s = lax.dot_general(states[j][0][r0:r0 + sub], k[:nk], _NT, preferred_element_type=F32)
            s = s + biases[j][:, :nk]
            if diagonal:
                rid = lax.broadcasted_iota(jnp.int32, (sub, nk), 0) + r0
                cid = lax.broadcasted_iota(jnp.int32, (sub, nk), 1)
                s = jnp.where(cid <= rid, s, MASK_VALUE)
            return s, nk

        items = [(j, r0) for j in range(2) for r0 in range(0, t, sub)]
        m_out, acc_out = ([], []), ([], [])
        nxt = logits(items[0])
        for idx, (j, r0) in enumerate(items):
            s, nk = nxt
            if idx + 1 < len(items):
                nxt = logits(items[idx + 1])
            rows = slice(r0, r0 + sub)
            m_old = states[j][1][rows]
            m_new = jnp.maximum(m_old, jnp.max(s, axis=1, keepdims=True))
            alpha = jnp.exp2(m_old - m_new)
            p = jnp.exp2(s - m_new[:, 0:1]).astype(BF16)
            pv = jnp.dot(p, v[:nk], preferred_element_type=F32)
            acc_out[j].append(jnp.concatenate([alpha, alpha], axis=1) * states[j][2][rows] + pv)
            m_out[j].append(m_new)
        for j in range(2):
            acc_ref[j] = jnp.concatenate(acc_out[j], axis=0)
            m_ref[j] = jnp.concatenate(m_out[j], axis=0)

    @pl.when(ki < qi)
    def _():
        step(False)

    @pl.when(ki == qi)
    def _():
        step(True)
        o0 = acc_ref[0, :, :LANES] / acc_ref[0, :, LANES:]
        o1 = acc_ref[1, :, :LANES] / acc_ref[1, :, LANES:]
        o_ref[...] = jnp.where(lane < ATT_HD, o0, o1)


def _fox_prompt(z, ct, nb, seq, t):
    nq = seq // t
    npair = ATT_HEADS // 2
    zq, zk, zv = (zb * (BRANCH_W // LANES) for zb in (ZB_AQ, ZB_AK, ZB_AV))
    return pl.pallas_call(
        functools.partial(_fox_kernel, t=t),
        grid=(nb, npair, nq, nq),
        in_specs=[
            pl.BlockSpec((t, LANES), lambda b, p, qi, ki: (b * nq + qi, zq + p)),
            pl.BlockSpec((t, LANES), lambda b, p, qi, ki: (b * nq + jnp.minimum(ki, qi), zk + p)),
            pl.BlockSpec((t, LANES), lambda b, p, qi, ki: (b * nq + jnp.minimum(ki, qi), zv + p)),
            pl.BlockSpec((1, ATT_HEADS, t), lambda b, p, qi, ki: (b, 0, jnp.minimum(ki, qi))),
            pl.BlockSpec((1, ATT_HEADS, t), lambda b, p, qi, ki: (b, 0, qi)),
        ],
        out_specs=pl.BlockSpec((t, LANES), lambda b, p, qi, ki: (b * nq + qi, p)),
        out_shape=SDS((nb * seq, BRANCH_W), F32),
        scratch_shapes=[pltpu.VMEM((2, t, LANES), BF16), pltpu.VMEM((2, t, LANES), F32),
                        pltpu.VMEM((2, t, 2 * LANES), F32)],
        compiler_params=_params("arbitrary", "arbitrary", "arbitrary", "arbitrary"),
        name="fox_prompt",
    )(z, z, z, ct, ct)


def _page_suffix_kernel(lf_ref, suf_ref, tot_ref):
    r = lax.broadcasted_iota(jnp.int32, (PAGE_SIZE, PAGE_SIZE), 0)
    c = lax.broadcasted_iota(jnp.int32, (PAGE_SIZE, PAGE_SIZE), 1)
    lf = lf_ref[...]
    suf_ref[...] = jnp.dot(lf, jnp.where(r > c, 1.0, 0.0).astype(F32), precision=HI, preferred_element_type=F32)
    tot_ref[...] = jnp.dot(lf, jnp.ones((PAGE_SIZE, PAGE_SIZE), F32), precision=HI, preferred_element_type=F32)


def _page_suffix(lf_rows, tr):
    n = lf_rows.shape[0]
    spec = pl.BlockSpec((tr, PAGE_SIZE), lambda i: (i, 0))
    return pl.pallas_call(
        _page_suffix_kernel,
        grid=(n // tr,),
        in_specs=[spec],
        out_specs=[spec, spec],
        out_shape=[SDS((n, PAGE_SIZE), F32), SDS((n, PAGE_SIZE), F32)],
        compiler_params=_params("arbitrary"),
        name="page_suffix",
    )(lf_rows)


_PAGES_PER_STEP = 8


def _fox_sample_kernel(pt_ref, q_ref, kn_ref, vn_ref, lfn_ref, *rest, g, tt):
    del pt_ref
    kp_refs, vp_refs, sf_refs, tot_refs = rest[:g], rest[g:2 * g], rest[2 * g:3 * g], rest[3 * g:4 * g]
    o_ref, qb_ref, m_ref, l_ref, acc_ref, car_ref, cn_ref = rest[4 * g:]
    step = pl.program_id(1)
    rows = tt * ATT_HEADS
    w = ATT_HEADS * ATT_HD
    rid = lax.broadcasted_iota(jnp.int32, (rows, 1), 0)
    hmask = (lax.broadcasted_iota(jnp.int32, (rows, w), 1) >> _HD_SHIFT) == (rid & (ATT_HEADS - 1))
    tok = rid >> _HEAD_SHIFT
    lane = lax.broadcasted_iota(jnp.int32, (rows, PAGE_SIZE), 1)

    def tile_rows(x):
        return jnp.broadcast_to(x[None], (tt,) + x.shape).reshape(rows, x.shape[-1])

    def online(scores, values, dims):
        m_old = m_ref[...]
        m_new = m_old
        for s in scores:
            m_new = jnp.maximum(m_new, jnp.max(s, axis=1, keepdims=True))
        alpha = jnp.exp(m_old - m_new)
        l_new = alpha * l_ref[...]
        acc = alpha * acc_ref[...]
        for s, v in zip(scores, values):
            p = jnp.exp(s - m_new)
            l_new = l_new + jnp.sum(p, axis=1, keepdims=True)
            acc = acc + lax.dot_general(p.astype(BF16), v, dims, preferred_element_type=F32)
        l_ref[...] = l_new
        acc_ref[...] = acc
        m_ref[...] = m_new

    @pl.when(step == 0)
    def _():
        q = q_ref[...] * (ATT_HD ** -0.5)
        qrep = jnp.broadcast_to(q[:, None, :], (tt, ATT_HEADS, w)).reshape(rows, w)
        qb_ref[...] = jnp.where(hmask, qrep, 0.0).astype(BF16)
        m_ref[...] = jnp.full_like(m_ref, MASK_VALUE)
        l_ref[...] = jnp.zeros_like(l_ref)
        acc_ref[...] = jnp.zeros_like(acc_ref)
        car_ref[...] = jnp.zeros_like(car_ref)
        r2 = lax.broadcasted_iota(jnp.int32, (PAGE_SIZE, PAGE_SIZE), 0)
        c2 = lax.broadcasted_iota(jnp.int32, (PAGE_SIZE, PAGE_SIZE), 1)
        incl = jnp.where(r2 <= c2, 1.0, 0.0).astype(F32)
        cn_ref[...] = jnp.dot(tile_rows(lfn_ref[0]), incl, precision=HI, preferred_element_type=F32)

    qb = qb_ref[...]
    cn = cn_ref[...]
    cn_own = jnp.sum(jnp.where(lane == tok, cn, 0.0), axis=1, keepdims=True)
    car = car_ref[...]
    scores = []
    for i in range(g):
        kt = kp_refs[i][0].reshape(w, PAGE_SIZE).astype(BF16)
        s = jnp.dot(qb, kt, preferred_element_type=F32)
        scores.append(s + ((cn_own + car) + tile_rows(sf_refs[i][0])))
        car = car + tile_rows(tot_refs[i][0])[:, 0:1]
    car_ref[...] = car
    online(scores, [vp_refs[i][0].reshape(w, PAGE_SIZE).astype(BF16) for i in range(g)], _NT)

    @pl.when(step == pl.num_programs(1) - 1)
    def _():
        pad = jnp.zeros((PAGE_SIZE - tt, w), F32)
        kn = jnp.concatenate([kn_ref[...], pad], axis=0).astype(BF16)
        vn = jnp.concatenate([vn_ref[...], pad], axis=0).astype(BF16)
        sn = lax.dot_general(qb, kn, _NT, preferred_element_type=F32)
        sn = jnp.where(lane <= tok, sn + (cn_own - cn), MASK_VALUE)
        online([sn], [vn], (((1,), (0,)), ((), ())))
        o = jnp.where(hmask, acc_ref[...] / l_ref[...], 0.0)
        o_ref[...] = jnp.sum(o.reshape(tt, ATT_HEADS, w), axis=1)


def _fox_sample(z, lfn_t, ckt, cvt, suffix, total, page_table, page_base, nb, tt):
    npg = page_table.shape[1]
    g = _PAGES_PER_STEP
    assert npg % g == 0
    w = ATT_HEADS * ATT_HD
    rows = tt * ATT_HEADS

    def page(i, ndim):
        def index(b, p, pt):
            return (page_base + pt[b * npg + (npg - 1 - (p * g + i))],) + (0,) * (ndim - 1)
        return index

    in_specs = [pl.BlockSpec((tt, w), lambda b, p, pt, zb=zb: (b, zb)) for zb in (ZB_AQ, ZB_AK, ZB_AV)]
    in_specs += [pl.BlockSpec((1, ATT_HEADS, PAGE_SIZE), lambda b, p, pt: (b, 0, 0))]
    in_specs += [pl.BlockSpec((1, ATT_HEADS, ATT_HD, PAGE_SIZE), page(i, 4)) for i in range(g)] * 2
    in_specs += [pl.BlockSpec((1, ATT_HEADS, PAGE_SIZE), page(i, 3)) for i in range(g)] * 2
    grid_spec = pltpu.PrefetchScalarGridSpec(
        num_scalar_prefetch=1,
        grid=(nb, npg // g),
        in_specs=in_specs,
        out_specs=pl.BlockSpec((tt, w), lambda b, p, pt: (b, 0)),
        scratch_shapes=[pltpu.VMEM((rows, w), BF16), pltpu.VMEM((rows, 1), F32), pltpu.VMEM((rows, 1), F32),
                        pltpu.VMEM((rows, w), F32), pltpu.VMEM((rows, 1), F32), pltpu.VMEM((rows, PAGE_SIZE), F32)],
    )
    return pl.pallas_call(
        functools.partial(_fox_sample_kernel, g=g, tt=tt),
        grid_spec=grid_spec,
        out_shape=SDS((nb * tt, w), F32),
        compiler_params=_params("arbitrary", "arbitrary"),
        name="fox_sample",
    )(page_table.reshape(-1), z, z, z, lfn_t, *([ckt] * g), *([cvt] * g), *([suffix] * g), *([total] * g))


def _merge_kernel(x_ref, yc_ref, yh_ref, ya_ref, gt_ref, wb_ref, wo_ref, g2_ref, h_ref, xnt_ref):
    merged = jnp.zeros(x_ref.shape, F32)
    for b, y_ref in enumerate((yc_ref, yh_ref, ya_ref)):
        proj = jnp.dot(y_ref[...].astype(BF16), wb_ref[b], preferred_element_type=F32)
        merged = merged + _sigmoid(gt_ref[:, b * D_MODEL:(b + 1) * D_MODEL]) * proj
    h = x_ref[...] + jnp.dot(merged.astype(BF16), wo_ref[...], preferred_element_type=F32)
    h_ref[...] = h
    xnt = _rms(h, g2_ref[...]).T.astype(BF16)
    for k in range(xnt_ref.shape[0]):
        xnt_ref[k] = xnt[:, k * _PEER_T:(k + 1) * _PEER_T]


_PEER_T = 256


def _merge(x, yc, yh, ya, z, wb, wo, g2, tm):
    n, d = x.shape
    assert tm % _PEER_T == 0
    br = pl.BlockSpec((tm, BRANCH_W), lambda i: (i, 0))
    return pl.pallas_call(
        _merge_kernel,
        grid=(n // tm,),
        in_specs=[
            pl.BlockSpec((tm, d), lambda i: (i, 0)), br, br, br,
            pl.BlockSpec((tm, 3 * d), lambda i: (i, 0)),
            pl.BlockSpec((3, BRANCH_W, d), lambda i: (0, 0, 0)),
            pl.BlockSpec((d, d), lambda i: (0, 0)),
            pl.BlockSpec((1, d), lambda i: (0, 0)),
        ],
        out_specs=[pl.BlockSpec((tm, d), lambda i: (i, 0)),
                   pl.BlockSpec((tm // _PEER_T, d, _PEER_T), lambda i: (i, 0, 0))],
        out_shape=[SDS((n, d), F32), SDS((n // _PEER_T, d, _PEER_T), BF16)],
        compiler_params=_params("arbitrary"),
        name="merge",
    )(x, yc, yh, ya, z, wb, wo, g2)


def _take_max(work, row_iota, n_rows):
    m = jnp.max(work, axis=0, keepdims=True)
    first = jnp.min(jnp.where(work == m, row_iota, float(n_rows)), axis=0, keepdims=True)
    return m, jnp.where(row_iota == first, -jnp.inf, work)


def _oddeven_merge_sort_pairs(n):
    pairs, p = [], 1
    while p < n:
        k = p
        while k >= 1:
            for j in range(k % p, n - k, 2 * k):
                for i in range(min(k, n - j - k)):
                    if (i + j) // (2 * p) == (i + j + k) // (2 * p):
                        pairs.append((i + j, i + j + k))
            k //= 2
        p *= 2
    return tuple(pairs)


_SORT_PAIRS = _oddeven_merge_sort_pairs(PEER_TOPK)


def _exchange(v, a, b):
    v[a], v[b] = jnp.maximum(v[a], v[b]), jnp.minimum(v[a], v[b])


def _top16_sorted(s):
    v = [s[_SUBLANES * i:_SUBLANES * (i + 1), :] for i in range(PEER_TOPK)]
    for a, b in _SORT_PAIRS:
        _exchange(v, a, b)
    for shift in (4, 6, 7):
        w = [pltpu.roll(x, shift, 0) for x in v]
        v = [jnp.maximum(v[k], w[PEER_TOPK - 1 - k]) for k in range(PEER_TOPK)]
        d = PEER_TOPK // 2
        while d >= 1:
            for i in range(PEER_TOPK):
                if not i & d:
                    _exchange(v, i, i + d)
            d //= 2
    return [x[0:1, :] for x in v]


_N_CAND = -(-len(_PAIRS) // 8) * 8


def _peer_route_kernel(xnt_ref, wqt_ref, keys_ref, cut1_ref, e1_ref, rank2_ref, e2_ref, cand_ref, *, t):
    xnt = xnt_ref[0]
    n_cand = _N_CAND
    cand_iota = lax.broadcasted_iota(jnp.int32, (n_cand, t), 0).astype(F32)
    cand_ref[len(_PAIRS):, :] = jnp.full((n_cand - len(_PAIRS), t), -jnp.inf, F32)

    def head(h, carry):
        dk = 2 * N_KEYS
        qt = jnp.dot(wqt_ref[pl.ds(pl.multiple_of(h * dk, dk), dk), :], xnt, preferred_element_type=F32)
        scores, tops = [], []
        for p in range(2):
            s = jnp.dot(keys_ref[h, p], qt[p * N_KEYS:(p + 1) * N_KEYS].astype(BF16), preferred_element_type=F32)
            scores.append(s)
            tops.append(_top16_sorted(s))
        for i, (a, b) in enumerate(_PAIRS):
            cand_ref[i:i + 1, :] = tops[0][a] + tops[1][b]
        work = cand_ref[...]
        best = []
        for _ in range(PEER_TOPK):
            m, work = _take_max(work, cand_iota, n_cand)
            best.append(m)
        norm = jnp.zeros((1, t), F32)
        for m in best:
            norm = norm + jnp.exp(m - best[0])
        cut = jnp.zeros((N_KEYS, t), F32)
        rank = jnp.full((N_KEYS, t), float(PEER_TOPK), F32)
        for b in range(PEER_TOPK):
            cut = jnp.where(scores[0] + tops[1][b] >= best[-1], float(b + 1), cut)
            rb = PEER_TOPK - 1 - b
            rank = jnp.where(scores[1] >= tops[1][rb], float(rb), rank)
        cut1_ref[0, h] = cut
        e1_ref[0, h] = jnp.exp(scores[0] - tops[0][0]) / norm
        rank2_ref[0, h] = rank.astype(BF16)
        e2_ref[0, h] = jnp.exp(scores[1] - tops[1][0]).astype(BF16)
        return carry

    lax.fori_loop(0, PEER_HEADS, head, 0)


def _peer_route(xnt, wqt, keys):
    nt, d, t = xnt.shape
    big = pl.BlockSpec((1, PEER_HEADS, N_KEYS, t), lambda i: (i, 0, 0, 0))
    wide, narrow = SDS((nt, PEER_HEADS, N_KEYS, t), F32), SDS((nt, PEER_HEADS, N_KEYS, t), BF16)
    return pl.pallas_call(
        functools.partial(_peer_route_kernel, t=t),
        grid=(nt,),
        in_specs=[
            pl.BlockSpec((1, d, t), lambda i: (i, 0, 0)),
            pl.BlockSpec(wqt.shape, lambda i: (0, 0)),
            pl.BlockSpec(keys.shape, lambda i: (0, 0, 0, 0)),
        ],
        out_specs=[big, big, big, big],
        out_shape=[wide, wide, narrow, narrow],
        scratch_shapes=[pltpu.VMEM((_N_CAND, t), F32)],
        compiler_params=_params("arbitrary"),
        name="peer_route",
    )(xnt, wqt, keys)


_BF16_ROWS = 16
_PEER_CHUNK = 2048


def _peer_dense_kernel(xnt_ref, h_ref, u_ref, vt_ref, cut1_ref, e1_ref, rank2_ref, e2_ref, o_ref,
                       acc_ref, act_ref, cutx_ref, e1x_ref, rank2x_ref, e2x_ref, *, t, ec):
    c = pl.program_id(1)
    tile = pl.program_id(2)
    n_chunks = pl.num_programs(1) - 1

    @pl.when(c == 0)
    def _():
        acc_ref[tile] = jnp.zeros(acc_ref.shape[1:], F32)
        act_ref[tile, 1] = jnp.zeros((ec, t), BF16)
        rank2x_ref[tile] = rank2_ref[tile]
        e2x_ref[tile] = e2_ref[tile]

    acc_ref[tile] += jnp.dot(vt_ref[...], act_ref[tile, (c + 1) & 1], preferred_element_type=F32)

    chunk = jnp.minimum(c, n_chunks - 1)
    n_i1 = ec // N_KEYS
    slot = c & 1
    rep = N_KEYS // _BF16_ROWS
    for ii in range(n_i1):
        row = pl.ds(chunk * n_i1 + ii, 1)
        for h in range(PEER_HEADS):
            cutx_ref[h, ii] = jnp.broadcast_to(cut1_ref[tile, h, row, :], (_BF16_ROWS, t)).astype(BF16)
            e1x_ref[h, ii] = jnp.broadcast_to(e1_ref[tile, h, row, :], (_BF16_ROWS, t)).astype(BF16)
    xnt = xnt_ref[tile]
    for ii in range(n_i1):
        rows = slice(ii * N_KEYS, (ii + 1) * N_KEYS)
        hid = jnp.dot(u_ref[rows, :], xnt, preferred_element_type=F32)
        wsum = jnp.zeros((N_KEYS, t), BF16)
        for h in range(PEER_HEADS):
            cut = jnp.concatenate([cutx_ref[h, ii]] * rep, axis=0)
            e1 = jnp.concatenate([e1x_ref[h, ii]] * rep, axis=0)
            wsum = wsum + jnp.where(rank2x_ref[tile, h] < cut, e1 * e2x_ref[tile, h], jnp.zeros((), BF16))
        gelu = 0.5 * hid * (1.0 + lax.erf(hid * (2.0 ** -0.5)))
        act_ref[tile, slot, rows, :] = gelu.astype(BF16) * wsum

    @pl.when(c == n_chunks)
    def _():
        tok = pl.ds(pl.multiple_of(tile * t, t), t)
        o_ref[tok, :] = h_ref[tok, :] + acc_ref[tile].T


_PEER_GROUP = 2


def _peer_dense(xnt, h, u, vt, route, ec):
    nt, d, t = xnt.shape
    g = _PEER_GROUP if nt % _PEER_GROUP == 0 else 1
    nc = u.shape[0] // ec
    n_i1 = ec // N_KEYS
    big = pl.BlockSpec((g, PEER_HEADS, N_KEYS, t), lambda i, c, k: (i, 0, 0, 0))
    tokens = pl.BlockSpec((g * t, d), lambda i, c, k: (i, 0))
    return pl.pallas_call(
        functools.partial(_peer_dense_kernel, t=t, ec=ec),
        grid=(nt // g, nc + 1, g),
        in_specs=[
            pl.BlockSpec((g, d, t), lambda i, c, k: (i, 0, 0)),
            tokens,
            pl.BlockSpec((ec, d), lambda i, c, k: (jnp.minimum(c, nc - 1), 0)),
            pl.BlockSpec((d, ec), lambda i, c, k: (0, jnp.maximum(c - 1, 0))),
            big, big, big, big,
        ],
        out_specs=tokens,
        out_shape=SDS((nt * t, d), F32),
        scratch_shapes=[pltpu.VMEM((g, d, t), F32), pltpu.VMEM((g, 2, ec, t), BF16),
                        pltpu.VMEM((PEER_HEADS, n_i1, _BF16_ROWS, t), BF16),
                        pltpu.VMEM((PEER_HEADS, n_i1, _BF16_ROWS, t), BF16),
                        pltpu.VMEM((g, PEER_HEADS, N_KEYS, t), BF16), pltpu.VMEM((g, PEER_HEADS, N_KEYS, t), BF16)],
        compiler_params=_params("arbitrary", "arbitrary", "arbitrary", vmem_mb=56),
        name="peer_dense",
    )(xnt, h, u, vt, *route)


def _final_norm_kernel(x_ref, g_ref, o_ref):
    o_ref[...] = _rms(x_ref[...], g_ref[...])


def _final_norm(x, g, tm):
    n, d = x.shape
    return pl.pallas_call(
        _final_norm_kernel,
        grid=(n // tm,),
        in_specs=[pl.BlockSpec((tm, d), lambda i: (i, 0)), pl.BlockSpec((1, d), lambda i: (0, 0))],
        out_specs=pl.BlockSpec((tm, d), lambda i: (i, 0)),
        out_shape=SDS((n, d), F32),
        compiler_params=_params("arbitrary"),
        name="final_norm",
    )(x, g)


def _tile(n, pref):
    return pref if n % pref == 0 else n


def _layer_weights(l, lbs, norm1_g, w_in, conv_dw, conv_db, conv_ln_g, conv_ln_b, hg_norm_g, att_fb,
                   w_branch, w_out, norm2_g, peer_wq, peer_keys, peer_u, peer_v):
    n_main = 2 * BRANCH_W + 4 * HG_HEADS * HG_DK + 3 * ATT_HEADS * ATT_HD
    w = w_in[l]
    row = lambda a: a.reshape(1, -1).astype(F32)
    return dict(
        g1=row(norm1_g[l]),
        w_main=jnp.concatenate([w[:, n_main + ATT_HEADS:], w[:, :n_main]], axis=1).astype(BF16),
        w_f=jnp.pad(w[:, n_main:n_main + ATT_HEADS], ((0, 0), (0, LANES - ATT_HEADS))).astype(BF16),
        fb=jnp.pad(att_fb[l].astype(F32), (0, LANES - ATT_HEADS)).reshape(1, LANES),
        conv_w=conv_dw[l].astype(F32), conv_b=row(conv_db[l]), ln_g=row(conv_ln_g[l]), ln_b=row(conv_ln_b[l]),
        lb=row(lbs[l]), hg_g=row(hg_norm_g[l]),
        wb=w_branch[l].astype(BF16), wo=w_out[l].astype(BF16), g2=row(norm2_g[l]),
        wqt=peer_wq[l].T.astype(BF16), keys=peer_keys[l].astype(BF16),
        u=peer_u[l].astype(BF16), vt=peer_v[l].T.astype(BF16),
    )


def _token_mixers(x, wt, nb, seq, conv_buf, hg_state, attend):
    n = nb * seq
    tm = _tile(n, 1024)
    w = ATT_HEADS * ATT_HD
    if seq % tm == 0:
        z, lf, kt, vt = _in_proj(x, wt["g1"], wt["w_main"], wt["w_f"], wt["fb"], tm, seq)
        k_rows, v_rows = (a.reshape(nb, ATT_HEADS, ATT_HD, seq).transpose(0, 3, 1, 2) for a in (kt, vt))
    else:
        z, lf = _in_proj(x, wt["g1"], wt["w_main"], wt["w_f"], wt["fb"], tm)
        k_rows, v_rows = (z[:, zb * BRANCH_W:zb * BRANCH_W + w].reshape(nb, seq, ATT_HEADS, ATT_HD)
                          for zb in (ZB_AK, ZB_AV))
    y_conv, conv_new = _conv_branch(z, conv_buf, wt["conv_w"], wt["conv_b"], wt["ln_g"], wt["ln_b"],
                                    nb, seq, _tile(seq, 512))
    y_hgrn, hg_new = _hgrn_branch(z, hg_state, wt["lb"], wt["hg_g"], nb, seq, _tile(seq, 128), _tile(seq, 16))
    y_att = attend(z, lf)
    h, xn = _merge(x, y_conv, y_hgrn, y_att, z, wt["wb"], wt["wo"], wt["g2"], _tile(n, 512))
    route = _peer_route(xn, wt["wqt"], wt["keys"])
    out = _peer_dense(xn, h, wt["u"], wt["vt"], route, _PEER_CHUNK)
    logf = lf[:, :ATT_HEADS].reshape(nb, seq, ATT_HEADS)
    return out, conv_new, hg_new, k_rows, v_rows, logf


def kernel(x_prompt, x_sample, cache_k, cache_v, cache_logf, state_conv, state_hgrn, page_table, norm1_g, w_in, conv_dw, conv_db, conv_ln_g, conv_ln_b, hg_lb_logits, hg_norm_g, att_fb, w_branch, w_out, norm2_g, peer_wq, peer_keys, peer_u, peer_v, final_g):
    depth = w_in.shape[0]
    bp, sp, d = x_prompt.shape
    bs, ss, _ = x_sample.shape
    w = ATT_HEADS * ATT_HD
    probs = jax.nn.softmax(hg_lb_logits.astype(F32), axis=0)
    lbs = jnp.cumsum(probs, axis=0) - probs[0:1]
    n_phys = cache_k.shape[1]
    ckt = jnp.transpose(cache_k, (0, 1, 3, 4, 2)).reshape(depth * n_phys, ATT_HEADS, ATT_HD, PAGE_SIZE)
    cvt = jnp.transpose(cache_v, (0, 1, 3, 4, 2)).reshape(depth * n_phys, ATT_HEADS, ATT_HD, PAGE_SIZE)
    lf_rows = jnp.swapaxes(cache_logf.astype(F32), 2, 3).reshape(depth * n_phys * ATT_HEADS, PAGE_SIZE)
    suffix, total = (a.reshape(depth * n_phys, ATT_HEADS, PAGE_SIZE)
                     for a in _page_suffix(lf_rows, _tile(lf_rows.shape[0], 2048)))

    hp = x_prompt.reshape(bp * sp, d)
    hs = x_sample.reshape(bs * ss, d)
    outs_p, outs_s = [], []
    for l in range(depth):
        wt = _layer_weights(l, lbs, norm1_g, w_in, conv_dw, conv_db, conv_ln_g, conv_ln_b, hg_norm_g, att_fb,
                            w_branch, w_out, norm2_g, peer_wq, peer_keys, peer_u, peer_v)

        def attend_prompt(z, lf):
            ct = _logf_cumsum(lf, bp, sp, _tile(sp, 512))
            return _fox_prompt(z, ct, bp, sp, _tile(sp, _FOX_BLOCK))

        def attend_sample(z, lf, l=l):
            lfn_t = jnp.swapaxes(lf[:, :ATT_HEADS].reshape(bs, ss, ATT_HEADS), 1, 2)
            lfn_t = jnp.pad(lfn_t, ((0, 0), (0, 0), (0, PAGE_SIZE - ss)))
            return _fox_sample(z, lfn_t, ckt, cvt, suffix, total, page_table, l * n_phys, bs, ss)

        hp, *rp = _token_mixers(hp, wt, bp, sp, jnp.zeros((bp, CONV_K - 1, BRANCH_W), F32),
                                jnp.zeros((bp, HG_HEADS, HG_DK, HG_DK), F32), attend_prompt)
        hs, *rs = _token_mixers(hs, wt, bs, ss, state_conv[l], state_hgrn[l], attend_sample)
        outs_p.append(rp)
        outs_s.append(rs)

    fg = final_g.reshape(1, d).astype(F32)
    y_prompt = _final_norm(hp, fg, _tile(bp * sp, 1024)).reshape(bp, sp, d)
    y_sample = _final_norm(hs, fg, _tile(bs * ss, 1024)).reshape(bs, ss, d)
    stack = lambda outs, i: jnp.stack([r[i] for r in outs])
    return (y_prompt, y_sample,
            stack(outs_p, 2), stack(outs_p, 3), stack(outs_p, 4), stack(outs_p, 0), stack(outs_p, 1),
            stack(outs_s, 2), stack(outs_s, 3), stack(outs_s, 4), stack(outs_s, 0), stack(outs_s, 1))
```

```python
import functools

import jax
import jax.numpy as jnp
from jax import lax
from jax.experimental import pallas as pl
from jax.experimental.pallas import tpu as pltpu

F32 = jnp.float32
BF16 = jnp.bfloat16
HI = lax.Precision.HIGHEST
SDS = jax.ShapeDtypeStruct

D_MODEL = 1024
BRANCH_W = 512
CONV_K = 31
HG_HEADS = 4
HG_DK = 128
ATT_HEADS = 8
ATT_HD = 64
PEER_HEADS = 8
N_KEYS = 128
PEER_TOPK = 16
PAGE_SIZE = 128
EPS = 1e-6
GATE_FLOOR = 1e-20
MASK_VALUE = -1e30
LANES = 128
_SUBLANES = 8
_HD_SHIFT = ATT_HD.bit_length() - 1
_HEAD_SHIFT = ATT_HEADS.bit_length() - 1

Z_COLS = 15 * BRANCH_W
ZB_CONV = 3
ZB_HQ, ZB_HF, ZB_HI, ZB_HG, ZB_AQ, ZB_AK, ZB_AV = 8, 9, 10, 11, 12, 13, 14

_PAIRS = tuple((a, b) for a in range(PEER_TOPK) for b in range(PEER_TOPK) if (a + 1) * (b + 1) <= PEER_TOPK)

_NT = (((1,), (1,)), ((), ()))
_TN = (((0,), (0,)), ((), ()))


def _params(*sem, vmem_mb=48):
    return pltpu.CompilerParams(dimension_semantics=sem, vmem_limit_bytes=vmem_mb * 1024 * 1024)


def _sigmoid(x):
    return 1.0 / (1.0 + jnp.exp(-x))


def _log_sigmoid(x):
    return jnp.minimum(x, 0.0) - jnp.log1p(jnp.exp(-jnp.abs(x)))


def _rms(x, g):
    return x * lax.rsqrt(jnp.mean(x * x, axis=-1, keepdims=True) + EPS) * g


def _in_proj_kernel(x_ref, g_ref, w_ref, wf_ref, fb_ref, z_ref, lf_ref, *rest):
    xn_ref = rest[-1]

    @pl.when(pl.program_id(1) == 0)
    def _():
        xn = _rms(x_ref[...], g_ref[...]).astype(BF16)
        xn_ref[...] = xn
        af = jnp.dot(xn, wf_ref[...], preferred_element_type=F32) + fb_ref[...]
        lf_ref[...] = _log_sigmoid(af)

    z = jnp.dot(xn_ref[...], w_ref[...], preferred_element_type=F32)
    z_ref[...] = z

    if len(rest) == 3:
        @pl.when(pl.program_id(1) == pl.num_programs(1) - 1)
        def _():
            kt_ref, vt_ref = rest[:2]
            kt_ref[0] = z[:, BRANCH_W:2 * BRANCH_W].T
            vt_ref[0] = z[:, 2 * BRANCH_W:].T


def _in_proj(x, g, w_main, w_f, fb, tm, seq=None):
    n, d = x.shape
    tn = 3 * BRANCH_W
    assert ZB_AK * BRANCH_W == Z_COLS - 2 * BRANCH_W and ZB_AV * BRANCH_W == Z_COLS - BRANCH_W
    out_specs = [pl.BlockSpec((tm, tn), lambda i, j: (i, j)), pl.BlockSpec((tm, LANES), lambda i, j: (i, 0))]
    out_shape = [SDS((n, Z_COLS), F32), SDS((n, LANES), F32)]
    if seq is not None:
        nlb = seq // tm
        tr = pl.BlockSpec((1, BRANCH_W, tm), lambda i, j: (i // nlb, 0, i % nlb))
        out_specs += [tr, tr]
        out_shape += [SDS((n // seq, BRANCH_W, seq), F32)] * 2
    return pl.pallas_call(
        _in_proj_kernel,
        grid=(n // tm, Z_COLS // tn),
        in_specs=[
            pl.BlockSpec((tm, d), lambda i, j: (i, 0)),
            pl.BlockSpec((1, d), lambda i, j: (0, 0)),
            pl.BlockSpec((d, tn), lambda i, j: (0, j)),
            pl.BlockSpec((d, LANES), lambda i, j: (0, 0)),
            pl.BlockSpec((1, LANES), lambda i, j: (0, 0)),
        ],
        out_specs=out_specs,
        out_shape=out_shape,
        scratch_shapes=[pltpu.VMEM((tm, d), BF16)],
        compiler_params=_params("arbitrary", "arbitrary"),
        name="in_proj",
    )(x, g, w_main, w_f, fb)


_CONV_PAD = 32


def _conv_kernel(a_ref, buf_ref, w_ref, b_ref, g_ref, bt_ref, y_ref, nc_ref, ext_ref, sh_ref, *, tl):
    hist = CONV_K - 1
    lo = _CONV_PAD - hist

    @pl.when(pl.program_id(1) == 0)
    def _():
        ext_ref[lo:_CONV_PAD, :] = buf_ref[0]

    a = a_ref[...]
    ext_ref[_CONV_PAD:_CONV_PAD + tl, :] = a[:, :BRANCH_W] * _sigmoid(a[:, BRANCH_W:])
    acc = jnp.zeros((tl, BRANCH_W), F32)
    for r in range(_SUBLANES):
        taps = [j for j in range(CONV_K) if (lo + j) % _SUBLANES == r]
        if not taps:
            continue
        last = (lo + taps[-1]) - r
        sh_ref[0:last + tl, :] = ext_ref[r:r + last + tl, :]
        for j in taps:
            q0 = lo + j - r
            acc = acc + w_ref[j:j + 1, :] * sh_ref[q0:q0 + tl, :]
    y = acc + b_ref[...]
    mu = jnp.mean(y, axis=-1, keepdims=True)
    yc = y - mu
    var = jnp.mean(yc * yc, axis=-1, keepdims=True)
    y = yc * lax.rsqrt(var + EPS) * g_ref[...] + bt_ref[...]
    y_ref[...] = y * _sigmoid(y)
    tail = ext_ref[tl + lo:tl + _CONV_PAD, :]
    nc_ref[0] = tail
    ext_ref[lo:_CONV_PAD, :] = tail


def _conv_branch(z, buf, w, b, g, bt, nb, seq, tl):
    nt = seq // tl
    hist = CONV_K - 1
    vec = pl.BlockSpec((1, BRANCH_W), lambda i, t: (0, 0))
    return pl.pallas_call(
        functools.partial(_conv_kernel, tl=tl),
        grid=(nb, nt),
        in_specs=[
            pl.BlockSpec((tl, 2 * BRANCH_W), lambda i, t: (i * nt + t, ZB_CONV)),
            pl.BlockSpec((1, hist, BRANCH_W), lambda i, t: (i, 0, 0)),
            pl.BlockSpec((CONV_K, BRANCH_W), lambda i, t: (0, 0)),
            vec, vec, vec,
        ],
        out_specs=[
            pl.BlockSpec((tl, BRANCH_W), lambda i, t: (i * nt + t, 0)),
            pl.BlockSpec((1, hist, BRANCH_W), lambda i, t: (i, 0, 0)),
        ],
        out_shape=[SDS((nb * seq, BRANCH_W), F32), SDS((nb, hist, BRANCH_W), F32)],
        scratch_shapes=[pltpu.VMEM((_CONV_PAD + tl, BRANCH_W), F32), pltpu.VMEM((_CONV_PAD + tl, BRANCH_W), F32)],
        compiler_params=_params("arbitrary", "arbitrary"),
        name="conv_branch",
    )(z, buf, w, b, g, bt)


def _hgrn_kernel(q_ref, f_ref, i_ref, g_ref, s0_ref, lb_ref, ng_ref, y_ref, sn_ref,
                 st_ref, qs_ref, ks_ref, bs_ref, *, tl, c):
    t = pl.program_id(1)

    @pl.when(t == 0)
    def _():
        for h in range(HG_HEADS):
            st_ref[h] = s0_ref[0, h].T

    lb = jnp.clip(lb_ref[...], 0.0, 1.0)
    f = lb + (1.0 - lb) * _sigmoid(f_ref[...])
    lf = jnp.log(jnp.maximum(f, GATE_FLOOR))
    q = q_ref[...]
    qs_ref[...] = q * _sigmoid(q)
    ks_ref[...] = 1.0 - f
    r = lax.broadcasted_iota(jnp.int32, (tl, tl), 0)
    cc = lax.broadcasted_iota(jnp.int32, (tl, tl), 1)
    tri = jnp.where(cc <= r, jnp.where(cc >= (r & -c), 1.0, 0.0), 0.0).astype(F32)
    bs_ref[...] = jnp.dot(tri, lf, precision=HI, preferred_element_type=F32)
    rowid = lax.broadcasted_iota(jnp.int32, (c, 1), 0)

    def chunk(ci, carry):
        off = pl.multiple_of(ci * c, c)
        for h in range(HG_HEADS):
            sl = slice(h * HG_DK, (h + 1) * HG_DK)
            qc = qs_ref[pl.ds(off, c), sl]
            kc = ks_ref[pl.ds(off, c), sl]
            bc = bs_ref[pl.ds(off, c), sl]
            vc = i_ref[pl.ds(off, c), sl]
            o = jnp.zeros((c, HG_DK), F32)
            for s in range(c):
                k_row = kc[s:s + 1, :]
                b_row = bc[s:s + 1, :]
                v_row = vc[s:s + 1, :]
                p = qc * k_row * jnp.exp(jnp.minimum(bc - b_row, 0.0))
                a = jnp.sum(p, axis=1, keepdims=True)
                o = o + jnp.where(rowid >= s, a, 0.0) * v_row
            st = st_ref[h]
            qt = (qc * jnp.exp(bc)).astype(BF16)
            o = o + lax.dot_general(qt, st.astype(BF16), _NT, preferred_element_type=F32)
            b_last = bc[c - 1:c, :]
            kt = (kc * jnp.exp(b_last - bc)).astype(BF16)
            upd = lax.dot_general(vc.astype(BF16), kt, _TN, preferred_element_type=F32)
            st_ref[h] = st * jnp.exp(b_last) + upd
            o = _rms(o, ng_ref[...])
            y_ref[pl.ds(off, c), sl] = o * _sigmoid(g_ref[pl.ds(off, c), sl])
        return carry

    lax.fori_loop(0, tl // c, chunk, 0)

    @pl.when(t == pl.num_programs(1) - 1)
    def _():
        for h in range(HG_HEADS):
            sn_ref[0, h] = st_ref[h].T


def _hgrn_branch(z, s0, lb, ng, nb, seq, tl, c):
    nt = seq // tl
    w = HG_HEADS * HG_DK

    def col(j):
        return pl.BlockSpec((tl, w), lambda i, t: (i * nt + t, j))

    st_spec = pl.BlockSpec((1, HG_HEADS, HG_DK, HG_DK), lambda i, t: (i, 0, 0, 0))
    return pl.pallas_call(
        functools.partial(_hgrn_kernel, tl=tl, c=c),
        grid=(nb, nt),
        in_specs=[col(ZB_HQ), col(ZB_HF), col(ZB_HI), col(ZB_HG), st_spec,
                  pl.BlockSpec((1, w), lambda i, t: (0, 0)),
                  pl.BlockSpec((1, HG_DK), lambda i, t: (0, 0))],
        out_specs=[pl.BlockSpec((tl, w), lambda i, t: (i * nt + t, 0)), st_spec],
        out_shape=[SDS((nb * seq, w), F32), SDS((nb, HG_HEADS, HG_DK, HG_DK), F32)],
        scratch_shapes=[pltpu.VMEM((HG_HEADS, HG_DK, HG_DK), F32),
                        pltpu.VMEM((tl, w), F32), pltpu.VMEM((tl, w), F32), pltpu.VMEM((tl, w), F32)],
        compiler_params=_params("arbitrary", "arbitrary"),
        name="hgrn_branch",
    )(z, z, z, z, s0, lb, ng)


def _cumsum_kernel(lf_ref, ct_ref, carry_ref, *, tl):
    @pl.when(pl.program_id(1) == 0)
    def _():
        carry_ref[...] = jnp.zeros_like(carry_ref)

    r = lax.broadcasted_iota(jnp.int32, (tl, tl), 0)
    cc = lax.broadcasted_iota(jnp.int32, (tl, tl), 1)
    tri = jnp.where(cc <= r, 1.0, 0.0).astype(F32)
    cs = jnp.dot(tri, lf_ref[...], precision=HI, preferred_element_type=F32) + carry_ref[...]
    carry_ref[...] = cs[tl - 1:tl, :]
    ct_ref[0] = cs.T[:ATT_HEADS, :]


def _logf_cumsum(lf, nb, seq, tl):
    nt = seq // tl
    return pl.pallas_call(
        functools.partial(_cumsum_kernel, tl=tl),
        grid=(nb, nt),
        in_specs=[pl.BlockSpec((tl, LANES), lambda i, t: (i * nt + t, 0))],
        out_specs=pl.BlockSpec((1, ATT_HEADS, tl), lambda i, t: (i, 0, t)),
        out_shape=SDS((nb, ATT_HEADS, seq), F32),
        scratch_shapes=[pltpu.VMEM((1, LANES), F32)],
        compiler_params=_params("arbitrary", "arbitrary"),
        name="logf_cumsum",
    )(lf)


_LOG2E = 1.4426950408889634
_FOX_BLOCK = 1024
_FOX_SUB = 512


def _fox_kernel(qi_ref, ki_ref, q_ref, k_ref, v_ref, ck_ref, cq_ref, o_ref, qm_ref, m_ref, acc_ref, *, t):
    pair = pl.program_id(1)
    qi = qi_ref[pl.program_id(2)]
    ki = ki_ref[pl.program_id(2)]
    lane = lax.broadcasted_iota(jnp.int32, (1, LANES), 1)

    @pl.when(ki == 0)
    def _():
        q = q_ref[...] * (ATT_HD ** -0.5 * _LOG2E)
        qm_ref[0] = jnp.where(lane < ATT_HD, q, 0.0).astype(BF16)
        qm_ref[1] = jnp.where(lane >= ATT_HD, q, 0.0).astype(BF16)
        m_ref[...] = jnp.full_like(m_ref, MASK_VALUE)
        acc_ref[...] = jnp.zeros_like(acc_ref)

    def step(diagonal):
        k = k_ref[...].astype(BF16)
        v = jnp.concatenate([v_ref[...].astype(BF16), jnp.ones((t, LANES), BF16)], axis=1)
        head_iota = lax.broadcasted_iota(jnp.int32, (ATT_HEADS, 1), 0)
        sub = min(t, _FOX_SUB)
        biases, states = [], []
        for j in range(2):
            own = head_iota == 2 * pair + j
            ck = jnp.sum(jnp.where(own, ck_ref[0], 0.0), axis=0, keepdims=True)
            c0 = jnp.sum(jnp.where(own, cq_ref[0, :, 0:1], 0.0), axis=0, keepdims=True)
            biases.append((c0 - ck) * _LOG2E)
            states.append((qm_ref[j], m_ref[j], acc_ref[j]))

        def logits(item):
            j, r0 = item
            nk = r0 + sub if diagonal else t
            s = lax.dot_general(states[j][0][r0:r0 + sub], k[:nk], _NT, preferred_element_type=F32)
            s = s + biases[j][:, :nk]
            if diagonal:
                rid = lax.broadcasted_iota(jnp.int32, (sub, nk), 0) + r0
                cid = lax.broadcasted_iota(jnp.int32, (sub, nk), 1)
                s = jnp.where(cid <= rid, s, MASK_VALUE)
            return s, nk

        items = [(j, r0) for j in range(2) for r0 in range(0, t, sub)]
        m_out, acc_out = ([], []), ([], [])
        nxt = logits(items[0])
        for idx, (j, r0) in enumerate(items):
            s, nk = nxt
            if idx + 1 < len(items):
                nxt = logits(items[idx + 1])
            rows = slice(r0, r0 + sub)
            m_old = states[j][1][rows]
            m_new = jnp.maximum(m_old, jnp.max(s, axis=1, keepdims=True))
            alpha = jnp.exp2(m_old - m_new)
            p = jnp.exp2(s - m_new[:, 0:1]).astype(BF16)
            pv = jnp.dot(p, v[:nk], preferred_element_type=F32)
            acc_out[j].append(jnp.concatenate([alpha, alpha], axis=1) * states[j][2][rows] + pv)
            m_out[j].append(m_new)
        for j in range(2):
            acc_ref[j] = jnp.concatenate(acc_out[j], axis=0)
            m_ref[j] = jnp.concatenate(m_out[j], axis=0)

    @pl.when(ki < qi)
    def _():
        step(False)

    @pl.when(ki == qi)
    def _():
        step(True)
        o0 = acc_ref[0, :, :LANES] / acc_ref[0, :, LANES:]
        o1 = acc_ref[1, :, :LANES] / acc_ref[1, :, LANES:]
        o_ref[...] = jnp.where(lane < ATT_HD, o0, o1)


def _fox_prompt(z, ct, nb, seq, t):
    nq = seq // t
    npair = ATT_HEADS // 2
    zq, zk, zv = (zb * (BRANCH_W // LANES) for zb in (ZB_AQ, ZB_AK, ZB_AV))
    pairs = [(qi, ki) for qi in range(nq) for ki in range(qi + 1)]
    qis = jnp.array([p[0] for p in pairs], jnp.int32)
    kis = jnp.array([p[1] for p in pairs], jnp.int32)
    grid_spec = pltpu.PrefetchScalarGridSpec(
        num_scalar_prefetch=2,
        grid=(nb, npair, len(pairs)),
        in_specs=[
            pl.BlockSpec((t, LANES), lambda b, p, s, qt, kt: (b * nq + qt[s], zq + p)),
            pl.BlockSpec((t, LANES), lambda b, p, s, qt, kt: (b * nq + kt[s], zk + p)),
            pl.BlockSpec((t, LANES), lambda b, p, s, qt, kt: (b * nq + kt[s], zv + p)),
            pl.BlockSpec((1, ATT_HEADS, t), lambda b, p, s, qt, kt: (b, 0, kt[s])),
            pl.BlockSpec((1, ATT_HEADS, t), lambda b, p, s, qt, kt: (b, 0, qt[s])),
        ],
        out_specs=pl.BlockSpec((t, LANES), lambda b, p, s, qt, kt: (b * nq + qt[s], p)),
        scratch_shapes=[pltpu.VMEM((2, t, LANES), BF16), pltpu.VMEM((2, t, LANES), F32),
                        pltpu.VMEM((2, t, 2 * LANES), F32)],
    )
    return pl.pallas_call(
        functools.partial(_fox_kernel, t=t),
        grid_spec=grid_spec,
        out_shape=SDS((nb * seq, BRANCH_W), F32),
        compiler_params=_params("arbitrary", "arbitrary", "arbitrary"),
        name="fox_prompt",
    )(qis, kis, z, z, z, ct, ct)


def _page_suffix_kernel(lf_ref, suf_ref, tot_ref):
    r = lax.broadcasted_iota(jnp.int32, (PAGE_SIZE, PAGE_SIZE), 0)
    c = lax.broadcasted_iota(jnp.int32, (PAGE_SIZE, PAGE_SIZE), 1)
    lf = lf_ref[...]
    suf_ref[...] = jnp.dot(lf, jnp.where(r > c, 1.0, 0.0).astype(F32), precision=HI, preferred_element_type=F32)
    tot_ref[...] = jnp.dot(lf, jnp.ones((PAGE_SIZE, PAGE_SIZE), F32), precision=HI, preferred_element_type=F32)


def _page_suffix(lf_rows, tr):
    n = lf_rows.shape[0]
    spec = pl.BlockSpec((tr, PAGE_SIZE), lambda i: (i, 0))
    return pl.pallas_call(
        _page_suffix_kernel,
        grid=(n // tr,),
        in_specs=[spec],
        out_specs=[spec, spec],
        out_shape=[SDS((n, PAGE_SIZE), F32), SDS((n, PAGE_SIZE), F32)],
        compiler_params=_params("arbitrary"),
        name="page_suffix",
    )(lf_rows)


_PAGES_PER_STEP = 8


def _fox_sample_kernel(pt_ref, q_ref, kn_ref, vn_ref, lfn_ref, *rest, g, tt):
    del pt_ref
    kp_refs, vp_refs, sf_refs, tot_refs = rest[:g], rest[g:2 * g], rest[2 * g:3 * g], rest[3 * g:4 * g]
    o_ref, qb_ref, m_ref, l_ref, acc_ref, car_ref, cn_ref = rest[4 * g:]
    step = pl.program_id(1)
    rows = tt * ATT_HEADS
    w = ATT_HEADS * ATT_HD
    rid = lax.broadcasted_iota(jnp.int32, (rows, 1), 0)
    hmask = (lax.broadcasted_iota(jnp.int32, (rows, w), 1) >> _HD_SHIFT) == (rid & (ATT_HEADS - 1))
    tok = rid >> _HEAD_SHIFT
    lane = lax.broadcasted_iota(jnp.int32, (rows, PAGE_SIZE), 1)

    def tile_rows(x):
        return jnp.broadcast_to(x[None], (tt,) + x.shape).reshape(rows, x.shape[-1])

    def online(scores, values, dims):
        m_old = m_ref[...]
        m_new = m_old
        for s in scores:
            m_new = jnp.maximum(m_new, jnp.max(s, axis=1, keepdims=True))
        alpha = jnp.exp(m_old - m_new)
        l_new = alpha * l_ref[...]
        acc = alpha * acc_ref[...]
        for s, v in zip(scores, values):
            p = jnp.exp(s - m_new)
            l_new = l_new + jnp.sum(p, axis=1, keepdims=True)
            acc = acc + lax.dot_general(p.astype(BF16), v, dims, preferred_element_type=F32)
        l_ref[...] = l_new
        acc_ref[...] = acc
        m_ref[...] = m_new

    @pl.when(step == 0)
    def _():
        q = q_ref[...] * (ATT_HD ** -0.5)
        qrep = jnp.broadcast_to(q[:, None, :], (tt, ATT_HEADS, w)).reshape(rows, w)
        qb_ref[...] = jnp.where(hmask, qrep, 0.0).astype(BF16)
        m_ref[...] = jnp.full_like(m_ref, MASK_VALUE)
        l_ref[...] = jnp.zeros_like(l_ref)
        acc_ref[...] = jnp.zeros_like(acc_ref)
        car_ref[...] = jnp.zeros_like(car_ref)
        r2 = lax.broadcasted_iota(jnp.int32, (PAGE_SIZE, PAGE_SIZE), 0)
        c2 = lax.broadcasted_iota(jnp.int32, (PAGE_SIZE, PAGE_SIZE), 1)
        incl = jnp.where(r2 <= c2, 1.0, 0.0).astype(F32)
        cn_ref[...] = jnp.dot(tile_rows(lfn_ref[0]), incl, precision=HI, preferred_element_type=F32)

    qb = qb_ref[...]
    cn = cn_ref[...]
    cn_own = jnp.sum(jnp.where(lane == tok, cn, 0.0), axis=1, keepdims=True)
    car = car_ref[...]
    scores = []
    for i in range(g):
        kt = kp_refs[i][0].reshape(w, PAGE_SIZE).astype(BF16)
        s = jnp.dot(qb, kt, preferred_element_type=F32)
        scores.append(s + ((cn_own + car) + tile_rows(sf_refs[i][0])))
        car = car + tile_rows(tot_refs[i][0])[:, 0:1]
    car_ref[...] = car
    online(scores, [vp_refs[i][0].reshape(w, PAGE_SIZE).astype(BF16) for i in range(g)], _NT)

    @pl.when(step == pl.num_programs(1) - 1)
    def _():
        pad = jnp.zeros((PAGE_SIZE - tt, w), F32)
        kn = jnp.concatenate([kn_ref[...], pad], axis=0).astype(BF16)
        vn = jnp.concatenate([vn_ref[...], pad], axis=0).astype(BF16)
        sn = lax.dot_general(qb, kn, _NT, preferred_element_type=F32)
        sn = jnp.where(lane <= tok, sn + (cn_own - cn), MASK_VALUE)
        online([sn], [vn], (((1,), (0,)), ((), ())))
        o = jnp.where(hmask, acc_ref[...] / l_ref[...], 0.0)
        o_ref[...] = jnp.sum(o.reshape(tt, ATT_HEADS, w), axis=1)


def _fox_sample(z, lfn_t, ckt, cvt, suffix, total, page_table, page_base, nb, tt):
    npg = page_table.shape[1]
    g = _PAGES_PER_STEP
    assert npg % g == 0
    w = ATT_HEADS * ATT_HD
    rows = tt * ATT_HEADS

    def page(i, ndim):
        def index(b, p, pt):
            return (page_base + pt[b * npg + (npg - 1 - (p * g + i))],) + (0,) * (ndim - 1)
        return index

    in_specs = [pl.BlockSpec((tt, w), lambda b, p, pt, zb=zb: (b, zb)) for zb in (ZB_AQ, ZB_AK, ZB_AV)]
    in_specs += [pl.BlockSpec((1, ATT_HEADS, PAGE_SIZE), lambda b, p, pt: (b, 0, 0))]
    in_specs += [pl.BlockSpec((1, ATT_HEADS, ATT_HD, PAGE_SIZE), page(i, 4)) for i in range(g)] * 2
    in_specs += [pl.BlockSpec((1, ATT_HEADS, PAGE_SIZE), page(i, 3)) for i in range(g)] * 2
    grid_spec = pltpu.PrefetchScalarGridSpec(
        num_scalar_prefetch=1,
        grid=(nb, npg // g),
        in_specs=in_specs,
        out_specs=pl.BlockSpec((tt, w), lambda b, p, pt: (b, 0)),
        scratch_shapes=[pltpu.VMEM((rows, w), BF16), pltpu.VMEM((rows, 1), F32), pltpu.VMEM((rows, 1), F32),
                        pltpu.VMEM((rows, w), F32), pltpu.VMEM((rows, 1), F32), pltpu.VMEM((rows, PAGE_SIZE), F32)],
    )
    return pl.pallas_call(
        functools.partial(_fox_sample_kernel, g=g, tt=tt),
        grid_spec=grid_spec,
        out_shape=SDS((nb * tt, w), F32),
        compiler_params=_params("arbitrary", "arbitrary"),
        name="fox_sample",
    )(page_table.reshape(-1), z, z, z, lfn_t, *([ckt] * g), *([cvt] * g), *([suffix] * g), *([total] * g))


def _merge_kernel(x_ref, yc_ref, yh_ref, ya_ref, gt_ref, wb_ref, wo_ref, g2_ref, h_ref, xnt_ref):
    merged = jnp.zeros(x_ref.shape, F32)
    for b, y_ref in enumerate((yc_ref, yh_ref, ya_ref)):
        proj = jnp.dot(y_ref[...].astype(BF16), wb_ref[b], preferred_element_type=F32)
        merged = merged + _sigmoid(gt_ref[:, b * D_MODEL:(b + 1) * D_MODEL]) * proj
    h = x_ref[...] + jnp.dot(merged.astype(BF16), wo_ref[...], preferred_element_type=F32)
    h_ref[...] = h
    xnt = _rms(h, g2_ref[...]).T.astype(BF16)
    for k in range(xnt_ref.shape[0]):
        xnt_ref[k] = xnt[:, k * _PEER_T:(k + 1) * _PEER_T]


_PEER_T = 256


def _merge(x, yc, yh, ya, z, wb, wo, g2, tm):
    n, d = x.shape
    assert tm % _PEER_T == 0
    br = pl.BlockSpec((tm, BRANCH_W), lambda i: (i, 0))
    return pl.pallas_call(
        _merge_kernel,
        grid=(n // tm,),
        in_specs=[
            pl.BlockSpec((tm, d), lambda i: (i, 0)), br, br, br,
            pl.BlockSpec((tm, 3 * d), lambda i: (i, 0)),
            pl.BlockSpec((3, BRANCH_W, d), lambda i: (0, 0, 0)),
            pl.BlockSpec((d, d), lambda i: (0, 0)),
            pl.BlockSpec((1, d), lambda i: (0, 0)),
        ],
        out_specs=[pl.BlockSpec((tm, d), lambda i: (i, 0)),
                   pl.BlockSpec((tm // _PEER_T, d, _PEER_T), lambda i: (i, 0, 0))],
        out_shape=[SDS((n, d), F32), SDS((n // _PEER_T, d, _PEER_T), BF16)],
        compiler_params=_params("arbitrary"),
        name="merge",
    )(x, yc, yh, ya, z, wb, wo, g2)


def _take_max(work, row_iota, n_rows):
    m = jnp.max(work, axis=0, keepdims=True)
    first = jnp.min(jnp.where(work == m, row_iota, float(n_rows)), axis=0, keepdims=True)
    return m, jnp.where(row_iota == first, -jnp.inf, work)


def _oddeven_merge_sort_pairs(n):
    pairs, p = [], 1
    while p < n:
        k = p
        while k >= 1:
            for j in range(k % p, n - k, 2 * k):
                for i in range(min(k, n - j - k)):
                    if (i + j) // (2 * p) == (i + j + k) // (2 * p):
                        pairs.append((i + j, i + j + k))
            k //= 2
        p *= 2
    return tuple(pairs)


_SORT_PAIRS = _oddeven_merge_sort_pairs(PEER_TOPK)


def _exchange(v, a, b):
    v[a], v[b] = jnp.maximum(v[a], v[b]), jnp.minimum(v[a], v[b])


def _top16_sorted(s):
    v = [s[_SUBLANES * i:_SUBLANES * (i + 1), :] for i in range(PEER_TOPK)]
    for a, b in _SORT_PAIRS:
        _exchange(v, a, b)
    for shift in (4, 6, 7):
        w = [pltpu.roll(x, shift, 0) for x in v]
        v = [jnp.maximum(v[k], w[PEER_TOPK - 1 - k]) for k in range(PEER_TOPK)]
        d = PEER_TOPK // 2
        while d >= 1:
            for i in range(PEER_TOPK):
                if not i & d:
                    _exchange(v, i, i + d)
            d //= 2
    return [x[0:1, :] for x in v]


_N_CAND = -(-len(_PAIRS) // 8) * 8


def _peer_route_kernel(xnt_ref, wqt_ref, keys_ref, cut1_ref, e1_ref, rank2_ref, e2_ref, cand_ref, *, t):
    xnt = xnt_ref[0]
    n_cand = _N_CAND
    cand_iota = lax.broadcasted_iota(jnp.int32, (n_cand, t), 0).astype(F32)
    cand_ref[len(_PAIRS):, :] = jnp.full((n_cand - len(_PAIRS), t), -jnp.inf, F32)

    def head(h, carry):
        dk = 2 * N_KEYS
        qt = jnp.dot(wqt_ref[pl.ds(pl.multiple_of(h * dk, dk), dk), :], xnt, preferred_element_type=F32)
        scores, tops = [], []
        for p in range(2):
            s = jnp.dot(keys_ref[h, p], qt[p * N_KEYS:(p + 1) * N_KEYS].astype(BF16), preferred_element_type=F32)
            scores.append(s)
            tops.append(_top16_sorted(s))
        for i, (a, b) in enumerate(_PAIRS):
            cand_ref[i:i + 1, :] = tops[0][a] + tops[1][b]
        work = cand_ref[...]
        best = []
        for _ in range(PEER_TOPK):
            m, work = _take_max(work, cand_iota, n_cand)
            best.append(m)
        norm = jnp.zeros((1, t), F32)
        for m in best:
            norm = norm + jnp.exp(m - best[0])
        cut = jnp.zeros((N_KEYS, t), F32)
        rank = jnp.full((N_KEYS, t), float(PEER_TOPK), F32)
        for b in range(PEER_TOPK):
            cut = jnp.where(scores[0] + tops[1][b] >= best[-1], float(b + 1), cut)
            rb = PEER_TOPK - 1 - b
            rank = jnp.where(scores[1] >= tops[1][rb], float(rb), rank)
        cut1_ref[0, h] = cut
        e1_ref[0, h] = jnp.exp(scores[0] - tops[0][0]) / norm
        rank2_ref[0, h] = rank.astype(BF16)
        e2_ref[0, h] = jnp.exp(scores[1] - tops[1][0]).astype(BF16)
        return carry

    lax.fori_loop(0, PEER_HEADS, head, 0)


def _peer_route(xnt, wqt, keys):
    nt, d, t = xnt.shape
    big = pl.BlockSpec((1, PEER_HEADS, N_KEYS, t), lambda i: (i, 0, 0, 0))
    wide, narrow = SDS((nt, PEER_HEADS, N_KEYS, t), F32), SDS((nt, PEER_HEADS, N_KEYS, t), BF16)
    return pl.pallas_call(
        functools.partial(_peer_route_kernel, t=t),
        grid=(nt,),
        in_specs=[
            pl.BlockSpec((1, d, t), lambda i: (i, 0, 0)),
            pl.BlockSpec(wqt.shape, lambda i: (0, 0)),
            pl.BlockSpec(keys.shape, lambda i: (0, 0, 0, 0)),
        ],
        out_specs=[big, big, big, big],
        out_shape=[wide, wide, narrow, narrow],
        scratch_shapes=[pltpu.VMEM((_N_CAND, t), F32)],
        compiler_params=_params("arbitrary"),
        name="peer_route",
    )(xnt, wqt, keys)


_BF16_ROWS = 16
_PEER_CHUNK = 2048


def _peer_dense_kernel(xnt_ref, h_ref, u_ref, vt_ref, cut1_ref, e1_ref, rank2_ref, e2_ref, o_ref,
                       acc_ref, act_ref, cutx_ref, e1x_ref, rank2x_ref, e2x_ref, *, t, ec):
    c = pl.program_id(1)
    tile = pl.program_id(2)
    n_chunks = pl.num_programs(1) - 1

    @pl.when(c == 0)
    def _():
        acc_ref[tile] = jnp.zeros(acc_ref.shape[1:], F32)
        act_ref[tile, 1] = jnp.zeros((ec, t), BF16)
        rank2x_ref[tile] = rank2_ref[tile]
        e2x_ref[tile] = e2_ref[tile]

    acc_ref[tile] += jnp.dot(vt_ref[...], act_ref[tile, (c + 1) & 1], preferred_element_type=F32)

    chunk = jnp.minimum(c, n_chunks - 1)
    n_i1 = ec // N_KEYS
    slot = c & 1
    rep = N_KEYS // _BF16_ROWS
    for ii in range(n_i1):
        row = pl.ds(chunk * n_i1 + ii, 1)
        for h in range(PEER_HEADS):
            cutx_ref[h, ii] = jnp.broadcast_to(cut1_ref[tile, h, row, :], (_BF16_ROWS, t)).astype(BF16)
            e1x_ref[h, ii] = jnp.broadcast_to(e1_ref[tile, h, row, :], (_BF16_ROWS, t)).astype(BF16)
    xnt = xnt_ref[tile]
    for ii in range(n_i1):
        rows = slice(ii * N_KEYS, (ii + 1) * N_KEYS)
        hid = jnp.dot(u_ref[rows, :], xnt, preferred_element_type=F32)
        wsum = jnp.zeros((N_KEYS, t), BF16)
        for h in range(PEER_HEADS):
            cut = jnp.concatenate([cutx_ref[h, ii]] * rep, axis=0)
            e1 = jnp.concatenate([e1x_ref[h, ii]] * rep, axis=0)
            wsum = wsum + jnp.where(rank2x_ref[tile, h] < cut, e1 * e2x_ref[tile, h], jnp.zeros((), BF16))
        gelu = 0.5 * hid * (1.0 + lax.erf(hid * (2.0 ** -0.5)))
        act_ref[tile, slot, rows, :] = gelu.astype(BF16) * wsum

    @pl.when(c == n_chunks)
    def _():
        tok = pl.ds(pl.multiple_of(tile * t, t), t)
        o_ref[tok, :] = h_ref[tok, :] + acc_ref[tile].T


_PEER_GROUP = 2


def _peer_dense(xnt, h, u, vt, route, ec):
    nt, d, t = xnt.shape
    g = _PEER_GROUP if nt % _PEER_GROUP == 0 else 1
    nc = u.shape[0] // ec
    n_i1 = ec // N_KEYS
    big = pl.BlockSpec((g, PEER_HEADS, N_KEYS, t), lambda i, c, k: (i, 0, 0, 0))
    tokens = pl.BlockSpec((g * t, d), lambda i, c, k: (i, 0))
    return pl.pallas_call(
        functools.partial(_peer_dense_kernel, t=t, ec=ec),
        grid=(nt // g, nc + 1, g),
        in_specs=[
            pl.BlockSpec((g, d, t), lambda i, c, k: (i, 0, 0)),
            tokens,
            pl.BlockSpec((ec, d), lambda i, c, k: (jnp.minimum(c, nc - 1), 0)),
            pl.BlockSpec((d, ec), lambda i, c, k: (0, jnp.maximum(c - 1, 0))),
            big, big, big, big,
        ],
        out_specs=tokens,
        out_shape=SDS((nt * t, d), F32),
        scratch_shapes=[pltpu.VMEM((g, d, t), F32), pltpu.VMEM((g, 2, ec, t), BF16),
                        pltpu.VMEM((PEER_HEADS, n_i1, _BF16_ROWS, t), BF16),
                        pltpu.VMEM((PEER_HEADS, n_i1, _BF16_ROWS, t), BF16),
                        pltpu.VMEM((g, PEER_HEADS, N_KEYS, t), BF16), pltpu.VMEM((g, PEER_HEADS, N_KEYS, t), BF16)],
        compiler_params=_params("arbitrary", "arbitrary", "arbitrary", vmem_mb=56),
        name="peer_dense",
    )(xnt, h, u, vt, *route)


def _final_norm_kernel(x_ref, g_ref, o_ref):
    o_ref[...] = _rms(x_ref[...], g_ref[...])


def _final_norm(x, g, tm):
    n, d = x.shape
    return pl.pallas_call(
        _final_norm_kernel,
        grid=(n // tm,),
        in_specs=[pl.BlockSpec((tm, d), lambda i: (i, 0)), pl.BlockSpec((1, d), lambda i: (0, 0))],
        out_specs=pl.BlockSpec((tm, d), lambda i: (i, 0)),
        out_shape=SDS((n, d), F32),
        compiler_params=_params("arbitrary"),
        name="final_norm",
    )(x, g)


def _tile(n, pref):
    return pref if n % pref == 0 else n


def _layer_weights(l, lbs, norm1_g, w_in, conv_dw, conv_db, conv_ln_g, conv_ln_b, hg_norm_g, att_fb,
                   w_branch, w_out, norm2_g, peer_wq, peer_keys, peer_u, peer_v):
    n_main = 2 * BRANCH_W + 4 * HG_HEADS * HG_DK + 3 * ATT_HEADS * ATT_HD
    w = w_in[l]
    row = lambda a: a.reshape(1, -1).astype(F32)
    return dict(
        g1=row(norm1_g[l]),
        w_main=jnp.concatenate([w[:, n_main + ATT_HEADS:], w[:, :n_main]], axis=1).astype(BF16),
        w_f=jnp.pad(w[:, n_main:n_main + ATT_HEADS], ((0, 0), (0, LANES - ATT_HEADS))).astype(BF16),
        fb=jnp.pad(att_fb[l].astype(F32), (0, LANES - ATT_HEADS)).reshape(1, LANES),
        conv_w=conv_dw[l].astype(F32), conv_b=row(conv_db[l]), ln_g=row(conv_ln_g[l]), ln_b=row(conv_ln_b[l]),
        lb=row(lbs[l]), hg_g=row(hg_norm_g[l]),
        wb=w_branch[l].astype(BF16), wo=w_out[l].astype(BF16), g2=row(norm2_g[l]),
        wqt=peer_wq[l].T.astype(BF16), keys=peer_keys[l].astype(BF16),
        u=peer_u[l].astype(BF16), vt=peer_v[l].T.astype(BF16),
    )


def _token_mixers(x, wt, nb, seq, conv_buf, hg_state, attend):
    n = nb * seq
    tm = _tile(n, 1024)
    w = ATT_HEADS * ATT_HD
    if seq % tm == 0:
        z, lf, kt, vt = _in_proj(x, wt["g1"], wt["w_main"], wt["w_f"], wt["fb"], tm, seq)
        k_rows, v_rows = (a.reshape(nb, ATT_HEADS, ATT_HD, seq).transpose(0, 3, 1, 2) for a in (kt, vt))
    else:
        z, lf = _in_proj(x, wt["g1"], wt["w_main"], wt["w_f"], wt["fb"], tm)
        k_rows, v_rows = (z[:, zb * BRANCH_W:zb * BRANCH_W + w].reshape(nb, seq, ATT_HEADS, ATT_HD)
                          for zb in (ZB_AK, ZB_AV))
    y_conv, conv_new = _conv_branch(z, conv_buf, wt["conv_w"], wt["conv_b"], wt["ln_g"], wt["ln_b"],
                                    nb, seq, _tile(seq, 512))
    y_hgrn, hg_new = _hgrn_branch(z, hg_state, wt["lb"], wt["hg_g"], nb, seq, _tile(seq, 128), _tile(seq, 16))
    y_att = attend(z, lf)
    h, xn = _merge(x, y_conv, y_hgrn, y_att, z, wt["wb"], wt["wo"], wt["g2"], _tile(n, 512))
    route = _peer_route(xn, wt["wqt"], wt["keys"])
    out = _peer_dense(xn, h, wt["u"], wt["vt"], route, _PEER_CHUNK)
    logf = lf[:, :ATT_HEADS].reshape(nb, seq, ATT_HEADS)
    return out, conv_new, hg_new, k_rows, v_rows, logf


def kernel(x_prompt, x_sample, cache_k, cache_v, cache_logf, state_conv, state_hgrn, page_table, norm1_g, w_in, conv_dw, conv_db, conv_ln_g, conv_ln_b, hg_lb_logits, hg_norm_g, att_fb, w_branch, w_out, norm2_g, peer_wq, peer_keys, peer_u, peer_v, final_g):
    depth = w_in.shape[0]
    bp, sp, d = x_prompt.shape
    bs, ss, _ = x_sample.shape
    w = ATT_HEADS * ATT_HD
    probs = jax.nn.softmax(hg_lb_logits.astype(F32), axis=0)
    lbs = jnp.cumsum(probs, axis=0) - probs[0:1]
    n_phys = cache_k.shape[1]
    ckt = jnp.transpose(cache_k, (0, 1, 3, 4, 2)).reshape(depth * n_phys, ATT_HEADS, ATT_HD, PAGE_SIZE)
    cvt = jnp.transpose(cache_v, (0, 1, 3, 4, 2)).reshape(depth * n_phys, ATT_HEADS, ATT_HD, PAGE_SIZE)
    lf_rows = jnp.swapaxes(cache_logf.astype(F32), 2, 3).reshape(depth * n_phys * ATT_HEADS, PAGE_SIZE)
    suffix, total = (a.reshape(depth * n_phys, ATT_HEADS, PAGE_SIZE)
                     for a in _page_suffix(lf_rows, _tile(lf_rows.shape[0], 2048)))

    hp = x_prompt.reshape(bp * sp, d)
    hs = x_sample.reshape(bs * ss, d)
    outs_p, outs_s = [], []
    for l in range(depth):
        wt = _layer_weights(l, lbs, norm1_g, w_in, conv_dw, conv_db, conv_ln_g, conv_ln_b, hg_norm_g, att_fb,
                            w_branch, w_out, norm2_g, peer_wq, peer_keys, peer_u, peer_v)

        def attend_prompt(z, lf):
            ct = _logf_cumsum(lf, bp, sp, _tile(sp, 512))
            return _fox_prompt(z, ct, bp, sp, _tile(sp, _FOX_BLOCK))

        def attend_sample(z, lf, l=l):
            lfn_t = jnp.swapaxes(lf[:, :ATT_HEADS].reshape(bs, ss, ATT_HEADS), 1, 2)
            lfn_t = jnp.pad(lfn_t, ((0, 0), (0, 0), (0, PAGE_SIZE - ss)))
            return _fox_sample(z, lfn_t, ckt, cvt, suffix, total, page_table, l * n_phys, bs, ss)

        hp, *rp = _token_mixers(hp, wt, bp, sp, jnp.zeros((bp, CONV_K - 1, BRANCH_W), F32),
                                jnp.zeros((bp, HG_HEADS, HG_DK, HG_DK), F32), attend_prompt)
        hs, *rs = _token_mixers(hs, wt, bs, ss, state_conv[l], state_hgrn[l], attend_sample)
        outs_p.append(rp)
        outs_s.append(rs)

    fg = final_g.reshape(1, d).astype(F32)
    y_prompt = _final_norm(hp, fg, _tile(bp * sp, 1024)).reshape(bp, sp, d)
    y_sample = _final_norm(hs, fg, _tile(bs * ss, 1024)).reshape(bs, ss, d)
    stack = lambda outs, i: jnp.stack([r[i] for r in outs])
    return (y_prompt, y_sample,
            stack(outs_p, 2), stack(outs_p, 3), stack(outs_p, 4), stack(outs_p, 0), stack(outs_p, 1),
            stack(outs_s, 2), stack(outs_s, 3), stack(outs_s, 4), stack(outs_s, 0), stack(outs_s, 1))
```

```python
import functools

import jax
import jax.numpy as jnp
from jax import lax
from jax.experimental import pallas as pl
from jax.experimental.pallas import tpu as pltpu

F32 = jnp.float32
BF16 = jnp.bfloat16
HI = lax.Precision.HIGHEST
SDS = jax.ShapeDtypeStruct

D_MODEL = 1024
BRANCH_W = 512
CONV_K = 31
HG_HEADS = 4
HG_DK = 128
ATT_HEADS = 8
ATT_HD = 64
PEER_HEADS = 8
N_KEYS = 128
PEER_TOPK = 16
PAGE_SIZE = 128
EPS = 1e-6
GATE_FLOOR = 1e-20
MASK_VALUE = -1e30
LANES = 128
_SUBLANES = 8
_HD_SHIFT = ATT_HD.bit_length() - 1
_HEAD_SHIFT = ATT_HEADS.bit_length() - 1

Z_COLS = 15 * BRANCH_W
ZB_CONV = 3
ZB_HQ, ZB_HF, ZB_HI, ZB_HG, ZB_AQ, ZB_AK, ZB_AV = 8, 9, 10, 11, 12, 13, 14

_PAIRS = tuple((a, b) for a in range(PEER_TOPK) for b in range(PEER_TOPK) if (a + 1) * (b + 1) <= PEER_TOPK)

_NT = (((1,), (1,)), ((), ()))
_TN = (((0,), (0,)), ((), ()))


def _params(*sem, vmem_mb=48):
    return pltpu.CompilerParams(dimension_semantics=sem, vmem_limit_bytes=vmem_mb * 1024 * 1024)


def _sigmoid(x):
    return 1.0 / (1.0 + jnp.exp(-x))


def _log_sigmoid(x):
    return jnp.minimum(x, 0.0) - jnp.log1p(jnp.exp(-jnp.abs(x)))


def _rms(x, g):
    return x * lax.rsqrt(jnp.mean(x * x, axis=-1, keepdims=True) + EPS) * g


def _in_proj_kernel(x_ref, g_ref, w_ref, wf_ref, fb_ref, z_ref, lf_ref, *rest):
    xn_ref = rest[-1]

    @pl.when(pl.program_id(1) == 0)
    def _():
        xn = _rms(x_ref[...], g_ref[...]).astype(BF16)
        xn_ref[...] = xn
        af = jnp.dot(xn, wf_ref[...], preferred_element_type=F32) + fb_ref[...]
        lf_ref[...] = _log_sigmoid(af)

    z = jnp.dot(xn_ref[...], w_ref[...], preferred_element_type=F32)
    z_ref[...] = z

    if len(rest) == 3:
        @pl.when(pl.program_id(1) == pl.num_programs(1) - 1)
        def _():
            kt_ref, vt_ref = rest[:2]
            kt_ref[0] = z[:, BRANCH_W:2 * BRANCH_W].T
            vt_ref[0] = z[:, 2 * BRANCH_W:].T


def _in_proj(x, g, w_main, w_f, fb, tm, seq=None):
    n, d = x.shape
    tn = 3 * BRANCH_W
    assert ZB_AK * BRANCH_W == Z_COLS - 2 * BRANCH_W and ZB_AV * BRANCH_W == Z_COLS - BRANCH_W
    out_specs = [pl.BlockSpec((tm, tn), lambda i, j: (i, j)), pl.BlockSpec((tm, LANES), lambda i, j: (i, 0))]
    out_shape = [SDS((n, Z_COLS), F32), SDS((n, LANES), F32)]
    if seq is not None:
        nlb = seq // tm
        tr = pl.BlockSpec((1, BRANCH_W, tm), lambda i, j: (i // nlb, 0, i % nlb))
        out_specs += [tr, tr]
        out_shape += [SDS((n // seq, BRANCH_W, seq), F32)] * 2
    return pl.pallas_call(
        _in_proj_kernel,
        grid=(n // tm, Z_COLS // tn),
        in_specs=[
            pl.BlockSpec((tm, d), lambda i, j: (i, 0)),
            pl.BlockSpec((1, d), lambda i, j: (0, 0)),
            pl.BlockSpec((d, tn), lambda i, j: (0, j)),
            pl.BlockSpec((d, LANES), lambda i, j: (0, 0)),
            pl.BlockSpec((1, LANES), lambda i, j: (0, 0)),
        ],
        out_specs=out_specs,
        out_shape=out_shape,
        scratch_shapes=[pltpu.VMEM((tm, d), BF16)],
        compiler_params=_params("arbitrary", "arbitrary"),
        name="in_proj",
    )(x, g, w_main, w_f, fb)


_CONV_PAD = 32


def _conv_kernel(a_ref, buf_ref, w_ref, b_ref, g_ref, bt_ref, y_ref, nc_ref, ext_ref, sh_ref, *, tl):
    hist = CONV_K - 1
    lo = _CONV_PAD - hist

    @pl.when(pl.program_id(1) == 0)
    def _():
        ext_ref[lo:_CONV_PAD, :] = buf_ref[0]

    a = a_ref[...]
    ext_ref[_CONV_PAD:_CONV_PAD + tl, :] = a[:, :BRANCH_W] * _sigmoid(a[:, BRANCH_W:])
    acc = jnp.zeros((tl, BRANCH_W), F32)
    for r in range(_SUBLANES):
        taps = [j for j in range(CONV_K) if (lo + j) % _SUBLANES == r]
        if not taps:
            continue
        last = (lo + taps[-1]) - r
        sh_ref[0:last + tl, :] = ext_ref[r:r + last + tl, :]
        for j in taps:
            q0 = lo + j - r
            acc = acc + w_ref[j:j + 1, :] * sh_ref[q0:q0 + tl, :]
    y = acc + b_ref[...]
    mu = jnp.mean(y, axis=-1, keepdims=True)
    yc = y - mu
    var = jnp.mean(yc * yc, axis=-1, keepdims=True)
    y = yc * lax.rsqrt(var + EPS) * g_ref[...] + bt_ref[...]
    y_ref[...] = y * _sigmoid(y)
    tail = ext_ref[tl + lo:tl + _CONV_PAD, :]
    nc_ref[0] = tail
    ext_ref[lo:_CONV_PAD, :] = tail


def _conv_branch(z, buf, w, b, g, bt, nb, seq, tl):
    nt = seq // tl
    hist = CONV_K - 1
    vec = pl.BlockSpec((1, BRANCH_W), lambda i, t: (0, 0))
    return pl.pallas_call(
        functools.partial(_conv_kernel, tl=tl),
        grid=(nb, nt),
        in_specs=[
            pl.BlockSpec((tl, 2 * BRANCH_W), lambda i, t: (i * nt + t, ZB_CONV)),
            pl.BlockSpec((1, hist, BRANCH_W), lambda i, t: (i, 0, 0)),
            pl.BlockSpec((CONV_K, BRANCH_W), lambda i, t: (0, 0)),
            vec, vec, vec,
        ],
        out_specs=[
            pl.BlockSpec((tl, BRANCH_W), lambda i, t: (i * nt + t, 0)),
            pl.BlockSpec((1, hist, BRANCH_W), lambda i, t: (i, 0, 0)),
        ],
        out_shape=[SDS((nb * seq, BRANCH_W), F32), SDS((nb, hist, BRANCH_W), F32)],
        scratch_shapes=[pltpu.VMEM((_CONV_PAD + tl, BRANCH_W), F32), pltpu.VMEM((_CONV_PAD + tl, BRANCH_W), F32)],
        compiler_params=_params("arbitrary", "arbitrary"),
        name="conv_branch",
    )(z, buf, w, b, g, bt)


def _hgrn_kernel(q_ref, f_ref, i_ref, g_ref, s0_ref, lb_ref, ng_ref, y_ref, sn_ref,
                 st_ref, qs_ref, ks_ref, bs_ref, *, tl, c):
    t = pl.program_id(1)

    @pl.when(t == 0)
    def _():
        for h in range(HG_HEADS):
            st_ref[h] = s0_ref[0, h].T

    lb = jnp.clip(lb_ref[...], 0.0, 1.0)
    f = lb + (1.0 - lb) * _sigmoid(f_ref[...])
    lf = jnp.log(jnp.maximum(f, GATE_FLOOR))
    q = q_ref[...]
    qs_ref[...] = q * _sigmoid(q)
    ks_ref[...] = 1.0 - f
    r = lax.broadcasted_iota(jnp.int32, (tl, tl), 0)
    cc = lax.broadcasted_iota(jnp.int32, (tl, tl), 1)
    tri = jnp.where(cc <= r, jnp.where(cc >= (r & -c), 1.0, 0.0), 0.0).astype(F32)
    bs_ref[...] = jnp.dot(tri, lf, precision=HI, preferred_element_type=F32)
    rowid = lax.broadcasted_iota(jnp.int32, (c, 1), 0)

    def chunk(ci, carry):
        off = pl.multiple_of(ci * c, c)
        for h in range(HG_HEADS):
            sl = slice(h * HG_DK, (h + 1) * HG_DK)
            qc = qs_ref[pl.ds(off, c), sl]
            kc = ks_ref[pl.ds(off, c), sl]
            bc = bs_ref[pl.ds(off, c), sl]
            vc = i_ref[pl.ds(off, c), sl]
            o = jnp.zeros((c, HG_DK), F32)
            for s in range(c):
                k_row = kc[s:s + 1, :]
                b_row = bc[s:s + 1, :]
                v_row = vc[s:s + 1, :]
                p = qc * k_row * jnp.exp(jnp.minimum(bc - b_row, 0.0))
                a = jnp.sum(p, axis=1, keepdims=True)
                o = o + jnp.where(rowid >= s, a, 0.0) * v_row
            st = st_ref[h]
            qt = (qc * jnp.exp(bc)).astype(BF16)
            o = o + lax.dot_general(qt, st.astype(BF16), _NT, preferred_element_type=F32)
            b_last = bc[c - 1:c, :]
            kt = (kc * jnp.exp(b_last - bc)).astype(BF16)
            upd = lax.dot_general(vc.astype(BF16), kt, _TN, preferred_element_type=F32)
            st_ref[h] = st * jnp.exp(b_last) + upd
            o = _rms(o, ng_ref[...])
            y_ref[pl.ds(off, c), sl] = o * _sigmoid(g_ref[pl.ds(off, c), sl])
        return carry

    lax.fori_loop(0, tl // c, chunk, 0)

    @pl.when(t == pl.num_programs(1) - 1)
    def _():
        for h in range(HG_HEADS):
            sn_ref[0, h] = st_ref[h].T


def _hgrn_branch(z, s0, lb, ng, nb, seq, tl, c):
    nt = seq // tl
    w = HG_HEADS * HG_DK

    def col(j):
        return pl.BlockSpec((tl, w), lambda i, t: (i * nt + t, j))

    st_spec = pl.BlockSpec((1, HG_HEADS, HG_DK, HG_DK), lambda i, t: (i, 0, 0, 0))
    return pl.pallas_call(
        functools.partial(_hgrn_kernel, tl=tl, c=c),
        grid=(nb, nt),
        in_specs=[col(ZB_HQ), col(ZB_HF), col(ZB_HI), col(ZB_HG), st_spec,
                  pl.BlockSpec((1, w), lambda i, t: (0, 0)),
                  pl.BlockSpec((1, HG_DK), lambda i, t: (0, 0))],
        out_specs=[pl.BlockSpec((tl, w), lambda i, t: (i * nt + t, 0)), st_spec],
        out_shape=[SDS((nb * seq, w), F32), SDS((nb, HG_HEADS, HG_DK, HG_DK), F32)],
        scratch_shapes=[pltpu.VMEM((HG_HEADS, HG_DK, HG_DK), F32),
                        pltpu.VMEM((tl, w), F32), pltpu.VMEM((tl, w), F32), pltpu.VMEM((tl, w), F32)],
        compiler_params=_params("arbitrary", "arbitrary"),
        name="hgrn_branch",
    )(z, z, z, z, s0, lb, ng)


def _cumsum_kernel(lf_ref, ct_ref, carry_ref, *, tl):
    @pl.when(pl.program_id(1) == 0)
    def _():
        carry_ref[...] = jnp.zeros_like(carry_ref)

    r = lax.broadcasted_iota(jnp.int32, (tl, tl), 0)
    cc = lax.broadcasted_iota(jnp.int32, (tl, tl), 1)
    tri = jnp.where(cc <= r, 1.0, 0.0).astype(F32)
    cs = jnp.dot(tri, lf_ref[...], precision=HI, preferred_element_type=F32) + carry_ref[...]
    carry_ref[...] = cs[tl - 1:tl, :]
    ct_ref[0] = cs.T[:ATT_HEADS, :]


def _logf_cumsum(lf, nb, seq, tl):
    nt = seq // tl
    return pl.pallas_call(
        functools.partial(_cumsum_kernel, tl=tl),
        grid=(nb, nt),
        in_specs=[pl.BlockSpec((tl, LANES), lambda i, t: (i * nt + t, 0))],
        out_specs=pl.BlockSpec((1, ATT_HEADS, tl), lambda i, t: (i, 0, t)),
        out_shape=SDS((nb, ATT_HEADS, seq), F32),
        scratch_shapes=[pltpu.VMEM((1, LANES), F32)],
        compiler_params=_params("arbitrary", "arbitrary"),
        name="logf_cumsum",
    )(lf)


_LOG2E = 1.4426950408889634
_FOX_BLOCK = 1024
_FOX_SUB = 512


def _fox_kernel(qi_ref, ki_ref, q_ref, k_ref, v_ref, ck_ref, cq_ref, o_ref, qm_ref, m_ref, acc_ref, *, t):
    pair = pl.program_id(1)
    qi = qi_ref[pl.program_id(2)]
    ki = ki_ref[pl.program_id(2)]
    lane = lax.broadcasted_iota(jnp.int32, (1, LANES), 1)

    @pl.when(ki == 0)
    def _():
        q = q_ref[...] * (ATT_HD ** -0.5 * _LOG2E)
        qm_ref[0] = jnp.where(lane < ATT_HD, q, 0.0).astype(BF16)
        qm_ref[1] = jnp.where(lane >= ATT_HD, q, 0.0).astype(BF16)
        m_ref[...] = jnp.full_like(m_ref, MASK_VALUE)
        acc_ref[...] = jnp.zeros_like(acc_ref)

    def step(diagonal):
        k = k_ref[...].astype(BF16)
        v = jnp.concatenate([v_ref[...].astype(BF16), jnp.ones((t, LANES), BF16)], axis=1)
        head_iota = lax.broadcasted_iota(jnp.int32, (ATT_HEADS, 1), 0)
        sub = min(t, _FOX_SUB)
        biases, states = [], []
        for j in range(2):
            own = head_iota == 2 * pair + j
            ck = jnp.sum(jnp.where(own, ck_ref[0], 0.0), axis=0, keepdims=True)
            c0 = jnp.sum(jnp.where(own, cq_ref[0, :, 0:1], 0.0), axis=0, keepdims=True)
            biases.append((c0 - ck) * _LOG2E)
            states.append((qm_ref[j], m_ref[j], acc_ref[j]))

        def logits(item):
            j, r0 = item
            nk = r0 + sub if diagonal else t
            s = lax.dot_general(states[j][0][r0:r0 + sub], k[:nk], _NT, preferred_element_type=F32)
            s = s + biases[j][:, :nk]
            if diagonal:
                rid = lax.broadcasted_iota(jnp.int32, (sub, nk), 0) + r0
                cid = lax.broadcasted_iota(jnp.int32, (sub, nk), 1)
                s = jnp.where(cid <= rid, s, MASK_VALUE)
            return s, nk

        items = [(j, r0) for j in range(2) for r0 in range(0, t, sub)]
        m_out, acc_out = ([], []), ([], [])
        nxt = logits(items[0])
        for idx, (j, r0) in enumerate(items):
            s, nk = nxt
            if idx + 1 < len(items):
                nxt = logits(items[idx + 1])
            rows = slice(r0, r0 + sub)
            m_old = states[j][1][rows]
            m_new = jnp.maximum(m_old, jnp.max(s, axis=1, keepdims=True))
            alpha = jnp.exp2(m_old - m_new)
            p = jnp.exp2(s - m_new[:, 0:1]).astype(BF16)
            pv = jnp.dot(p, v[:nk], preferred_element_type=F32)
            acc_out[j].append(jnp.concatenate([alpha, alpha], axis=1) * states[j][2][rows] + pv)
            m_out[j].append(m_new)
        for j in range(2):
            acc_ref[j] = jnp.concatenate(acc_out[j], axis=0)
            m_ref[j] = jnp.concatenate(m_out[j], axis=0)

    @pl.when(ki < qi)
    def _():
        step(False)

    @pl.when(ki == qi)
    def _():
        step(True)
        o0 = acc_ref[0, :, :LANES] / acc_ref[0, :, LANES:]
        o1 = acc_ref[1, :, :LANES] / acc_ref[1, :, LANES:]
        o_ref[...] = jnp.where(lane < ATT_HD, o0, o1)


def _fox_prompt(z, ct, nb, seq, t):
    nq = seq // t
    npair = ATT_HEADS // 2
    zq, zk, zv = (zb * (BRANCH_W // LANES) for zb in (ZB_AQ, ZB_AK, ZB_AV))
    pairs = [(qi, ki) for qi in range(nq) for ki in range(qi + 1)]
    qis = jnp.array([p[0] for p in pairs], jnp.int32)
    kis = jnp.array([p[1] for p in pairs], jnp.int32)
    grid_spec = pltpu.PrefetchScalarGridSpec(
        num_scalar_prefetch=2,
        grid=(nb, npair, len(pairs)),
        in_specs=[
            pl.BlockSpec((t, LANES), lambda b, p, s, qt, kt: (b * nq + qt[s], zq + p)),
            pl.BlockSpec((t, LANES), lambda b, p, s, qt, kt: (b * nq + kt[s], zk + p)),
            pl.BlockSpec((t, LANES), lambda b, p, s, qt, kt: (b * nq + kt[s], zv + p)),
            pl.BlockSpec((1, ATT_HEADS, t), lambda b, p, s, qt, kt: (b, 0, kt[s])),
            pl.BlockSpec((1, ATT_HEADS, t), lambda b, p, s, qt, kt: (b, 0, qt[s])),
        ],
        out_specs=pl.BlockSpec((t, LANES), lambda b, p, s, qt, kt: (b * nq + qt[s], p)),
        scratch_shapes=[pltpu.VMEM((2, t, LANES), BF16), pltpu.VMEM((2, t, LANES), F32),
                        pltpu.VMEM((2, t, 2 * LANES), F32)],
    )
    return pl.pallas_call(
        functools.partial(_fox_kernel, t=t),
        grid_spec=grid_spec,
        out_shape=SDS((nb * seq, BRANCH_W), F32),
        compiler_params=_params("arbitrary", "arbitrary", "arbitrary"),
        name="fox_prompt",
    )(qis, kis, z, z, z, ct, ct)


def _page_suffix_kernel(lf_ref, suf_ref, tot_ref):
    r = lax.broadcasted_iota(jnp.int32, (PAGE_SIZE, PAGE_SIZE), 0)
    c = lax.broadcasted_iota(jnp.int32, (PAGE_SIZE, PAGE_SIZE), 1)
    lf = lf_ref[...]
    suf_ref[...] = jnp.dot(lf, jnp.where(r > c, 1.0, 0.0).astype(F32), precision=HI, preferred_element_type=F32)
    tot_ref[...] = jnp.dot(lf, jnp.ones((PAGE_SIZE, PAGE_SIZE), F32), precision=HI, preferred_element_type=F32)


def _page_suffix(lf_rows, tr):
    n = lf_rows.shape[0]
    spec = pl.BlockSpec((tr, PAGE_SIZE), lambda i: (i, 0))
    return pl.pallas_call(
        _page_suffix_kernel,
        grid=(n // tr,),
        in_specs=[spec],
        out_specs=[spec, spec],
        out_shape=[SDS((n, PAGE_SIZE), F32), SDS((n, PAGE_SIZE), F32)],
        compiler_params=_params("arbitrary"),
        name="page_suffix",
    )(lf_rows)


_PAGES_PER_STEP = 8


def _fox_sample_kernel(pt_ref, q_ref, kn_ref, vn_ref, lfn_ref, *rest, g, tt):
    del pt_ref
    kp_refs, vp_refs, sf_refs, tot_refs = rest[:g], rest[g:2 * g], rest[2 * g:3 * g], rest[3 * g:4 * g]
    o_ref, qb_ref, m_ref, l_ref, acc_ref, car_ref, cn_ref = rest[4 * g:]
    step = pl.program_id(1)
    rows = tt * ATT_HEADS
    w = ATT_HEADS * ATT_HD
    rid = lax.broadcasted_iota(jnp.int32, (rows, 1), 0)
    hmask = (lax.broadcasted_iota(jnp.int32, (rows, w), 1) >> _HD_SHIFT) == (rid & (ATT_HEADS - 1))
    tok = rid >> _HEAD_SHIFT
    lane = lax.broadcasted_iota(jnp.int32, (rows, PAGE_SIZE), 1)

    def tile_rows(x):
        return jnp.broadcast_to(x[None], (tt,) + x.shape).reshape(rows, x.shape[-1])

    def online(scores, values, dims):
        m_old = m_ref[...]
        m_new = m_old
        for s in scores:
            m_new = jnp.maximum(m_new, jnp.max(s, axis=1, keepdims=True))
        alpha = jnp.exp(m_old - m_new)
        l_new = alpha * l_ref[...]
        acc = alpha * acc_ref[...]
        for s, v in zip(scores, values):
            p = jnp.exp(s - m_new)
            l_new = l_new + jnp.sum(p, axis=1, keepdims=True)
            acc = acc + lax.dot_general(p.astype(BF16), v, dims, preferred_element_type=F32)
        l_ref[...] = l_new
        acc_ref[...] = acc
        m_ref[...] = m_new

    @pl.when(step == 0)
    def _():
        q = q_ref[...] * (ATT_HD ** -0.5)
        qrep = jnp.broadcast_to(q[:, None, :], (tt, ATT_HEADS, w)).reshape(rows, w)
        qb_ref[...] = jnp.where(hmask, qrep, 0.0).astype(BF16)
        m_ref[...] = jnp.full_like(m_ref, MASK_VALUE)
        l_ref[...] = jnp.zeros_like(l_ref)
        acc_ref[...] = jnp.zeros_like(acc_ref)
        car_ref[...] = jnp.zeros_like(car_ref)
        r2 = lax.broadcasted_iota(jnp.int32, (PAGE_SIZE, PAGE_SIZE), 0)
        c2 = lax.broadcasted_iota(jnp.int32, (PAGE_SIZE, PAGE_SIZE), 1)
        incl = jnp.where(r2 <= c2, 1.0, 0.0).astype(F32)
        cn_ref[...] = jnp.dot(tile_rows(lfn_ref[0]), incl, precision=HI, preferred_element_type=F32)

    qb = qb_ref[...]
    cn = cn_ref[...]
    cn_own = jnp.sum(jnp.where(lane == tok, cn, 0.0), axis=1, keepdims=True)
    car = car_ref[...]
    scores = []
    for i in range(g):
        kt = kp_refs[i][0].reshape(w, PAGE_SIZE).astype(BF16)
        s = jnp.dot(qb, kt, preferred_element_type=F32)
        scores.append(s + ((cn_own + car) + tile_rows(sf_refs[i][0])))
        car = car + tile_rows(tot_refs[i][0])[:, 0:1]
    car_ref[...] = car
    online(scores, [vp_refs[i][0].reshape(w, PAGE_SIZE).astype(BF16) for i in range(g)], _NT)

    @pl.when(step == pl.num_programs(1) - 1)
    def _():
        pad = jnp.zeros((PAGE_SIZE - tt, w), F32)
        kn = jnp.concatenate([kn_ref[...], pad], axis=0).astype(BF16)
        vn = jnp.concatenate([vn_ref[...], pad], axis=0).astype(BF16)
        sn = lax.dot_general(qb, kn, _NT, preferred_element_type=F32)
        sn = jnp.where(lane <= tok, sn + (cn_own - cn), MASK_VALUE)
        online([sn], [vn], (((1,), (0,)), ((), ())))
        o = jnp.where(hmask, acc_ref[...] / l_ref[...], 0.0)
        o_ref[...] = jnp.sum(o.reshape(tt, ATT_HEADS, w), axis=1)


def _fox_sample(z, lfn_t, ckt, cvt, suffix, total, page_table, page_base, nb, tt):
    npg = page_table.shape[1]
    g = _PAGES_PER_STEP
    assert npg % g == 0
    w = ATT_HEADS * ATT_HD
    rows = tt * ATT_HEADS

    def page(i, ndim):
        def index(b, p, pt):
            return (page_base + pt[b * npg + (npg - 1 - (p * g + i))],) + (0,) * (ndim - 1)
        return index

    in_specs = [pl.BlockSpec((tt, w), lambda b, p, pt, zb=zb: (b, zb)) for zb in (ZB_AQ, ZB_AK, ZB_AV)]
    in_specs += [pl.BlockSpec((1, ATT_HEADS, PAGE_SIZE), lambda b, p, pt: (b, 0, 0))]
    in_specs += [pl.BlockSpec((1, ATT_HEADS, ATT_HD, PAGE_SIZE), page(i, 4)) for i in range(g)] * 2
    in_specs += [pl.BlockSpec((1, ATT_HEADS, PAGE_SIZE), page(i, 3)) for i in range(g)] * 2
    grid_spec = pltpu.PrefetchScalarGridSpec(
        num_scalar_prefetch=1,
        grid=(nb, npg // g),
        in_specs=in_specs,
        out_specs=pl.BlockSpec((tt, w), lambda b, p, pt: (b, 0)),
        scratch_shapes=[pltpu.VMEM((rows, w), BF16), pltpu.VMEM((rows, 1), F32), pltpu.VMEM((rows, 1), F32),
                        pltpu.VMEM((rows, w), F32), pltpu.VMEM((rows, 1), F32), pltpu.VMEM((rows, PAGE_SIZE), F32)],
    )
    return pl.pallas_call(
        functools.partial(_fox_sample_kernel, g=g, tt=tt),
        grid_spec=grid_spec,
        out_shape=SDS((nb * tt, w), F32),
        compiler_params=_params("arbitrary", "arbitrary"),
        name="fox_sample",
    )(page_table.reshape(-1), z, z, z, lfn_t, *([ckt] * g), *([cvt] * g), *([suffix] * g), *([total] * g))


def _merge_kernel(x_ref, yc_ref, yh_ref, ya_ref, gt_ref, wb_ref, wo_ref, g2_ref, h_ref, xnt_ref):
    merged = jnp.zeros(x_ref.shape, F32)
    for b, y_ref in enumerate((yc_ref, yh_ref, ya_ref)):
        proj = jnp.dot(y_ref[...].astype(BF16), wb_ref[b], preferred_element_type=F32)
        merged = merged + _sigmoid(gt_ref[:, b * D_MODEL:(b + 1) * D_MODEL]) * proj
    h = x_ref[...] + jnp.dot(merged.astype(BF16), wo_ref[...], preferred_element_type=F32)
    h_ref[...] = h
    xnt = _rms(h, g2_ref[...]).T.astype(BF16)
    for k in range(xnt_ref.shape[0]):
        xnt_ref[k] = xnt[:, k * _PEER_T:(k + 1) * _PEER_T]


_PEER_T = 256


def _merge(x, yc, yh, ya, z, wb, wo, g2, tm):
    n, d = x.shape
    assert tm % _PEER_T == 0
    br = pl.BlockSpec((tm, BRANCH_W), lambda i: (i, 0))
    return pl.pallas_call(
        _merge_kernel,
        grid=(n // tm,),
        in_specs=[
            pl.BlockSpec((tm, d), lambda i: (i, 0)), br, br, br,
            pl.BlockSpec((tm, 3 * d), lambda i: (i, 0)),
            pl.BlockSpec((3, BRANCH_W, d), lambda i: (0, 0, 0)),
            pl.BlockSpec((d, d), lambda i: (0, 0)),
            pl.BlockSpec((1, d), lambda i: (0, 0)),
        ],
        out_specs=[pl.BlockSpec((tm, d), lambda i: (i, 0)),
                   pl.BlockSpec((tm // _PEER_T, d, _PEER_T), lambda i: (i, 0, 0))],
        out_shape=[SDS((n, d), F32), SDS((n // _PEER_T, d, _PEER_T), BF16)],
        compiler_params=_params("arbitrary"),
        name="merge",
    )(x, yc, yh, ya, z, wb, wo, g2)


def _take_max(work, row_iota, n_rows):
    m = jnp.max(work, axis=0, keepdims=True)
    first = jnp.min(jnp.where(work == m, row_iota, float(n_rows)), axis=0, keepdims=True)
    return m, jnp.where(row_iota == first, -jnp.inf, work)


def _oddeven_merge_sort_pairs(n):
    pairs, p = [], 1
    while p < n:
        k = p
        while k >= 1:
            for j in range(k % p, n - k, 2 * k):
                for i in range(min(k, n - j - k)):
                    if (i + j) // (2 * p) == (i + j + k) // (2 * p):
                        pairs.append((i + j, i + j + k))
            k //= 2
        p *= 2
    return tuple(pairs)


_SORT_PAIRS = _oddeven_merge_sort_pairs(PEER_TOPK)


def _exchange(v, a, b):
    v[a], v[b] = jnp.maximum(v[a], v[b]), jnp.minimum(v[a], v[b])


def _top16_sorted(s):
    v = [s[_SUBLANES * i:_SUBLANES * (i + 1), :] for i in range(PEER_TOPK)]
    for a, b in _SORT_PAIRS:
        _exchange(v, a, b)
    for shift in (4, 6, 7):
        w = [pltpu.roll(x, shift, 0) for x in v]
        v = [jnp.maximum(v[k], w[PEER_TOPK - 1 - k]) for k in range(PEER_TOPK)]
        d = PEER_TOPK // 2
        while d >= 1:
            for i in range(PEER_TOPK):
                if not i & d:
                    _exchange(v, i, i + d)
            d //= 2
    return [x[0:1, :] for x in v]


_N_CAND = -(-len(_PAIRS) // 8) * 8


def _peer_route_kernel(xnt_ref, wqt_ref, keys_ref, cut1_ref, e1_ref, rank2_ref, e2_ref, cand_ref, *, t):
    xnt = xnt_ref[0]
    n_cand = _N_CAND
    cand_iota = lax.broadcasted_iota(jnp.int32, (n_cand, t), 0).astype(F32)
    cand_ref[len(_PAIRS):, :] = jnp.full((n_cand - len(_PAIRS), t), -jnp.inf, F32)

    def head(h, carry):
        dk = 2 * N_KEYS
        qt = jnp.dot(wqt_ref[pl.ds(pl.multiple_of(h * dk, dk), dk), :], xnt, preferred_element_type=F32)
        scores, tops = [], []
        for p in range(2):
            s = jnp.dot(keys_ref[h, p], qt[p * N_KEYS:(p + 1) * N_KEYS].astype(BF16), preferred_element_type=F32)
            scores.append(s)
            tops.append(_top16_sorted(s))
        for i, (a, b) in enumerate(_PAIRS):
            cand_ref[i:i + 1, :] = tops[0][a] + tops[1][b]
        work = cand_ref[...]
        best = []
        for _ in range(PEER_TOPK):
            m, work = _take_max(work, cand_iota, n_cand)
            best.append(m)
        norm = jnp.zeros((1, t), F32)
        for m in best:
            norm = norm + jnp.exp(m - best[0])
        cut = jnp.zeros((N_KEYS, t), F32)
        rank = jnp.full((N_KEYS, t), float(PEER_TOPK), F32)
        for b in range(PEER_TOPK):
            cut = jnp.where(scores[0] + tops[1][b] >= best[-1], float(b + 1), cut)
            rb = PEER_TOPK - 1 - b
            rank = jnp.where(scores[1] >= tops[1][rb], float(rb), rank)
        cut1_ref[0, h] = cut
        e1_ref[0, h] = jnp.exp(scores[0] - tops[0][0]) / norm
        rank2_ref[0, h] = rank.astype(BF16)
        e2_ref[0, h] = jnp.exp(scores[1] - tops[1][0]).astype(BF16)
        return carry

    lax.fori_loop(0, PEER_HEADS, head, 0)


def _peer_route(xnt, wqt, keys):
    nt, d, t = xnt.shape
    big = pl.BlockSpec((1, PEER_HEADS, N_KEYS, t), lambda i: (i, 0, 0, 0))
    wide, narrow = SDS((nt, PEER_HEADS, N_KEYS, t), F32), SDS((nt, PEER_HEADS, N_KEYS, t), BF16)
    return pl.pallas_call(
        functools.partial(_peer_route_kernel, t=t),
        grid=(nt,),
        in_specs=[
            pl.BlockSpec((1, d, t), lambda i: (i, 0, 0)),
            pl.BlockSpec(wqt.shape, lambda i: (0, 0)),
            pl.BlockSpec(keys.shape, lambda i: (0, 0, 0, 0)),
        ],
        out_specs=[big, big, big, big],
        out_shape=[wide, wide, narrow, narrow],
        scratch_shapes=[pltpu.VMEM((_N_CAND, t), F32)],
        compiler_params=_params("arbitrary"),
        name="peer_route",
    )(xnt, wqt, keys)


_BF16_ROWS = 16
_PEER_CHUNK = 2048


def _peer_dense_kernel(xnt_ref, h_ref, u_ref, vt_ref, cut1_ref, e1_ref, rank2_ref, e2_ref, *rest, t, ec):
    gain_ref = rest[0] if len(rest) == 8 else None
    o_ref, acc_ref, act_ref, cutx_ref, e1x_ref, rank2x_ref, e2x_ref = rest[-7:]
    c = pl.program_id(1)
    tile = pl.program_id(2)
    n_chunks = pl.num_programs(1) - 1

    @pl.when(c == 0)
    def _():
        acc_ref[tile] = jnp.zeros(acc_ref.shape[1:], F32)
        act_ref[tile, 1] = jnp.zeros((ec, t), BF16)
        rank2x_ref[tile] = rank2_ref[tile]
        e2x_ref[tile] = e2_ref[tile]

    acc_ref[tile] += jnp.dot(vt_ref[...], act_ref[tile, (c + 1) & 1], preferred_element_type=F32)

    chunk = jnp.minimum(c, n_chunks - 1)
    n_i1 = ec // N_KEYS
    slot = c & 1
    rep = N_KEYS // _BF16_ROWS
    for ii in range(n_i1):
        row = pl.ds(chunk * n_i1 + ii, 1)
        for h in range(PEER_HEADS):
            cutx_ref[h, ii] = jnp.broadcast_to(cut1_ref[tile, h, row, :], (_BF16_ROWS, t)).astype(BF16)
            e1x_ref[h, ii] = jnp.broadcast_to(e1_ref[tile, h, row, :], (_BF16_ROWS, t)).astype(BF16)
    xnt = xnt_ref[tile]
    for ii in range(n_i1):
        rows = slice(ii * N_KEYS, (ii + 1) * N_KEYS)
        hid = jnp.dot(u_ref[rows, :], xnt, preferred_element_type=F32)
        wsum = jnp.zeros((N_KEYS, t), BF16)
        for h in range(PEER_HEADS):
            cut = jnp.concatenate([cutx_ref[h, ii]] * rep, axis=0)
            e1 = jnp.concatenate([e1x_ref[h, ii]] * rep, axis=0)
            wsum = wsum + jnp.where(rank2x_ref[tile, h] < cut, e1 * e2x_ref[tile, h], jnp.zeros((), BF16))
        gelu = 0.5 * hid * (1.0 + lax.erf(hid * (2.0 ** -0.5)))
        act_ref[tile, slot, rows, :] = gelu.astype(BF16) * wsum

    @pl.when(c == n_chunks)
    def _():
        tok = pl.ds(pl.multiple_of(tile * t, t), t)
        out = h_ref[tok, :] + acc_ref[tile].T
        o_ref[tok, :] = out if gain_ref is None else _rms(out, gain_ref[...])


_PEER_GROUP = 2


def _peer_dense(xnt, h, u, vt, route, ec, final_gain=None):
    nt, d, t = xnt.shape
    extra = [] if final_gain is None else [final_gain]
    g = _PEER_GROUP if nt % _PEER_GROUP == 0 else 1
    nc = u.shape[0] // ec
    n_i1 = ec // N_KEYS
    big = pl.BlockSpec((g, PEER_HEADS, N_KEYS, t), lambda i, c, k: (i, 0, 0, 0))
    tokens = pl.BlockSpec((g * t, d), lambda i, c, k: (i, 0))
    return pl.pallas_call(
        functools.partial(_peer_dense_kernel, t=t, ec=ec),
        grid=(nt // g, nc + 1, g),
        in_specs=[
            pl.BlockSpec((g, d, t), lambda i, c, k: (i, 0, 0)),
            tokens,
            pl.BlockSpec((ec, d), lambda i, c, k: (jnp.minimum(c, nc - 1), 0)),
            pl.BlockSpec((d, ec), lambda i, c, k: (0, jnp.maximum(c - 1, 0))),
            big, big, big, big,
        ] + [pl.BlockSpec((1, d), lambda i, c, k: (0, 0))] * len(extra),
        out_specs=tokens,
        out_shape=SDS((nt * t, d), F32),
        scratch_shapes=[pltpu.VMEM((g, d, t), F32), pltpu.VMEM((g, 2, ec, t), BF16),
                        pltpu.VMEM((PEER_HEADS, n_i1, _BF16_ROWS, t), BF16),
                        pltpu.VMEM((PEER_HEADS, n_i1, _BF16_ROWS, t), BF16),
                        pltpu.VMEM((g, PEER_HEADS, N_KEYS, t), BF16), pltpu.VMEM((g, PEER_HEADS, N_KEYS, t), BF16)],
        compiler_params=_params("arbitrary", "arbitrary", "arbitrary", vmem_mb=56),
        name="peer_dense",
    )(xnt, h, u, vt, *route, *extra)


def _tile(n, pref):
    return pref if n % pref == 0 else n


def _layer_weights(l, lbs, norm1_g, w_in, conv_dw, conv_db, conv_ln_g, conv_ln_b, hg_norm_g, att_fb,
                   w_branch, w_out, norm2_g, peer_wq, peer_keys, peer_u, peer_v):
    n_main = 2 * BRANCH_W + 4 * HG_HEADS * HG_DK + 3 * ATT_HEADS * ATT_HD
    w = w_in[l]
    row = lambda a: a.reshape(1, -1).astype(F32)
    return dict(
        g1=row(norm1_g[l]),
        w_main=jnp.concatenate([w[:, n_main + ATT_HEADS:], w[:, :n_main]], axis=1).astype(BF16),
        w_f=jnp.pad(w[:, n_main:n_main + ATT_HEADS], ((0, 0), (0, LANES - ATT_HEADS))).astype(BF16),
        fb=jnp.pad(att_fb[l].astype(F32), (0, LANES - ATT_HEADS)).reshape(1, LANES),
        conv_w=conv_dw[l].astype(F32), conv_b=row(conv_db[l]), ln_g=row(conv_ln_g[l]), ln_b=row(conv_ln_b[l]),
        lb=row(lbs[l]), hg_g=row(hg_norm_g[l]),
        wb=w_branch[l].astype(BF16), wo=w_out[l].astype(BF16), g2=row(norm2_g[l]),
        wqt=peer_wq[l].T.astype(BF16), keys=peer_keys[l].astype(BF16),
        u=peer_u[l].astype(BF16), vt=peer_v[l].T.astype(BF16),
    )


def _token_mixers(x, wt, nb, seq, conv_buf, hg_state, attend, final_gain=None):
    n = nb * seq
    tm = _tile(n, 1024)
    w = ATT_HEADS * ATT_HD
    if seq % tm == 0:
        z, lf, kt, vt = _in_proj(x, wt["g1"], wt["w_main"], wt["w_f"], wt["fb"], tm, seq)
        k_rows, v_rows = (a.reshape(nb, ATT_HEADS, ATT_HD, seq).transpose(0, 3, 1, 2) for a in (kt, vt))
    else:
        z, lf = _in_proj(x, wt["g1"], wt["w_main"], wt["w_f"], wt["fb"], tm)
        k_rows, v_rows = (z[:, zb * BRANCH_W:zb * BRANCH_W + w].reshape(nb, seq, ATT_HEADS, ATT_HD)
                          for zb in (ZB_AK, ZB_AV))
    y_conv, conv_new = _conv_branch(z, conv_buf, wt["conv_w"], wt["conv_b"], wt["ln_g"], wt["ln_b"],
                                    nb, seq, _tile(seq, 512))
    y_hgrn, hg_new = _hgrn_branch(z, hg_state, wt["lb"], wt["hg_g"], nb, seq, _tile(seq, 256), _tile(seq, 16))
    y_att = attend(z, lf)
    h, xn = _merge(x, y_conv, y_hgrn, y_att, z, wt["wb"], wt["wo"], wt["g2"], _tile(n, 512))
    route = _peer_route(xn, wt["wqt"], wt["keys"])
    out = _peer_dense(xn, h, wt["u"], wt["vt"], route, _PEER_CHUNK, final_gain)
    logf = lf[:, :ATT_HEADS].reshape(nb, seq, ATT_HEADS)
    return out, conv_new, hg_new, k_rows, v_rows, logf


def kernel(x_prompt, x_sample, cache_k, cache_v, cache_logf, state_conv, state_hgrn, page_table, norm1_g, w_in, conv_dw, conv_db, conv_ln_g, conv_ln_b, hg_lb_logits, hg_norm_g, att_fb, w_branch, w_out, norm2_g, peer_wq, peer_keys, peer_u, peer_v, final_g):
    depth = w_in.shape[0]
    bp, sp, d = x_prompt.shape
    bs, ss, _ = x_sample.shape
    w = ATT_HEADS * ATT_HD
    probs = jax.nn.softmax(hg_lb_logits.astype(F32), axis=0)
    lbs = jnp.cumsum(probs, axis=0) - probs[0:1]
    n_phys = cache_k.shape[1]
    ckt = jnp.transpose(cache_k, (0, 1, 3, 4, 2)).reshape(depth * n_phys, ATT_HEADS, ATT_HD, PAGE_SIZE)
    cvt = jnp.transpose(cache_v, (0, 1, 3, 4, 2)).reshape(depth * n_phys, ATT_HEADS, ATT_HD, PAGE_SIZE)
    lf_rows = jnp.swapaxes(cache_logf.astype(F32), 2, 3).reshape(depth * n_phys * ATT_HEADS, PAGE_SIZE)
    suffix, total = (a.reshape(depth * n_phys, ATT_HEADS, PAGE_SIZE)
                     for a in _page_suffix(lf_rows, _tile(lf_rows.shape[0], 2048)))

    hp = x_prompt.reshape(bp * sp, d)
    hs = x_sample.reshape(bs * ss, d)
    outs_p, outs_s = [], []
    for l in range(depth):
        wt = _layer_weights(l, lbs, norm1_g, w_in, conv_dw, conv_db, conv_ln_g, conv_ln_b, hg_norm_g, att_fb,
                            w_branch, w_out, norm2_g, peer_wq, peer_keys, peer_u, peer_v)

        def attend_prompt(z, lf):
            ct = _logf_cumsum(lf, bp, sp, _tile(sp, 512))
            return _fox_prompt(z, ct, bp, sp, _tile(sp, _FOX_BLOCK))

        def attend_sample(z, lf, l=l):
            lfn_t = jnp.swapaxes(lf[:, :ATT_HEADS].reshape(bs, ss, ATT_HEADS), 1, 2)
            lfn_t = jnp.pad(lfn_t, ((0, 0), (0, 0), (0, PAGE_SIZE - ss)))
            return _fox_sample(z, lfn_t, ckt, cvt, suffix, total, page_table, l * n_phys, bs, ss)

        fg = final_g.reshape(1, d).astype(F32) if l == depth - 1 else None
        hp, *rp = _token_mixers(hp, wt, bp, sp, jnp.zeros((bp, CONV_K - 1, BRANCH_W), F32),
                                jnp.zeros((bp, HG_HEADS, HG_DK, HG_DK), F32), attend_prompt, fg)
        hs, *rs = _token_mixers(hs, wt, bs, ss, state_conv[l], state_hgrn[l], attend_sample, fg)
        outs_p.append(rp)
        outs_s.append(rs)

    y_prompt = hp.reshape(bp, sp, d)
    y_sample = hs.reshape(bs, ss, d)
    stack = lambda outs, i: jnp.stack([r[i] for r in outs])
    return (y_prompt, y_sample,
            stack(outs_p, 2), stack(outs_p, 3), stack(outs_p, 4), stack(outs_p, 0), stack(outs_p, 1),
            stack(outs_s, 2), stack(outs_s, 3), stack(outs_s, 4), stack(outs_s, 0), stack(outs_s, 1))
```

```python
import functools

import jax
import jax.numpy as jnp
from jax import lax
from jax.experimental import pallas as pl
from jax.experimental.pallas import tpu as pltpu

F32 = jnp.float32
BF16 = jnp.bfloat16
HI = lax.Precision.HIGHEST
SDS = jax.ShapeDtypeStruct

D_MODEL = 1024
BRANCH_W = 512
CONV_K = 31
HG_HEADS = 4
HG_DK = 128
ATT_HEADS = 8
ATT_HD = 64
PEER_HEADS = 8
N_KEYS = 128
PEER_TOPK = 16
PAGE_SIZE = 128
EPS = 1e-6
GATE_FLOOR = 1e-20
MASK_VALUE = -1e30
LANES = 128
_SUBLANES = 8
_HD_SHIFT = ATT_HD.bit_length() - 1
_HEAD_SHIFT = ATT_HEADS.bit_length() - 1

Z_COLS = 15 * BRANCH_W
ZB_CONV = 3
ZB_HQ, ZB_HF, ZB_HI, ZB_HG, ZB_AQ, ZB_AK, ZB_AV = 8, 9, 10, 11, 12, 13, 14

_PAIRS = tuple((a, b) for a in range(PEER_TOPK) for b in range(PEER_TOPK) if (a + 1) * (b + 1) <= PEER_TOPK)

_NT = (((1,), (1,)), ((), ()))
_TN = (((0,), (0,)), ((), ()))


def _params(*sem, vmem_mb=48):
    return pltpu.CompilerParams(dimension_semantics=sem, vmem_limit_bytes=vmem_mb * 1024 * 1024)


def _sigmoid(x):
    return 1.0 / (1.0 + jnp.exp(-x))


def _log_sigmoid(x):
    return jnp.minimum(x, 0.0) - jnp.log1p(jnp.exp(-jnp.abs(x)))


def _rms(x, g):
    return x * lax.rsqrt(jnp.mean(x * x, axis=-1, keepdims=True) + EPS) * g


def _in_proj_kernel(x_ref, g_ref, w_ref, wf_ref, fb_ref, z_ref, lf_ref, *rest):
    xn_ref = rest[-1]

    @pl.when(pl.program_id(1) == 0)
    def _():
        xn = _rms(x_ref[...], g_ref[...]).astype(BF16)
        xn_ref[...] = xn
        af = jnp.dot(xn, wf_ref[...], preferred_element_type=F32) + fb_ref[...]
        lf_ref[...] = _log_sigmoid(af)

    z = jnp.dot(xn_ref[...], w_ref[...], preferred_element_type=F32)
    z_ref[...] = z

    if len(rest) == 3:
        @pl.when(pl.program_id(1) == pl.num_programs(1) - 1)
        def _():
            kt_ref, vt_ref = rest[:2]
            kt_ref[0] = z[:, BRANCH_W:2 * BRANCH_W].T
            vt_ref[0] = z[:, 2 * BRANCH_W:].T


def _in_proj(x, g, w_main, w_f, fb, tm, seq=None):
    n, d = x.shape
    tn = 3 * BRANCH_W
    assert ZB_AK * BRANCH_W == Z_COLS - 2 * BRANCH_W and ZB_AV * BRANCH_W == Z_COLS - BRANCH_W
    out_specs = [pl.BlockSpec((tm, tn), lambda i, j: (i, j)), pl.BlockSpec((tm, LANES), lambda i, j: (i, 0))]
    out_shape = [SDS((n, Z_COLS), F32), SDS((n, LANES), F32)]
    if seq is not None:
        nlb = seq // tm
        tr = pl.BlockSpec((1, BRANCH_W, tm), lambda i, j: (i // nlb, 0, i % nlb))
        out_specs += [tr, tr]
        out_shape += [SDS((n // seq, BRANCH_W, seq), F32)] * 2
    return pl.pallas_call(
        _in_proj_kernel,
        grid=(n // tm, Z_COLS // tn),
        in_specs=[
            pl.BlockSpec((tm, d), lambda i, j: (i, 0)),
            pl.BlockSpec((1, d), lambda i, j: (0, 0)),
            pl.BlockSpec((d, tn), lambda i, j: (0, j)),
            pl.BlockSpec((d, LANES), lambda i, j: (0, 0)),
            pl.BlockSpec((1, LANES), lambda i, j: (0, 0)),
        ],
        out_specs=out_specs,
        out_shape=out_shape,
        scratch_shapes=[pltpu.VMEM((tm, d), BF16)],
        compiler_params=_params("arbitrary", "arbitrary"),
        name="in_proj",
    )(x, g, w_main, w_f, fb)


_CONV_PAD = 32


def _conv_kernel(a_ref, buf_ref, w_ref, b_ref, g_ref, bt_ref, y_ref, nc_ref, ext_ref, sh_ref, *, tl):
    hist = CONV_K - 1
    lo = _CONV_PAD - hist

    @pl.when(pl.program_id(1) == 0)
    def _():
        ext_ref[lo:_CONV_PAD, :] = buf_ref[0]

    a = a_ref[...]
    ext_ref[_CONV_PAD:_CONV_PAD + tl, :] = a[:, :BRANCH_W] * _sigmoid(a[:, BRANCH_W:])
    acc = jnp.zeros((tl, BRANCH_W), F32)
    for r in range(_SUBLANES):
        taps = [j for j in range(CONV_K) if (lo + j) % _SUBLANES == r]
        if not taps:
            continue
        last = (lo + taps[-1]) - r
        sh_ref[0:last + tl, :] = ext_ref[r:r + last + tl, :]
        for j in taps:
            q0 = lo + j - r
            acc = acc + w_ref[j:j + 1, :] * sh_ref[q0:q0 + tl, :]
    y = acc + b_ref[...]
    mu = jnp.mean(y, axis=-1, keepdims=True)
    yc = y - mu
    var = jnp.mean(yc * yc, axis=-1, keepdims=True)
    y = yc * lax.rsqrt(var + EPS) * g_ref[...] + bt_ref[...]
    y_ref[...] = y * _sigmoid(y)
    tail = ext_ref[tl + lo:tl + _CONV_PAD, :]
    nc_ref[0] = tail
    ext_ref[lo:_CONV_PAD, :] = tail


def _conv_branch(z, buf, w, b, g, bt, nb, seq, tl):
    nt = seq // tl
    hist = CONV_K - 1
    vec = pl.BlockSpec((1, BRANCH_W), lambda i, t: (0, 0))
    return pl.pallas_call(
        functools.partial(_conv_kernel, tl=tl),
        grid=(nb, nt),
        in_specs=[
            pl.BlockSpec((tl, 2 * BRANCH_W), lambda i, t: (i * nt + t, ZB_CONV)),
            pl.BlockSpec((1, hist, BRANCH_W), lambda i, t: (i, 0, 0)),
            pl.BlockSpec((CONV_K, BRANCH_W), lambda i, t: (0, 0)),
            vec, vec, vec,
        ],
        out_specs=[
            pl.BlockSpec((tl, BRANCH_W), lambda i, t: (i * nt + t, 0)),
            pl.BlockSpec((1, hist, BRANCH_W), lambda i, t: (i, 0, 0)),
        ],
        out_shape=[SDS((nb * seq, BRANCH_W), F32), SDS((nb, hist, BRANCH_W), F32)],
        scratch_shapes=[pltpu.VMEM((_CONV_PAD + tl, BRANCH_W), F32), pltpu.VMEM((_CONV_PAD + tl, BRANCH_W), F32)],
        compiler_params=_params("arbitrary", "arbitrary"),
        name="conv_branch",
    )(z, buf, w, b, g, bt)


def _hgrn_kernel(q_ref, f_ref, i_ref, g_ref, s0_ref, lb_ref, ng_ref, y_ref, sn_ref,
                 st_ref, qs_ref, ks_ref, bs_ref, *, tl, c):
    t = pl.program_id(1)

    @pl.when(t == 0)
    def _():
        for h in range(HG_HEADS):
            st_ref[h] = s0_ref[0, h].T

    lb = jnp.clip(lb_ref[...], 0.0, 1.0)
    f = lb + (1.0 - lb) * _sigmoid(f_ref[...])
    lf = jnp.log(jnp.maximum(f, GATE_FLOOR))
    q = q_ref[...]
    qs_ref[...] = q * _sigmoid(q)
    ks_ref[...] = 1.0 - f
    r = lax.broadcasted_iota(jnp.int32, (tl, tl), 0)
    cc = lax.broadcasted_iota(jnp.int32, (tl, tl), 1)
    tri = jnp.where(cc <= r, jnp.where(cc >= (r & -c), 1.0, 0.0), 0.0).astype(F32)
    bs_ref[...] = jnp.dot(tri, lf, precision=HI, preferred_element_type=F32)
    rowid = lax.broadcasted_iota(jnp.int32, (c, 1), 0)

    def chunk(ci, carry):
        off = pl.multiple_of(ci * c, c)
        for h in range(HG_HEADS):
            sl = slice(h * HG_DK, (h + 1) * HG_DK)
            qc = qs_ref[pl.ds(off, c), sl]
            kc = ks_ref[pl.ds(off, c), sl]
            bc = bs_ref[pl.ds(off, c), sl]
            vc = i_ref[pl.ds(off, c), sl]
            o = jnp.zeros((c, HG_DK), F32)
            for s in range(c):
                k_row = kc[s:s + 1, :]
                b_row = bc[s:s + 1, :]
                v_row = vc[s:s + 1, :]
                p = qc * k_row * jnp.exp(jnp.minimum(bc - b_row, 0.0))
                a = jnp.sum(p, axis=1, keepdims=True)
                o = o + jnp.where(rowid >= s, a, 0.0) * v_row
            st = st_ref[h]
            qt = (qc * jnp.exp(bc)).astype(BF16)
            o = o + lax.dot_general(qt, st.astype(BF16), _NT, preferred_element_type=F32)
            b_last = bc[c - 1:c, :]
            kt = (kc * jnp.exp(b_last - bc)).astype(BF16)
            upd = lax.dot_general(vc.astype(BF16), kt, _TN, preferred_element_type=F32)
            st_ref[h] = st * jnp.exp(b_last) + upd
            o = _rms(o, ng_ref[...])
            y_ref[pl.ds(off, c), sl] = o * _sigmoid(g_ref[pl.ds(off, c), sl])
        return carry

    lax.fori_loop(0, tl // c, chunk, 0)

    @pl.when(t == pl.num_programs(1) - 1)
    def _():
        for h in range(HG_HEADS):
            sn_ref[0, h] = st_ref[h].T


def _hgrn_branch(z, s0, lb, ng, nb, seq, tl, c):
    nt = seq // tl
    w = HG_HEADS * HG_DK

    def col(j):
        return pl.BlockSpec((tl, w), lambda i, t: (i * nt + t, j))

    st_spec = pl.BlockSpec((1, HG_HEADS, HG_DK, HG_DK), lambda i, t: (i, 0, 0, 0))
    return pl.pallas_call(
        functools.partial(_hgrn_kernel, tl=tl, c=c),
        grid=(nb, nt),
        in_specs=[col(ZB_HQ), col(ZB_HF), col(ZB_HI), col(ZB_HG), st_spec,
                  pl.BlockSpec((1, w), lambda i, t: (0, 0)),
                  pl.BlockSpec((1, HG_DK), lambda i, t: (0, 0))],
        out_specs=[pl.BlockSpec((tl, w), lambda i, t: (i * nt + t, 0)), st_spec],
        out_shape=[SDS((nb * seq, w), F32), SDS((nb, HG_HEADS, HG_DK, HG_DK), F32)],
        scratch_shapes=[pltpu.VMEM((HG_HEADS, HG_DK, HG_DK), F32),
                        pltpu.VMEM((tl, w), F32), pltpu.VMEM((tl, w), F32), pltpu.VMEM((tl, w), F32)],
        compiler_params=_params("arbitrary", "arbitrary"),
        name="hgrn_branch",
    )(z, z, z, z, s0, lb, ng)


def _cumsum_kernel(lf_ref, ct_ref, carry_ref, *, tl):
    @pl.when(pl.program_id(1) == 0)
    def _():
        carry_ref[...] = jnp.zeros_like(carry_ref)

    r = lax.broadcasted_iota(jnp.int32, (tl, tl), 0)
    cc = lax.broadcasted_iota(jnp.int32, (tl, tl), 1)
    tri = jnp.where(cc <= r, 1.0, 0.0).astype(F32)
    cs = jnp.dot(tri, lf_ref[...], precision=HI, preferred_element_type=F32) + carry_ref[...]
    carry_ref[...] = cs[tl - 1:tl, :]
    ct_ref[0] = cs.T[:ATT_HEADS, :]


def _logf_cumsum(lf, nb, seq, tl):
    nt = seq // tl
    return pl.pallas_call(
        functools.partial(_cumsum_kernel, tl=tl),
        grid=(nb, nt),
        in_specs=[pl.BlockSpec((tl, LANES), lambda i, t: (i * nt + t, 0))],
        out_specs=pl.BlockSpec((1, ATT_HEADS, tl), lambda i, t: (i, 0, t)),
        out_shape=SDS((nb, ATT_HEADS, seq), F32),
        scratch_shapes=[pltpu.VMEM((1, LANES), F32)],
        compiler_params=_params("arbitrary", "arbitrary"),
        name="logf_cumsum",
    )(lf)


_LOG2E = 1.4426950408889634
_FOX_BLOCK = 1024
_FOX_SUB = 512


def _fox_kernel(qi_ref, ki_ref, q_ref, k_ref, v_ref, ck_ref, cq_ref, o_ref, qm_ref, m_ref, acc_ref, *, t):
    pair = pl.program_id(1)
    qi = qi_ref[pl.program_id(2)]
    ki = ki_ref[pl.program_id(2)]
    lane = lax.broadcasted_iota(jnp.int32, (1, LANES), 1)

    @pl.when(ki == 0)
    def _():
        q = q_ref[...] * (ATT_HD ** -0.5 * _LOG2E)
        qm_ref[0] = jnp.where(lane < ATT_HD, q, 0.0).astype(BF16)
        qm_ref[1] = jnp.where(lane >= ATT_HD, q, 0.0).astype(BF16)
        m_ref[...] = jnp.full_like(m_ref, MASK_VALUE)
        acc_ref[...] = jnp.zeros_like(acc_ref)

    def step(diagonal):
        k = k_ref[...].astype(BF16)
        v = jnp.concatenate([v_ref[...].astype(BF16), jnp.ones((t, LANES), BF16)], axis=1)
        head_iota = lax.broadcasted_iota(jnp.int32, (ATT_HEADS, 1), 0)
        sub = min(t, _FOX_SUB)
        biases, states = [], []
        for j in range(2):
            own = head_iota == 2 * pair + j
            ck = jnp.sum(jnp.where(own, ck_ref[0], 0.0), axis=0, keepdims=True)
            c0 = jnp.sum(jnp.where(own, cq_ref[0, :, 0:1], 0.0), axis=0, keepdims=True)
            biases.append((c0 - ck) * _LOG2E)
            states.append((qm_ref[j], m_ref[j], acc_ref[j]))

        def logits(item):
            j, r0 = item
            nk = r0 + sub if diagonal else t
            s = lax.dot_general(states[j][0][r0:r0 + sub], k[:nk], _NT, preferred_element_type=F32)
            s = s + biases[j][:, :nk]
            if diagonal:
                rid = lax.broadcasted_iota(jnp.int32, (sub, nk), 0) + r0
                cid = lax.broadcasted_iota(jnp.int32, (sub, nk), 1)
                s = jnp.where(cid <= rid, s, MASK_VALUE)
            return s, nk

        items = [(j, r0) for j in range(2) for r0 in range(0, t, sub)]
        m_out, acc_out = ([], []), ([], [])
        nxt = logits(items[0])
        for idx, (j, r0) in enumerate(items):
            s, nk = nxt
            if idx + 1 < len(items):
                nxt = logits(items[idx + 1])
            rows = slice(r0, r0 + sub)
            m_old = states[j][1][rows]
            m_new = jnp.maximum(m_old, jnp.max(s, axis=1, keepdims=True))
            alpha = jnp.exp2(m_old - m_new)
            p = jnp.exp2(s - m_new[:, 0:1]).astype(BF16)
            pv = jnp.dot(p, v[:nk], preferred_element_type=F32)
            acc_out[j].append(jnp.concatenate([alpha, alpha], axis=1) * states[j][2][rows] + pv)
            m_out[j].append(m_new)
        for j in range(2):
            acc_ref[j] = jnp.concatenate(acc_out[j], axis=0)
            m_ref[j] = jnp.concatenate(m_out[j], axis=0)

    @pl.when(ki < qi)
    def _():
        step(False)

    @pl.when(ki == qi)
    def _():
        step(True)
        o0 = acc_ref[0, :, :LANES] / acc_ref[0, :, LANES:]
        o1 = acc_ref[1, :, :LANES] / acc_ref[1, :, LANES:]
        o_ref[...] = jnp.where(lane < ATT_HD, o0, o1)


def _fox_prompt(z, ct, nb, seq, t):
    nq = seq // t
    npair = ATT_HEADS // 2
    zq, zk, zv = (zb * (BRANCH_W // LANES) for zb in (ZB_AQ, ZB_AK, ZB_AV))
    pairs = [(qi, ki) for qi in range(nq) for ki in range(qi + 1)]
    qis = jnp.array([p[0] for p in pairs], jnp.int32)
    kis = jnp.array([p[1] for p in pairs], jnp.int32)
    grid_spec = pltpu.PrefetchScalarGridSpec(
        num_scalar_prefetch=2,
        grid=(nb, npair, len(pairs)),
        in_specs=[
            pl.BlockSpec((t, LANES), lambda b, p, s, qt, kt: (b * nq + qt[s], zq + p)),
            pl.BlockSpec((t, LANES), lambda b, p, s, qt, kt: (b * nq + kt[s], zk + p)),
            pl.BlockSpec((t, LANES), lambda b, p, s, qt, kt: (b * nq + kt[s], zv + p)),
            pl.BlockSpec((1, ATT_HEADS, t), lambda b, p, s, qt, kt: (b, 0, kt[s])),
            pl.BlockSpec((1, ATT_HEADS, t), lambda b, p, s, qt, kt: (b, 0, qt[s])),
        ],
        out_specs=pl.BlockSpec((t, LANES), lambda b, p, s, qt, kt: (b * nq + qt[s], p)),
        scratch_shapes=[pltpu.VMEM((2, t, LANES), BF16), pltpu.VMEM((2, t, LANES), F32),
                        pltpu.VMEM((2, t, 2 * LANES), F32)],
    )
    return pl.pallas_call(
        functools.partial(_fox_kernel, t=t),
        grid_spec=grid_spec,
        out_shape=SDS((nb * seq, BRANCH_W), F32),
        compiler_params=_params("arbitrary", "arbitrary", "arbitrary"),
        name="fox_prompt",
    )(qis, kis, z, z, z, ct, ct)


def _page_suffix_kernel(lf_ref, suf_ref, tot_ref):
    r = lax.broadcasted_iota(jnp.int32, (PAGE_SIZE, PAGE_SIZE), 0)
    c = lax.broadcasted_iota(jnp.int32, (PAGE_SIZE, PAGE_SIZE), 1)
    lf = lf_ref[...]
    suf_ref[...] = jnp.dot(lf, jnp.where(r > c, 1.0, 0.0).astype(F32), precision=HI, preferred_element_type=F32)
    tot_ref[...] = jnp.dot(lf, jnp.ones((PAGE_SIZE, PAGE_SIZE), F32), precision=HI, preferred_element_type=F32)


def _page_suffix(lf_rows, tr):
    n = lf_rows.shape[0]
    spec = pl.BlockSpec((tr, PAGE_SIZE), lambda i: (i, 0))
    return pl.pallas_call(
        _page_suffix_kernel,
        grid=(n // tr,),
        in_specs=[spec],
        out_specs=[spec, spec],
        out_shape=[SDS((n, PAGE_SIZE), F32), SDS((n, PAGE_SIZE), F32)],
        compiler_params=_params("arbitrary"),
        name="page_suffix",
    )(lf_rows)


_PAGES_PER_STEP = 8


def _fox_sample_kernel(pt_ref, q_ref, kn_ref, vn_ref, lfn_ref, *rest, g, tt):
    del pt_ref
    kp_refs, vp_refs, sf_refs, tot_refs = rest[:g], rest[g:2 * g], rest[2 * g:3 * g], rest[3 * g:4 * g]
    o_ref, qb_ref, m_ref, l_ref, acc_ref, car_ref, cn_ref = rest[4 * g:]
    step = pl.program_id(1)
    rows = tt * ATT_HEADS
    w = ATT_HEADS * ATT_HD
    rid = lax.broadcasted_iota(jnp.int32, (rows, 1), 0)
    hmask = (lax.broadcasted_iota(jnp.int32, (rows, w), 1) >> _HD_SHIFT) == (rid & (ATT_HEADS - 1))
    tok = rid >> _HEAD_SHIFT
    lane = lax.broadcasted_iota(jnp.int32, (rows, PAGE_SIZE), 1)

    def tile_rows(x):
        return jnp.broadcast_to(x[None], (tt,) + x.shape).reshape(rows, x.shape[-1])

    def online(scores, values, dims):
        m_old = m_ref[...]
        m_new = m_old
        for s in scores:
            m_new = jnp.maximum(m_new, jnp.max(s, axis=1, keepdims=True))
        alpha = jnp.exp(m_old - m_new)
        l_new = alpha * l_ref[...]
        acc = alpha * acc_ref[...]
        for s, v in zip(scores, values):
            p = jnp.exp(s - m_new)
            l_new = l_new + jnp.sum(p, axis=1, keepdims=True)
            acc = acc + lax.dot_general(p.astype(BF16), v, dims, preferred_element_type=F32)
        l_ref[...] = l_new
        acc_ref[...] = acc
        m_ref[...] = m_new

    @pl.when(step == 0)
    def _():
        q = q_ref[...] * (ATT_HD ** -0.5)
        qrep = jnp.broadcast_to(q[:, None, :], (tt, ATT_HEADS, w)).reshape(rows, w)
        qb_ref[...] = jnp.where(hmask, qrep, 0.0).astype(BF16)
        m_ref[...] = jnp.full_like(m_ref, MASK_VALUE)
        l_ref[...] = jnp.zeros_like(l_ref)
        acc_ref[...] = jnp.zeros_like(acc_ref)
        car_ref[...] = jnp.zeros_like(car_ref)
        r2 = lax.broadcasted_iota(jnp.int32, (PAGE_SIZE, PAGE_SIZE), 0)
        c2 = lax.broadcasted_iota(jnp.int32, (PAGE_SIZE, PAGE_SIZE), 1)
        incl = jnp.where(r2 <= c2, 1.0, 0.0).astype(F32)
        cn_ref[...] = jnp.dot(tile_rows(lfn_ref[0]), incl, precision=HI, preferred_element_type=F32)

    qb = qb_ref[...]
    cn = cn_ref[...]
    cn_own = jnp.sum(jnp.where(lane == tok, cn, 0.0), axis=1, keepdims=True)
    car = car_ref[...]
    scores = []
    for i in range(g):
        kt = kp_refs[i][0].reshape(w, PAGE_SIZE).astype(BF16)
        s = jnp.dot(qb, kt, preferred_element_type=F32)
        scores.append(s + ((cn_own + car) + tile_rows(sf_refs[i][0])))
        car = car + tile_rows(tot_refs[i][0])[:, 0:1]
    car_ref[...] = car
    online(scores, [vp_refs[i][0].reshape(w, PAGE_SIZE).astype(BF16) for i in range(g)], _NT)

    @pl.when(step == pl.num_programs(1) - 1)
    def _():
        pad = jnp.zeros((PAGE_SIZE - tt, w), F32)
        kn = jnp.concatenate([kn_ref[...], pad], axis=0).astype(BF16)
        vn = jnp.concatenate([vn_ref[...], pad], axis=0).astype(BF16)
        sn = lax.dot_general(qb, kn, _NT, preferred_element_type=F32)
        sn = jnp.where(lane <= tok, sn + (cn_own - cn), MASK_VALUE)
        online([sn], [vn], (((1,), (0,)), ((), ())))
        o = jnp.where(hmask, acc_ref[...] / l_ref[...], 0.0)
        o_ref[...] = jnp.sum(o.reshape(tt, ATT_HEADS, w), axis=1)


def _fox_sample(z, lfn_t, ckt, cvt, suffix, total, page_table, page_base, nb, tt):
    npg = page_table.shape[1]
    g = _PAGES_PER_STEP
    assert npg % g == 0
    w = ATT_HEADS * ATT_HD
    rows = tt * ATT_HEADS

    def page(i, ndim):
        def index(b, p, pt):
            return (page_base + pt[b * npg + (npg - 1 - (p * g + i))],) + (0,) * (ndim - 1)
        return index

    in_specs = [pl.BlockSpec((tt, w), lambda b, p, pt, zb=zb: (b, zb)) for zb in (ZB_AQ, ZB_AK, ZB_AV)]
    in_specs += [pl.BlockSpec((1, ATT_HEADS, PAGE_SIZE), lambda b, p, pt: (b, 0, 0))]
    in_specs += [pl.BlockSpec((1, ATT_HEADS, ATT_HD, PAGE_SIZE), page(i, 4)) for i in range(g)] * 2
    in_specs += [pl.BlockSpec((1, ATT_HEADS, PAGE_SIZE), page(i, 3)) for i in range(g)] * 2
    grid_spec = pltpu.PrefetchScalarGridSpec(
        num_scalar_prefetch=1,
        grid=(nb, npg // g),
        in_specs=in_specs,
        out_specs=pl.BlockSpec((tt, w), lambda b, p, pt: (b, 0)),
        scratch_shapes=[pltpu.VMEM((rows, w), BF16), pltpu.VMEM((rows, 1), F32), pltpu.VMEM((rows, 1), F32),
                        pltpu.VMEM((rows, w), F32), pltpu.VMEM((rows, 1), F32), pltpu.VMEM((rows, PAGE_SIZE), F32)],
    )
    return pl.pallas_call(
        functools.partial(_fox_sample_kernel, g=g, tt=tt),
        grid_spec=grid_spec,
        out_shape=SDS((nb * tt, w), F32),
        compiler_params=_params("arbitrary", "arbitrary"),
        name="fox_sample",
    )(page_table.reshape(-1), z, z, z, lfn_t, *([ckt] * g), *([cvt] * g), *([suffix] * g), *([total] * g))


def _merge_kernel(x_ref, yc_ref, yh_ref, ya_ref, gt_ref, wb_ref, wo_ref, g2_ref, h_ref, xnt_ref):
    merged = jnp.zeros(x_ref.shape, F32)
    for b, y_ref in enumerate((yc_ref, yh_ref, ya_ref)):
        proj = jnp.dot(y_ref[...].astype(BF16), wb_ref[b], preferred_element_type=F32)
        merged = merged + _sigmoid(gt_ref[:, b * D_MODEL:(b + 1) * D_MODEL]) * proj
    h = x_ref[...] + jnp.dot(merged.astype(BF16), wo_ref[...], preferred_element_type=F32)
    h_ref[...] = h
    xnt = _rms(h, g2_ref[...]).T.astype(BF16)
    for k in range(xnt_ref.shape[0]):
        xnt_ref[k] = xnt[:, k * _PEER_T:(k + 1) * _PEER_T]


_PEER_T = 256


def _merge(x, yc, yh, ya, z, wb, wo, g2, tm):
    n, d = x.shape
    assert tm % _PEER_T == 0
    br = pl.BlockSpec((tm, BRANCH_W), lambda i: (i, 0))
    return pl.pallas_call(
        _merge_kernel,
        grid=(n // tm,),
        in_specs=[
            pl.BlockSpec((tm, d), lambda i: (i, 0)), br, br, br,
            pl.BlockSpec((tm, 3 * d), lambda i: (i, 0)),
            pl.BlockSpec((3, BRANCH_W, d), lambda i: (0, 0, 0)),
            pl.BlockSpec((d, d), lambda i: (0, 0)),
            pl.BlockSpec((1, d), lambda i: (0, 0)),
        ],
        out_specs=[pl.BlockSpec((tm, d), lambda i: (i, 0)),
                   pl.BlockSpec((tm // _PEER_T, d, _PEER_T), lambda i: (i, 0, 0))],
        out_shape=[SDS((n, d), F32), SDS((n // _PEER_T, d, _PEER_T), BF16)],
        compiler_params=_params("arbitrary"),
        name="merge",
    )(x, yc, yh, ya, z, wb, wo, g2)


def _take_max(work, row_iota, n_rows):
    m = jnp.max(work, axis=0, keepdims=True)
    first = jnp.min(jnp.where(work == m, row_iota, float(n_rows)), axis=0, keepdims=True)
    return m, jnp.where(row_iota == first, -jnp.inf, work)


def _oddeven_merge_sort_pairs(n):
    pairs, p = [], 1
    while p < n:
        k = p
        while k >= 1:
            for j in range(k % p, n - k, 2 * k):
                for i in range(min(k, n - j - k)):
                    if (i + j) // (2 * p) == (i + j + k) // (2 * p):
                        pairs.append((i + j, i + j + k))
            k //= 2
        p *= 2
    return tuple(pairs)


_SORT_PAIRS = _oddeven_merge_sort_pairs(PEER_TOPK)


def _exchange(v, a, b):
    v[a], v[b] = jnp.maximum(v[a], v[b]), jnp.minimum(v[a], v[b])


def _top16_sorted(s):
    v = [s[_SUBLANES * i:_SUBLANES * (i + 1), :] for i in range(PEER_TOPK)]
    for a, b in _SORT_PAIRS:
        _exchange(v, a, b)
    for shift in (4, 6, 7):
        w = [pltpu.roll(x, shift, 0) for x in v]
        v = [jnp.maximum(v[k], w[PEER_TOPK - 1 - k]) for k in range(PEER_TOPK)]
        d = PEER_TOPK // 2
        while d >= 1:
            for i in range(PEER_TOPK):
                if not i & d:
                    _exchange(v, i, i + d)
            d //= 2
    return [x[0:1, :] for x in v]


_N_CAND = -(-len(_PAIRS) // 8) * 8
_ROUTE_GROUP = 2


def _peer_route_kernel(xnt_ref, wqt_ref, keys_ref, cut1_ref, e1_ref, rank2_ref, e2_ref, cand_ref, *, t):
    g = xnt_ref.shape[0]
    xnt = xnt_ref[0] if g == 1 else jnp.concatenate([xnt_ref[k] for k in range(g)], axis=1)
    tile_t, t = t, t * g
    n_cand = _N_CAND
    cand_iota = lax.broadcasted_iota(jnp.int32, (n_cand, t), 0).astype(F32)
    cand_ref[len(_PAIRS):, :] = jnp.full((n_cand - len(_PAIRS), t), -jnp.inf, F32)

    def head(h, carry):
        dk = 2 * N_KEYS
        qt = jnp.dot(wqt_ref[pl.ds(pl.multiple_of(h * dk, dk), dk), :], xnt, preferred_element_type=F32)
        scores, tops = [], []
        for p in range(2):
            s = jnp.dot(keys_ref[h, p], qt[p * N_KEYS:(p + 1) * N_KEYS].astype(BF16), preferred_element_type=F32)
            scores.append(s)
            tops.append(_top16_sorted(s))
        for i, (a, b) in enumerate(_PAIRS):
            cand_ref[i:i + 1, :] = tops[0][a] + tops[1][b]
        work = cand_ref[...]
        best = []
        for _ in range(PEER_TOPK):
            m, work = _take_max(work, cand_iota, n_cand)
            best.append(m)
        norm = jnp.zeros((1, t), F32)
        for m in best:
            norm = norm + jnp.exp(m - best[0])
        cut = jnp.zeros((N_KEYS, t), F32)
        rank = jnp.full((N_KEYS, t), float(PEER_TOPK), F32)
        for b in range(PEER_TOPK):
            cut = jnp.where(scores[0] + tops[1][b] >= best[-1], float(b + 1), cut)
            rb = PEER_TOPK - 1 - b
            rank = jnp.where(scores[1] >= tops[1][rb], float(rb), rank)
        e1 = jnp.exp(scores[0] - tops[0][0]) / norm
        rank = rank.astype(BF16)
        e2 = jnp.exp(scores[1] - tops[1][0]).astype(BF16)
        for k in range(g):
            lanes = slice(k * tile_t, (k + 1) * tile_t)
            cut1_ref[k, h] = cut[:, lanes]
            e1_ref[k, h] = e1[:, lanes]
            rank2_ref[k, h] = rank[:, lanes]
            e2_ref[k, h] = e2[:, lanes]
        return carry

    lax.fori_loop(0, PEER_HEADS, head, 0)


def _peer_route(xnt, wqt, keys):
    nt, d, t = xnt.shape
    g = _ROUTE_GROUP if nt % _ROUTE_GROUP == 0 else 1
    big = pl.BlockSpec((g, PEER_HEADS, N_KEYS, t), lambda i: (i, 0, 0, 0))
    wide, narrow = SDS((nt, PEER_HEADS, N_KEYS, t), F32), SDS((nt, PEER_HEADS, N_KEYS, t), BF16)
    return pl.pallas_call(
        functools.partial(_peer_route_kernel, t=t),
        grid=(nt // g,),
        in_specs=[
            pl.BlockSpec((g, d, t), lambda i: (i, 0, 0)),
            pl.BlockSpec(wqt.shape, lambda i: (0, 0)),
            pl.BlockSpec(keys.shape, lambda i: (0, 0, 0, 0)),
        ],
        out_specs=[big, big, big, big],
        out_shape=[wide, wide, narrow, narrow],
        scratch_shapes=[pltpu.VMEM((_N_CAND, g * t), F32)],
        compiler_params=_params("arbitrary"),
        name="peer_route",
    )(xnt, wqt, keys)


_BF16_ROWS = 16
_PEER_CHUNK = 2048


def _peer_dense_kernel(xnt_ref, h_ref, u_ref, vt_ref, cut1_ref, e1_ref, rank2_ref, e2_ref, *rest, t, ec):
    gain_ref = rest[0] if len(rest) == 8 else None
    o_ref, acc_ref, act_ref, cutx_ref, e1x_ref, rank2x_ref, e2x_ref = rest[-7:]
    c = pl.program_id(1)
    tile = pl.program_id(2)
    n_chunks = pl.num_programs(1) - 1

    @pl.when(c == 0)
    def _():
        acc_ref[tile] = jnp.zeros(acc_ref.shape[1:], F32)
        act_ref[tile, 1] = jnp.zeros((ec, t), BF16)
        rank2x_ref[tile] = rank2_ref[tile]
        e2x_ref[tile] = e2_ref[tile]

    acc_ref[tile] += jnp.dot(vt_ref[...], act_ref[tile, (c + 1) & 1], preferred_element_type=F32)

    chunk = jnp.minimum(c, n_chunks - 1)
    n_i1 = ec // N_KEYS
    slot = c & 1
    rep = N_KEYS // _BF16_ROWS
    for ii in range(n_i1):
        row = pl.ds(chunk * n_i1 + ii, 1)
        for h in range(PEER_HEADS):
            cutx_ref[h, ii] = jnp.broadcast_to(cut1_ref[tile, h, row, :], (_BF16_ROWS, t)).astype(BF16)
            e1x_ref[h, ii] = jnp.broadcast_to(e1_ref[tile, h, row, :], (_BF16_ROWS, t)).astype(BF16)
    xnt = xnt_ref[tile]
    for ii in range(n_i1):
        rows = slice(ii * N_KEYS, (ii + 1) * N_KEYS)
        hid = jnp.dot(u_ref[rows, :], xnt, preferred_element_type=F32)
        wsum = jnp.zeros((N_KEYS, t), BF16)
        for h in range(PEER_HEADS):
            cut = jnp.concatenate([cutx_ref[h, ii]] * rep, axis=0)
            e1 = jnp.concatenate([e1x_ref[h, ii]] * rep, axis=0)
            wsum = wsum + jnp.where(rank2x_ref[tile, h] < cut, e1 * e2x_ref[tile, h], jnp.zeros((), BF16))
        gelu = 0.5 * hid * (1.0 + lax.erf(hid * (2.0 ** -0.5)))
        act_ref[tile, slot, rows, :] = gelu.astype(BF16) * wsum

    @pl.when(c == n_chunks)
    def _():
        tok = pl.ds(pl.multiple_of(tile * t, t), t)
        out = h_ref[tok, :] + acc_ref[tile].T
        o_ref[tok, :] = out if gain_ref is None else _rms(out, gain_ref[...])


_PEER_GROUP = 2


def _peer_dense(xnt, h, u, vt, route, ec, final_gain=None):
    nt, d, t = xnt.shape
    extra = [] if final_gain is None else [final_gain]
    g = _PEER_GROUP if nt % _PEER_GROUP == 0 else 1
    nc = u.shape[0] // ec
    n_i1 = ec // N_KEYS
    big = pl.BlockSpec((g, PEER_HEADS, N_KEYS, t), lambda i, c, k: (i, 0, 0, 0))
    tokens = pl.BlockSpec((g * t, d), lambda i, c, k: (i, 0))
    return pl.pallas_call(
        functools.partial(_peer_dense_kernel, t=t, ec=ec),
        grid=(nt // g, nc + 1, g),
        in_specs=[
            pl.BlockSpec((g, d, t), lambda i, c, k: (i, 0, 0)),
            tokens,
            pl.BlockSpec((ec, d), lambda i, c, k: (jnp.minimum(c, nc - 1), 0)),
            pl.BlockSpec((d, ec), lambda i, c, k: (0, jnp.maximum(c - 1, 0))),
            big, big, big, big,
        ] + [pl.BlockSpec((1, d), lambda i, c, k: (0, 0))] * len(extra),
        out_specs=tokens,
        out_shape=SDS((nt * t, d), F32),
        scratch_shapes=[pltpu.VMEM((g, d, t), F32), pltpu.VMEM((g, 2, ec, t), BF16),
                        pltpu.VMEM((PEER_HEADS, n_i1, _BF16_ROWS, t), BF16),
                        pltpu.VMEM((PEER_HEADS, n_i1, _BF16_ROWS, t), BF16),
                        pltpu.VMEM((g, PEER_HEADS, N_KEYS, t), BF16), pltpu.VMEM((g, PEER_HEADS, N_KEYS, t), BF16)],
        compiler_params=_params("arbitrary", "arbitrary", "arbitrary", vmem_mb=56),
        name="peer_dense",
    )(xnt, h, u, vt, *route, *extra)


def _tile(n, pref):
    return pref if n % pref == 0 else n


def _layer_weights(l, lbs, norm1_g, w_in, conv_dw, conv_db, conv_ln_g, conv_ln_b, hg_norm_g, att_fb,
                   w_branch, w_out, norm2_g, peer_wq, peer_keys, peer_u, peer_v):
    n_main = 2 * BRANCH_W + 4 * HG_HEADS * HG_DK + 3 * ATT_HEADS * ATT_HD
    w = w_in[l]
    row = lambda a: a.reshape(1, -1).astype(F32)
    return dict(
        g1=row(norm1_g[l]),
        w_main=jnp.concatenate([w[:, n_main + ATT_HEADS:], w[:, :n_main]], axis=1).astype(BF16),
        w_f=jnp.pad(w[:, n_main:n_main + ATT_HEADS], ((0, 0), (0, LANES - ATT_HEADS))).astype(BF16),
        fb=jnp.pad(att_fb[l].astype(F32), (0, LANES - ATT_HEADS)).reshape(1, LANES),
        conv_w=conv_dw[l].astype(F32), conv_b=row(conv_db[l]), ln_g=row(conv_ln_g[l]), ln_b=row(conv_ln_b[l]),
        lb=row(lbs[l]), hg_g=row(hg_norm_g[l]),
        wb=w_branch[l].astype(BF16), wo=w_out[l].astype(BF16), g2=row(norm2_g[l]),
        wqt=peer_wq[l].T.astype(BF16), keys=peer_keys[l].astype(BF16),
        u=peer_u[l].astype(BF16), vt=peer_v[l].T.astype(BF16),
    )


def _token_mixers(x, wt, nb, seq, conv_buf, hg_state, attend, final_gain=None):
    n = nb * seq
    tm = _tile(n, 1024)
    w = ATT_HEADS * ATT_HD
    if seq % tm == 0:
        z, lf, kt, vt = _in_proj(x, wt["g1"], wt["w_main"], wt["w_f"], wt["fb"], tm, seq)
        k_rows, v_rows = (a.reshape(nb, ATT_HEADS, ATT_HD, seq).transpose(0, 3, 1, 2) for a in (kt, vt))
    else:
        z, lf = _in_proj(x, wt["g1"], wt["w_main"], wt["w_f"], wt["fb"], tm)
        k_rows, v_rows = (z[:, zb * BRANCH_W:zb * BRANCH_W + w].reshape(nb, seq, ATT_HEADS, ATT_HD)
                          for zb in (ZB_AK, ZB_AV))
    y_conv, conv_new = _conv_branch(z, conv_buf, wt["conv_w"], wt["conv_b"], wt["ln_g"], wt["ln_b"],
                                    nb, seq, _tile(seq, 512))
    y_hgrn, hg_new = _hgrn_branch(z, hg_state, wt["lb"], wt["hg_g"], nb, seq, _tile(seq, 256), _tile(seq, 16))
    y_att = attend(z, lf)
    h, xn = _merge(x, y_conv, y_hgrn, y_att, z, wt["wb"], wt["wo"], wt["g2"], _tile(n, 512))
    route = _peer_route(xn, wt["wqt"], wt["keys"])
    out = _peer_dense(xn, h, wt["u"], wt["vt"], route, _PEER_CHUNK, final_gain)
    logf = lf[:, :ATT_HEADS].reshape(nb, seq, ATT_HEADS)
    return out, conv_new, hg_new, k_rows, v_rows, logf


def kernel(x_prompt, x_sample, cache_k, cache_v, cache_logf, state_conv, state_hgrn, page_table, norm1_g, w_in, conv_dw, conv_db, conv_ln_g, conv_ln_b, hg_lb_logits, hg_norm_g, att_fb, w_branch, w_out, norm2_g, peer_wq, peer_keys, peer_u, peer_v, final_g):
    depth = w_in.shape[0]
    bp, sp, d = x_prompt.shape
    bs, ss, _ = x_sample.shape
    w = ATT_HEADS * ATT_HD
    probs = jax.nn.softmax(hg_lb_logits.astype(F32), axis=0)
    lbs = jnp.cumsum(probs, axis=0) - probs[0:1]
    n_phys = cache_k.shape[1]
    ckt = jnp.transpose(cache_k, (0, 1, 3, 4, 2)).reshape(depth * n_phys, ATT_HEADS, ATT_HD, PAGE_SIZE)
    cvt = jnp.transpose(cache_v, (0, 1, 3, 4, 2)).reshape(depth * n_phys, ATT_HEADS, ATT_HD, PAGE_SIZE)
    lf_rows = jnp.swapaxes(cache_logf.astype(F32), 2, 3).reshape(depth * n_phys * ATT_HEADS, PAGE_SIZE)
    suffix, total = (a.reshape(depth * n_phys, ATT_HEADS, PAGE_SIZE)
                     for a in _page_suffix(lf_rows, _tile(lf_rows.shape[0], 2048)))

    hp = x_prompt.reshape(bp * sp, d)
    hs = x_sample.reshape(bs * ss, d)
    outs_p, outs_s = [], []
    for l in range(depth):
        wt = _layer_weights(l, lbs, norm1_g, w_in, conv_dw, conv_db, conv_ln_g, conv_ln_b, hg_norm_g, att_fb,
                            w_branch, w_out, norm2_g, peer_wq, peer_keys, peer_u, peer_v)

        def attend_prompt(z, lf):
            ct = _logf_cumsum(lf, bp, sp, _tile(sp, 512))
            return _fox_prompt(z, ct, bp, sp, _tile(sp, _FOX_BLOCK))

        def attend_sample(z, lf, l=l):
            lfn_t = jnp.swapaxes(lf[:, :ATT_HEADS].reshape(bs, ss, ATT_HEADS), 1, 2)
            lfn_t = jnp.pad(lfn_t, ((0, 0), (0, 0), (0, PAGE_SIZE - ss)))
            return _fox_sample(z, lfn_t, ckt, cvt, suffix, total, page_table, l * n_phys, bs, ss)

        fg = final_g.reshape(1, d).astype(F32) if l == depth - 1 else None
        hp, *rp = _token_mixers(hp, wt, bp, sp, jnp.zeros((bp, CONV_K - 1, BRANCH_W), F32),
                                jnp.zeros((bp, HG_HEADS, HG_DK, HG_DK), F32), attend_prompt, fg)
        hs, *rs = _token_mixers(hs, wt, bs, ss, state_conv[l], state_hgrn[l], attend_sample, fg)
        outs_p.append(rp)
        outs_s.append(rs)

    y_prompt = hp.reshape(bp, sp, d)
    y_sample = hs.reshape(bs, ss, d)
    stack = lambda outs, i: jnp.stack([r[i] for r in outs])
    return (y_prompt, y_sample,
            stack(outs_p, 2), stack(outs_p, 3), stack(outs_p, 4), stack(outs_p, 0), stack(outs_p, 1),
            stack(outs_s, 2), stack(outs_s, 3), stack(outs_s, 4), stack(outs_s, 0), stack(outs_s, 1))
```
